```python
import math
import jax, jax.numpy as jnp
from jax import lax
import numpy as np

D_MODEL = 1024
BATCH = 16
SEQ = 256
DEPTH = 4
DEC_BATCH = 8
DEC_SEQ = 1024
PAST_LEN = 256

GRID_W = 64
Q_BLOCK = 128
ROPE_BASE = 10000.0
NORM_EPS = 1e-6

MLA_HEADS = 8
MLA_NOPE = 64
MLA_ROPE = 32
MLA_QK = MLA_NOPE + MLA_ROPE
MLA_V = 64
Q_LORA = 384
KV_LORA = 256
D_RNN = 512
LRU_BLOCKS = 8
LRU_BW = D_RNN // LRU_BLOCKS
CONV_W = 4
CONV_LEFT = 2
LRU_C = 8.0
DIFF_HEADS = 4
DIFF_DH = 64
DIFF_DV = 2 * DIFF_DH
N_EXPERTS = 32
TOP_K = 4
D_EXPERT = 512
SWIGLU_LIMIT = 7.0
SWIGLU_ALPHA = 1.702
N_BRANCH = 3

OFF_QA = 0
OFF_KVA = OFF_QA + Q_LORA
OFF_LRU_X = OFF_KVA + KV_LORA + MLA_ROPE
OFF_LRU_G = OFF_LRU_X + D_RNN
OFF_DQ = OFF_LRU_G + D_RNN
OFF_DK = OFF_DQ + DIFF_HEADS * 2 * DIFF_DH
OFF_DV = OFF_DK + DIFF_HEADS * 2 * DIFF_DH
IN_COLS = OFF_DV + DIFF_HEADS * DIFF_DV

kernel_name = 'hybrid_mla_rglru_diffattn_moe_dit_step'


def _rmsnorm(x, g):
    xf = x.astype(jnp.float32)
    y = xf * lax.rsqrt(jnp.mean(xf * xf, axis=-1, keepdims=True) + NORM_EPS)
    return (y * g.astype(jnp.float32)).astype(x.dtype)


def _modulation(cond, w, b):
    m = jax.nn.silu(cond) @ w + b
    return jnp.split(m[:, None, :], 6, axis=-1)


def _modulate(x, g, shift, scale):
    return _rmsnorm(x, g) * (1.0 + scale) + shift


def _rot_half(x):
    x1, x2 = jnp.split(x, 2, axis=-1)
    return jnp.concatenate([-x2, x1], axis=-1)


def _axial_rope_tables(n_tokens, dim):
    rows = n_tokens // GRID_W
    row = jnp.repeat(jnp.arange(rows), GRID_W)
    col = jnp.tile(jnp.arange(GRID_W), rows)
    half = dim // 2
    inv = 1.0 / (ROPE_BASE ** (jnp.arange(0, half, 2, dtype=jnp.float32) / half))

    def axis_angles(pos):
        ang = pos.astype(jnp.float32)[:, None] * inv[None, :]
        return jnp.concatenate([ang, ang], axis=-1)

    ang = jnp.concatenate([axis_angles(row), axis_angles(col)], axis=-1)
    return jnp.cos(ang), jnp.sin(ang)


def _apply_rope(x, cos, sin):
    shape = (x.shape[1],) + (1,) * (x.ndim - 3) + (x.shape[-1],)
    cos = cos.reshape(shape).astype(x.dtype)
    sin = sin.reshape(shape).astype(x.dtype)
    half = x.shape[-1] // 2
    rot = jnp.concatenate([_rot_half(x[..., :half]), _rot_half(x[..., half:])], axis=-1)
    return x * cos + rot * sin


def _rope_tail(x, cos, sin, n_plain):
    return jnp.concatenate([x[..., :n_plain], _apply_rope(x[..., n_plain:], cos, sin)], axis=-1)


def _over_query_blocks(fn, q):
    b, s = q.shape[0], q.shape[1]
    nb = s // Q_BLOCK
    qb = jnp.moveaxis(q.reshape((b, nb, Q_BLOCK) + q.shape[2:]), 1, 0)
    out = jnp.moveaxis(lax.map(fn, qb), 0, 1)
    return out.reshape((b, s) + out.shape[3:])


def _mla_attend(q, k, v):
    scale = MLA_QK ** -0.5

    def block(qb):
        s = jnp.einsum('bqhd,bkhd->bhqk', qb, k).astype(jnp.float32) * scale
        p = jax.nn.softmax(s, axis=-1).astype(v.dtype)
        return jnp.einsum('bhqk,bkhe->bqhe', p, v)

    return _over_query_blocks(block, q)


def _diff_attend(q, k, v, lam):
    scale = DIFF_DH ** -0.5

    def block(qb):
        s = jnp.einsum('bqhcd,bkhcd->bhcqk', qb, k).astype(jnp.float32) * scale
        p = jax.nn.softmax(s, axis=-1)
        pd = (p[:, :, 0] - lam * p[:, :, 1]).astype(v.dtype)
        return jnp.einsum('bhqk,bkhe->bqhe', pd, v)

    return _over_query_blocks(block, q)


def _mla_query(proj, p):
    b, n = proj.shape[:2]
    q_a = _rmsnorm(proj[..., OFF_QA:OFF_KVA], p['mla_qa_norm'])
    q = (q_a @ p['w_q_b']).reshape(b, n, MLA_HEADS, MLA_QK)
    return _rmsnorm(q, p['mla_qn'])


def _mla_compress(proj, p):
    kv_a = proj[..., OFF_KVA:OFF_LRU_X]
    c_kv = _rmsnorm(kv_a[..., :KV_LORA], p['mla_kva_norm'])
    return c_kv, kv_a[..., KV_LORA:]


def _mla_expand(c_kv, k_rope, p):
    b, n = c_kv.shape[:2]
    kv = (c_kv @ p['w_kv_b']).reshape(b, n, MLA_HEADS, MLA_NOPE + MLA_V)
    k_pe = jnp.broadcast_to(k_rope[:, :, None, :], (b, n, MLA_HEADS, MLA_ROPE))
    k = _rmsnorm(jnp.concatenate([kv[..., :MLA_NOPE], k_pe], axis=-1), p['mla_kn'])
    return k, kv[..., MLA_NOPE:]


def _mla_out(o, p):
    b, n = o.shape[:2]
    return o.reshape(b, n, MLA_HEADS * MLA_V) @ p['w_o_mla']


def _centred_dwconv(x, w, bias):
    n = x.shape[1]
    xp = jnp.pad(x, ((0, 0), (CONV_LEFT, CONV_W - 1 - CONV_LEFT), (0, 0)))
    out = bias
    for j in range(CONV_W):
        out = out + xp[:, j:j + n] * w[j]
    return out


def _linear_scan(a, bx, h0, reverse):
    def step(h, inp):
        a_t, b_t = inp
        h = a_t * h + b_t
        return h, h

    h_last, hs = lax.scan(step, h0, (jnp.swapaxes(a, 0, 1), jnp.swapaxes(bx, 0, 1)), reverse=reverse)
    return jnp.swapaxes(hs, 0, 1), h_last


def _rglru(x, p, h0):
    b, n = x.shape[:2]
    xf = x.astype(jnp.float32)
    xb = xf.reshape(b, n, LRU_BLOCKS, LRU_BW)
    outs, finals = [], []
    for d in range(2):
        gates = jnp.einsum('bnkc,gkcd->gbnkd', xb, p['lru_gate_w'][d].astype(jnp.float32))
        gates = gates.reshape(2, b, n, D_RNN) + p['lru_gate_b'][d].astype(jnp.float32)[:, None, None, :]
        r = jax.nn.sigmoid(gates[0])
        i = jax.nn.sigmoid(gates[1])
        log_a = -LRU_C * r * jax.nn.softplus(-p['lru_lambda'][d].astype(jnp.float32))
        a = jnp.exp(log_a)
        bx = jnp.sqrt(-jnp.expm1(2.0 * log_a)) * (i * xf)
        h, h_last = _linear_scan(a, bx, h0[:, d].astype(jnp.float32), reverse=(d == 1))
        outs.append(h)
        finals.append(h_last)
    return (outs[0] + outs[1]).astype(x.dtype), jnp.stack(finals, axis=1).astype(x.dtype)


def _lru_out(y_r, proj, p):
    return (y_r * jax.nn.gelu(proj[..., OFF_LRU_G:OFF_DQ])) @ p['w_o_lru']


def _diff_qkv(proj, p):
    b, n = proj.shape[:2]
    q = _rmsnorm(proj[..., OFF_DQ:OFF_DK].reshape(b, n, DIFF_HEADS, 2, DIFF_DH), p['diff_qn'])
    k = _rmsnorm(proj[..., OFF_DK:OFF_DV].reshape(b, n, DIFF_HEADS, 2, DIFF_DH), p['diff_kn'])
    v = proj[..., OFF_DV:IN_COLS].reshape(b, n, DIFF_HEADS, DIFF_DV)
    return q, k, v


def _diff_lambda(p, lam_init):
    lam = p['diff_lambda'].astype(jnp.float32)
    return jnp.exp(jnp.sum(lam[0] * lam[1])) - jnp.exp(jnp.sum(lam[2] * lam[3])) + lam_init


def _diff_out(o, p, lam_init):
    b, n = o.shape[:2]
    o = _rmsnorm(o, p['diff_subln']) * (1.0 - lam_init)
    return o.reshape(b, n, DIFF_HEADS * DIFF_DV) @ p['w_o_diff']


def _merge(h, o_mla, o_lru, o_diff, p):
    b, n = h.shape[:2]
    g = jax.nn.sigmoid(h @ p['w_merge'] + p['b_merge']).reshape(b, n, N_BRANCH, D_MODEL)
    merged = g[:, :, 0] * o_mla + g[:, :, 1] * o_lru + g[:, :, 2] * o_diff
    return merged @ p['w_out']


def _context_mixers(h, p, lam_init):
    proj = h @ p['w_in']
    q = _mla_query(proj, p)
    c_kv, k_rope = _mla_compress(proj, p)
    k, v = _mla_expand(c_kv, k_rope, p)
    o_mla = _mla_out(_mla_attend(q, k, v), p)
    x_r = _centred_dwconv(proj[..., OFF_LRU_X:OFF_LRU_G], p['lru_conv_w'], p['lru_conv_b'])
    h0 = jnp.zeros((h.shape[0], 2, D_RNN), h.dtype)
    y_r, lru_state = _rglru(x_r, p, h0)
    o_lru = _lru_out(y_r, proj, p)
    qd, kd, vd = _diff_qkv(proj, p)
    o_diff = _diff_out(_diff_attend(qd, kd, vd, _diff_lambda(p, lam_init)), p, lam_init)
    return _merge(h, o_mla, o_lru, o_diff, p), (c_kv, k_rope, lru_state, kd, vd)


def _latent_mixers(h, p, lam_init, c_kv_ctx, k_rope_ctx, lru_ctx, kd_ctx, vd_ctx):
    n = h.shape[1]
    proj = h @ p['w_in']
    cos_m, sin_m = _axial_rope_tables(n, MLA_ROPE)
    cos_d, sin_d = _axial_rope_tables(n, DIFF_DH)
    q = _rope_tail(_mla_query(proj, p), cos_m, sin_m, MLA_NOPE)
    c_kv, k_rope = _mla_compress(proj, p)
    k_lat, v_lat = _mla_expand(c_kv, k_rope, p)
    k_lat = _rope_tail(k_lat, cos_m, sin_m, MLA_NOPE)
    k_ctx, v_ctx = _mla_expand(c_kv_ctx, k_rope_ctx, p)
    k = jnp.concatenate([k_ctx, k_lat], axis=1)
    v = jnp.concatenate([v_ctx, v_lat], axis=1)
    o_mla = _mla_out(_mla_attend(q, k, v), p)
    x_r = _centred_dwconv(proj[..., OFF_LRU_X:OFF_LRU_G], p['lru_conv_w'], p['lru_conv_b'])
    y_r, _ = _rglru(x_r, p, lru_ctx)
    o_lru = _lru_out(y_r, proj, p)
    qd, kd, vd = _diff_qkv(proj, p)
    qd = _apply_rope(qd, cos_d, sin_d)
    kd = jnp.concatenate([kd_ctx, _apply_rope(kd, cos_d, sin_d)], axis=1)
    vd = jnp.concatenate([vd_ctx, vd], axis=1)
    o_diff = _diff_out(_diff_attend(qd, kd, vd, _diff_lambda(p, lam_init)), p, lam_init)
    return _merge(h, o_mla, o_lru, o_diff, p)


def _moe(h, p):
    b, n, d = h.shape
    t = h.reshape(b * n, d)
    logits = (t @ p['router_w'] + p['router_b']).astype(jnp.float32)
    top_v, top_i = lax.top_k(logits, TOP_K)
    w_top = jax.nn.softmax(top_v, axis=-1)
    combine = jnp.einsum('tk,tke->te', w_top, jax.nn.one_hot(top_i, N_EXPERTS, dtype=jnp.float32)).astype(t.dtype)

    def expert(acc, e):
        w1, b1, w2, b2, ce = e
        gu = t @ w1 + b1
        gate = jnp.minimum(gu[:, :D_EXPERT], SWIGLU_LIMIT)
        up = jnp.clip(gu[:, D_EXPERT:], -SWIGLU_LIMIT, SWIGLU_LIMIT)
        act = (up + 1.0) * (gate * jax.nn.sigmoid(SWIGLU_ALPHA * gate))
        return acc + ce[:, None] * (act @ w2 + b2), None

    y, _ = lax.scan(expert, jnp.zeros_like(t), (p['exp_w1'], p['exp_b1'], p['exp_w2'], p['exp_b2'], combine.T))
    return y.reshape(b, n, d)


def setup_inputs(seed: int = 0) -> dict:
    key = jax.random.key(seed)
    ks = iter(jax.random.split(key, 48))
    L, D = DEPTH, D_MODEL

    def nrm(shape, scale=1.0):
        return jax.random.normal(next(ks), shape, jnp.float32) * scale

    def gain(shape):
        return 1.0 + nrm(shape, 0.02)

    a0 = jax.random.uniform(next(ks), (L, 2, D_RNN), jnp.float32, 0.9, 0.999)
    return {
        'x_prompt': nrm((BATCH, SEQ, D)),
        'x_sample': nrm((DEC_BATCH, DEC_SEQ, D)),
        'cache_mla_ckv': nrm((DEC_BATCH, L, PAST_LEN, KV_LORA)),
        'cache_mla_krope': nrm((DEC_BATCH, L, PAST_LEN, MLA_ROPE)),
        'state_lru': nrm((DEC_BATCH, L, 2, D_RNN), 0.5),
        'cache_diff_k': nrm((DEC_BATCH, L, PAST_LEN, DIFF_HEADS, 2, DIFF_DH)),
        'cache_diff_v': nrm((DEC_BATCH, L, PAST_LEN, DIFF_HEADS, DIFF_DV)),
        'c': nrm((DEC_BATCH, D)),
        'c_ctx': nrm((D,)),
        'ada_w': nrm((L, D, 6 * D), 0.5 * D ** -0.5),
        'ada_b': nrm((L, 6 * D), 0.01),
        'norm1_g': gain((L, D)),
        'norm2_g': gain((L, D)),
        'w_in': nrm((L, D, IN_COLS), D ** -0.5),
        'mla_qa_norm': gain((L, Q_LORA)),
        'w_q_b': nrm((L, Q_LORA, MLA_HEADS * MLA_QK), Q_LORA ** -0.5),
        'mla_kva_norm': gain((L, KV_LORA)),
        'w_kv_b': nrm((L, KV_LORA, MLA_HEADS * (MLA_NOPE + MLA_V)), KV_LORA ** -0.5),
        'mla_qn': gain((L, MLA_QK)),
        'mla_kn': gain((L, MLA_QK)),
        'w_o_mla': nrm((L, MLA_HEADS * MLA_V, D), (MLA_HEADS * MLA_V) ** -0.5),
        'lru_conv_w': nrm((L, CONV_W, D_RNN), CONV_W ** -0.5),
        'lru_conv_b': nrm((L, D_RNN), 0.01),
        'lru_gate_w': nrm((L, 2, 2, LRU_BLOCKS, LRU_BW, LRU_BW), LRU_BW ** -0.5),
        'lru_gate_b': nrm((L, 2, 2, D_RNN), 0.01),
        'lru_lambda': jnp.log(a0) - jnp.log1p(-a0),
        'w_o_lru': nrm((L, D_RNN, D), D_RNN ** -0.5),
        'diff_qn': gain((L, DIFF_DH)),
        'diff_kn': gain((L, DIFF_DH)),
        'diff_lambda': nrm((L, 4, DIFF_DH), 0.1),
        'diff_subln': gain((L, DIFF_DV)),
        'w_o_diff': nrm((L, DIFF_HEADS * DIFF_DV, D), (DIFF_HEADS * DIFF_DV) ** -0.5),
        'w_merge': nrm((L, D, N_BRANCH * D), D ** -0.5),
        'b_merge': nrm((L, N_BRANCH * D), 0.01),
        'w_out': nrm((L, D, D), D ** -0.5),
        'router_w': nrm((L, D, N_EXPERTS), D ** -0.5),
        'router_b': nrm((L, N_EXPERTS), 0.01),
        'exp_w1': nrm((L, N_EXPERTS, D, 2 * D_EXPERT), D ** -0.5),
        'exp_b1': nrm((L, N_EXPERTS, 2 * D_EXPERT), 0.01),
        'exp_w2': nrm((L, N_EXPERTS, D_EXPERT, D), D_EXPERT ** -0.5),
        'exp_b2': nrm((L, N_EXPERTS, D), 0.01),
    }


def reference(x_prompt, x_sample, cache_mla_ckv, cache_mla_krope, state_lru, cache_diff_k, cache_diff_v,
              c, c_ctx, ada_w, ada_b, norm1_g, norm2_g, w_in, mla_qa_norm, w_q_b, mla_kva_norm, w_kv_b,
              mla_qn, mla_kn, w_o_mla, lru_conv_w, lru_conv_b, lru_gate_w, lru_gate_b, lru_lambda, w_o_lru,
              diff_qn, diff_kn, diff_lambda, diff_subln, w_o_diff, w_merge, b_merge, w_out,
              router_w, router_b, exp_w1, exp_b1, exp_w2, exp_b2):
    layer_params = {
        'ada_w': ada_w, 'ada_b': ada_b, 'norm1': norm1_g, 'norm2': norm2_g, 'w_in': w_in,
        'mla_qa_norm': mla_qa_norm, 'w_q_b': w_q_b, 'mla_kva_norm': mla_kva_norm, 'w_kv_b': w_kv_b,
        'mla_qn': mla_qn, 'mla_kn': mla_kn, 'w_o_mla': w_o_mla,
        'lru_conv_w': lru_conv_w, 'lru_conv_b': lru_conv_b, 'lru_gate_w': lru_gate_w,
        'lru_gate_b': lru_gate_b, 'lru_lambda': lru_lambda, 'w_o_lru': w_o_lru,
        'diff_qn': diff_qn, 'diff_kn': diff_kn, 'diff_lambda': diff_lambda, 'diff_subln': diff_subln,
        'w_o_diff': w_o_diff, 'w_merge': w_merge, 'b_merge': b_merge, 'w_out': w_out,
        'router_w': router_w, 'router_b': router_b, 'exp_w1': exp_w1, 'exp_b1': exp_b1,
        'exp_w2': exp_w2, 'exp_b2': exp_b2,
    }
    xp, xs = x_prompt, x_sample
    new_ckv, new_krope, new_lru, new_dk, new_dv = [], [], [], [], []
    for l in range(DEPTH):
        p = {name: arr[l] for name, arr in layer_params.items()}
        lam_init = 0.8 - 0.6 * math.exp(-0.3 * l)
        sh1, sc1, g1, sh2, sc2, g2 = _modulation(c_ctx[None, :], p['ada_w'], p['ada_b'])
        mix, (ckv, krope, lst, dk, dv) = _context_mixers(_modulate(xp, p['norm1'], sh1, sc1), p, lam_init)
        xp = xp + g1 * mix
        xp = xp + g2 * _moe(_modulate(xp, p['norm2'], sh2, sc2), p)
        new_ckv.append(ckv)
        new_krope.append(krope)
        new_lru.append(lst)
        new_dk.append(dk)
        new_dv.append(dv)
        sh1, sc1, g1, sh2, sc2, g2 = _modulation(c, p['ada_w'], p['ada_b'])
        mix = _latent_mixers(_modulate(xs, p['norm1'], sh1, sc1), p, lam_init,
                             cache_mla_ckv[:, l], cache_mla_krope[:, l], state_lru[:, l],
                             cache_diff_k[:, l], cache_diff_v[:, l])
        xs = xs + g1 * mix
        xs = xs + g2 * _moe(_modulate(xs, p['norm2'], sh2, sc2), p)
    return (xp, xs, jnp.stack(new_ckv, axis=1), jnp.stack(new_krope, axis=1), jnp.stack(new_lru, axis=1),
            jnp.stack(new_dk, axis=1), jnp.stack(new_dv, axis=1))
```

```python
import functools
import math

import jax
import jax.numpy as jnp
from jax import lax
from jax.experimental import pallas as pl
from jax.experimental.pallas import tpu as pltpu

F32 = jnp.float32
BF16 = jnp.bfloat16

D_MODEL = 1024
BATCH = 16
SEQ = 256
DEPTH = 4
DEC_BATCH = 8
DEC_SEQ = 1024
PAST_LEN = 256
GRID_W = 64
ROPE_BASE = 10000.0
NORM_EPS = 1e-6

MLA_HEADS = 8
MLA_NOPE = 64
MLA_ROPE = 32
MLA_QK = MLA_NOPE + MLA_ROPE
MLA_V = 64
Q_LORA = 384
KV_LORA = 256
D_RNN = 512
LRU_BLOCKS = 8
LRU_BW = D_RNN // LRU_BLOCKS
CONV_W = 4
LRU_C = 8.0
DIFF_HEADS = 4
DIFF_DH = 64
DIFF_DV = 2 * DIFF_DH
N_EXPERTS = 32
TOP_K = 4
D_EXPERT = 512
SWIGLU_LIMIT = 7.0
SWIGLU_ALPHA = 1.702

OFF_QA = 0
OFF_KVA = OFF_QA + Q_LORA
OFF_LRU_X = OFF_KVA + KV_LORA + MLA_ROPE
OFF_LRU_G = OFF_LRU_X + D_RNN
OFF_DQ = OFF_LRU_G + D_RNN
OFF_DK = OFF_DQ + DIFF_HEADS * 2 * DIFF_DH
OFF_DV = OFF_DK + DIFF_HEADS * 2 * DIFF_DH
IN_COLS = OFF_DV + DIFF_HEADS * DIFF_DV

LANES = 128
TILE = 256
N_CTX_TILES = BATCH * SEQ // TILE
LAT_TILES_PER_SEQ = DEC_SEQ // TILE
N_LAT_TILES = DEC_BATCH * LAT_TILES_PER_SEQ
N_TILES = N_CTX_TILES + N_LAT_TILES
N_CTX_TOK = BATCH * SEQ
N_TOK = N_TILES * TILE
COND_ROWS = 16
CTX_COND_ROW = DEC_BATCH

P_QA = 0
P_CKV = P_QA + Q_LORA
P_KR = P_CKV + KV_LORA
P_LX = P_KR + LANES
P_LG = P_LX + D_RNN
P_DQ = P_LG + D_RNN
P_DK = P_DQ + 512
P_DV = P_DK + 512
P_COLS = P_DV + 512

MOE_TOK = 1024
N_MOE_BLOCKS = N_TOK // MOE_TOK
VMEM_LIMIT = 56 * 1024 * 1024


def _cond_row(i):
    return jnp.where(i < N_CTX_TILES, CTX_COND_ROW, (i - N_CTX_TILES) // LAT_TILES_PER_SEQ)


def _rope_idx(i):
    return jnp.where(i < N_CTX_TILES, 0, 1 + (i - N_CTX_TILES) % LAT_TILES_PER_SEQ)


def _ctx_out_idx(i):
    return jnp.minimum(i, N_CTX_TILES)


def _const_spec(shape, index):
    nd = len(index)
    return pl.BlockSpec(shape, lambda *_: index, pipeline_mode=pl.Buffered(1))


def _dot(a, b):
    return jnp.dot(a, b, preferred_element_type=F32)


def _dot_nt(a, b):
    return lax.dot_general(a, b, (((1,), (1,)), ((), ())), preferred_element_type=F32)


def _rms(x, n=None):
    n = x.shape[-1] if n is None else n
    ss = jnp.sum(x * x, axis=-1, keepdims=True)
    return x * lax.rsqrt(ss * (1.0 / n) + NORM_EPS)


def _rope(x, cos, sin_signed, half, first):
    rot = jnp.where(first, pltpu.roll(x, LANES - half, 1), pltpu.roll(x, half, 1))
    return x * cos + rot * sin_signed


def _lane_iota(shape):
    return lax.broadcasted_iota(jnp.int32, shape, len(shape) - 1)


def _ada_kernel(cond_ref, w_ref, b_ref, o_ref):
    c = cond_ref[...]
    s = c * jax.nn.sigmoid(c)
    o_ref[...] = _dot(s.astype(BF16), w_ref[...].astype(BF16)) + b_ref[...]


def _ada_call(cond, ada_w, ada_b3):
    cb = 1024
    return pl.pallas_call(
        _ada_kernel,
        grid=(DEPTH, 6 * D_MODEL // cb),
        in_specs=[
            pl.BlockSpec((COND_ROWS, D_MODEL), lambda l, j: (0, 0)),
            pl.BlockSpec((None, D_MODEL, cb), lambda l, j: (l, 0, j)),
            pl.BlockSpec((None, 1, cb), lambda l, j: (l, 0, j)),
        ],
        out_specs=pl.BlockSpec((None, COND_ROWS, cb), lambda l, j: (l, 0, j)),
        out_shape=jax.ShapeDtypeStruct((DEPTH, COND_ROWS, 6 * D_MODEL), F32),
        compiler_params=pltpu.CompilerParams(
            dimension_semantics=("arbitrary", "arbitrary"), vmem_limit_bytes=VMEM_LIMIT),
        name="ada_mod",
    )(cond, ada_w, ada_b3)


def _mla_k_heads(kn, krs, kn_gain, cos, sin, first, store):
    for hd in range(MLA_HEADS):
        kh = kn[:, hd * LANES:(hd + 1) * LANES] + krs
        kh = _rms(kh, MLA_QK) * kn_gain
        if cos is not None:
            kh = _rope(kh, cos, sin, MLA_ROPE // 4, first)
        store(hd, kh)


def _kvx_kernel(ckv_ref, kr_ref, wk_ref, wv_ref, kn_ref, place_ref, k_ref, v_ref):
    c = ckv_ref[...].astype(BF16)
    kn = _dot(c, wk_ref[...])
    v_ref[...] = _dot(c, wv_ref[...]).astype(BF16)
    krs = _dot(kr_ref[...].astype(BF16), place_ref[...])

    def store(hd, kh):
        k_ref[:, hd * LANES:(hd + 1) * LANES] = kh.astype(BF16)

    _mla_k_heads(kn, krs, kn_ref[...], None, None, None, store)


def _kvx_call(cache_ckv, cache_krope, wkvk, wkvv, kn_pad, place):
    return pl.pallas_call(
        _kvx_kernel,
        grid=(DEPTH, DEC_BATCH),
        in_specs=[
            pl.BlockSpec((None, None, PAST_LEN, KV_LORA), lambda l, b: (b, l, 0, 0)),
            pl.BlockSpec((None, None, PAST_LEN, MLA_ROPE), lambda l, b: (b, l, 0, 0)),
            pl.BlockSpec((None, KV_LORA, MLA_HEADS * LANES), lambda l, b: (l, 0, 0)),
            pl.BlockSpec((None, KV_LORA, MLA_HEADS * MLA_V), lambda l, b: (l, 0, 0)),
            pl.BlockSpec((None, 1, LANES), lambda l, b: (l, 0, 0)),
            pl.BlockSpec((MLA_ROPE, LANES), lambda l, b: (0, 0)),
        ],
        out_specs=[
            pl.BlockSpec((None, None, PAST_LEN, MLA_HEADS * LANES), lambda l, b: (l, b, 0, 0)),
            pl.BlockSpec((None, None, PAST_LEN, MLA_HEADS * MLA_V), lambda l, b: (l, b, 0, 0)),
        ],
        out_shape=[
            jax.ShapeDtypeStruct((DEPTH, DEC_BATCH, PAST_LEN, MLA_HEADS * LANES), BF16),
            jax.ShapeDtypeStruct((DEPTH, DEC_BATCH, PAST_LEN, MLA_HEADS * MLA_V), BF16),
        ],
        compiler_params=pltpu.CompilerParams(
            dimension_semantics=("arbitrary", "arbitrary"), vmem_limit_bytes=VMEM_LIMIT),
        name="ctx_kv_expand",
    )(cache_ckv, cache_krope, wkvk, wkvv, kn_pad, place)


def _modulated(x, gain, shift, scale):
    return _rms(x) * gain * (1.0 + scale) + shift


def _front_kernel(x_ref, mod_ref, n1_ref, win_ref, qan_ref, wqb_ref, kvan_ref, wkvk_ref, wkvv_ref,
                  qn_ref, kn_ref, dqn_ref, dkn_ref, cosm_ref, sinm_ref, cosd_ref, sind_ref,
                  qm_ref, km_ref, vm_ref, qd_ref, kd_ref, vd_ref, lx_ref, lg_ref,
                  ckv_ref, kro_ref, dko_ref, dvo_ref):
    x = x_ref[...]
    mod = mod_ref[...]
    h = _modulated(x, n1_ref[...], mod[:, 0:D_MODEL], mod[:, D_MODEL:2 * D_MODEL])
    hb = h.astype(BF16)

    lane = _lane_iota((1, LANES))
    first_m = (lane % (MLA_ROPE // 2)) < (MLA_ROPE // 4)
    first_d = (lane % (DIFF_DH // 2)) < (DIFF_DH // 4)
    cosm, sinm = cosm_ref[...], sinm_ref[...]
    cosd, sind = cosd_ref[...], sind_ref[...]

    qa = _rms(_dot(hb, win_ref[:, P_QA:P_QA + Q_LORA])) * qan_ref[...]
    q = _dot(qa.astype(BF16), wqb_ref[...])
    q_scale = MLA_QK ** -0.5
    for hd in range(MLA_HEADS):
        qh = _rms(q[:, hd * LANES:(hd + 1) * LANES], MLA_QK) * qn_ref[...]
        qh = _rope(qh, cosm, sinm, MLA_ROPE // 4, first_m)
        qm_ref[:, hd * LANES:(hd + 1) * LANES] = (qh * q_scale).astype(BF16)

    ckv = _rms(_dot(hb, win_ref[:, P_CKV:P_CKV + KV_LORA])) * kvan_ref[...]
    ckv_ref[...] = ckv
    krs = _dot(hb, win_ref[:, P_KR:P_KR + LANES])
    kro_ref[...] = krs[:, MLA_NOPE:MLA_NOPE + MLA_ROPE]
    cb = ckv.astype(BF16)
    vm_ref[...] = _dot(cb, wkvv_ref[...]).astype(BF16)
    kn = _dot(cb, wkvk_ref[...])

    def store_k(hd, kh):
        km_ref[:, hd * LANES:(hd + 1) * LANES] = kh.astype(BF16)

    _mla_k_heads(kn, krs, kn_ref[...], cosm, sinm, first_m, store_k)

    lx_ref[...] = _dot(hb, win_ref[:, P_LX:P_LX + D_RNN])
    lg_ref[...] = _dot(hb, win_ref[:, P_LG:P_LG + D_RNN])

    low = lane < DIFF_DH

    def pair_norm(t):
        sq = t * t
        s_all = jnp.sum(sq, axis=-1, keepdims=True)
        s_lo = jnp.sum(jnp.where(low, sq, 0.0), axis=-1, keepdims=True)
        r_lo = lax.rsqrt(s_lo * (1.0 / DIFF_DH) + NORM_EPS)
        r_hi = lax.rsqrt((s_all - s_lo) * (1.0 / DIFF_DH) + NORM_EPS)
        return t * jnp.where(low, r_lo, r_hi)

    dq = _dot(hb, win_ref[:, P_DQ:P_DQ + 512])
    dk = _dot(hb, win_ref[:, P_DK:P_DK + 512])
    d_scale = DIFF_DH ** -0.5
    for hd in range(DIFF_HEADS):
        sl = slice(hd * LANES, (hd + 1) * LANES)
        qh = _rope(pair_norm(dq[:, sl]) * dqn_ref[...], cosd, sind, DIFF_DH // 4, first_d)
        qd_ref[:, sl] = (qh * d_scale).astype(BF16)
        kh = _rope(pair_norm(dk[:, sl]) * dkn_ref[...], cosd, sind, DIFF_DH // 4, first_d)
        kd_ref[:, sl] = kh.astype(BF16)
        dko_ref[:, sl] = kh
    dv = _dot(hb, win_ref[:, P_DV:P_DV + 512])
    vd_ref[...] = dv.astype(BF16)
    dvo_ref[...] = dv


def _front_call(l, x, mod3, w):
    tok = lambda width: pl.BlockSpec((TILE, width), lambda i: (i, 0))
    ctx = lambda width: pl.BlockSpec((TILE, width), lambda i: (_ctx_out_idx(i), 0))
    lay = lambda *shape: _const_spec((None,) + shape, (l,) + (0,) * len(shape))
    rope = pl.BlockSpec((None, TILE, LANES), lambda i: (_rope_idx(i), 0, 0))
    n_ctx_rows = (N_CTX_TILES + 1) * TILE
    return pl.pallas_call(
        _front_kernel,
        grid=(N_TILES,),
        in_specs=[
            tok(D_MODEL),
            pl.BlockSpec((None, 1, 6 * D_MODEL), lambda i: (l * COND_ROWS + _cond_row(i), 0, 0)),
            lay(1, D_MODEL), lay(D_MODEL, P_COLS), lay(1, Q_LORA), lay(Q_LORA, MLA_HEADS * LANES),
            lay(1, KV_LORA), lay(KV_LORA, MLA_HEADS * LANES), lay(KV_LORA, MLA_HEADS * MLA_V),
            lay(1, LANES), lay(1, LANES), lay(1, LANES), lay(1, LANES),
            rope, rope, rope, rope,
        ],
        out_specs=[
            tok(1024), tok(1024), tok(512), tok(512), tok(512), tok(512), tok(512), tok(512),
            ctx(KV_LORA), ctx(MLA_ROPE), ctx(512), ctx(512),
        ],
        out_shape=[
            jax.ShapeDtypeStruct((N_TOK, 1024), BF16), jax.ShapeDtypeStruct((N_TOK, 1024), BF16),
            jax.ShapeDtypeStruct((N_TOK, 512), BF16), jax.ShapeDtypeStruct((N_TOK, 512), BF16),
            jax.ShapeDtypeStruct((N_TOK, 512), BF16), jax.ShapeDtypeStruct((N_TOK, 512), BF16),
            jax.ShapeDtypeStruct((N_TOK, 512), F32), jax.ShapeDtypeStruct((N_TOK, 512), F32),
            jax.ShapeDtypeStruct((n_ctx_rows, KV_LORA), F32),
            jax.ShapeDtypeStruct((n_ctx_rows, MLA_ROPE), F32),
            jax.ShapeDtypeStruct((n_ctx_rows, 512), F32),
            jax.ShapeDtypeStruct((n_ctx_rows, 512), F32),
        ],
        compiler_params=pltpu.CompilerParams(
            dimension_semantics=("arbitrary",), vmem_limit_bytes=VMEM_LIMIT),
        name="front",
    )(x, mod3, w["norm1"], w["w_in"], w["qa_norm"], w["wqb"], w["kva_norm"], w["wkvk"], w["wkvv"],
      w["qn"], w["kn"], w["dqn"], w["dkn"], w["cos_m"], w["sin_m"], w["cos_d"], w["sin_d"])


def _scan(a, b, h0, reverse):
    n = a.shape[0]
    row = lax.broadcasted_iota(jnp.int32, (n, 1), 0)
    d = 1
    while d < n:
        if reverse:
            valid = row < n - d
            shift = n - d
        else:
            valid = row >= d
            shift = d
        a_s = jnp.where(valid, pltpu.roll(a, shift, 0), 1.0)
        b_s = jnp.where(valid, pltpu.roll(b, shift, 0), 0.0)
        b = a * b_s + b
        a = a * a_s
        d *= 2
    return a * h0 + b


def _gelu_tanh(x):
    return 0.5 * x * (1.0 + jnp.tanh(math.sqrt(2.0 / math.pi) * (x + 0.044715 * x * x * x)))


def _lru_kernel(lx_ref, lg_ref, cw_ref, cb_ref, wbd_ref, gb_ref, lam_ref, h0_ref, u_ref, hf_ref):
    x = lx_ref[...]
    n = x.shape[0]
    row = lax.broadcasted_iota(jnp.int32, (n, 1), 0)
    cw = cw_ref[...]
    xr = cb_ref[...] + cw[2:3] * x
    xr = xr + cw[0:1] * jnp.where(row >= 2, pltpu.roll(x, 2, 0), 0.0)
    xr = xr + cw[1:2] * jnp.where(row >= 1, pltpu.roll(x, 1, 0), 0.0)
    xr = xr + cw[3:4] * jnp.where(row < n - 1, pltpu.roll(x, n - 1, 0), 0.0)
    xb = xr.astype(BF16)
    half = D_RNN // 2
    lam = lam_ref[...]
    h0 = h0_ref[...]
    gb = gb_ref[...]
    total = None
    for d in range(2):
        pre = []
        for g in range(2):
            k = d * 2 + g
            p = jnp.concatenate([_dot(xb[:, :half], wbd_ref[k, 0]), _dot(xb[:, half:], wbd_ref[k, 1])],
                                axis=-1)
            pre.append(p + gb[k:k + 1])
        r = jax.nn.sigmoid(pre[0])
        i = jax.nn.sigmoid(pre[1])
        z = -lam[d:d + 1]
        softplus = jnp.maximum(z, 0.0) + jnp.log(1.0 + jnp.exp(-jnp.abs(z)))
        a = jnp.exp(-LRU_C * r * softplus)
        bx = jnp.sqrt(1.0 - a * a) * (i * xr)
        h = _scan(a, bx, h0[d:d + 1], reverse=(d == 1))
        if d == 0:
            hf_ref[0:1, :] = h[n - 1:n, :]
            total = h
        else:
            hf_ref[1:2, :] = h[0:1, :]
            total = total + h
    u_ref[...] = (total * _gelu_tanh(lg_ref[...])).astype(BF16)


def _lru_call(l, lx, lg, h0, w, n_seq, seq_len, row_block0):
    seq = lambda: pl.BlockSpec((seq_len, D_RNN), lambda s: (row_block0 + s, 0))
    lay = lambda *shape: _const_spec((None,) + shape, (l,) + (0,) * len(shape))
    return pl.pallas_call(
        _lru_kernel,
        grid=(n_seq,),
        in_specs=[
            seq(), seq(), lay(CONV_W, D_RNN), lay(1, D_RNN), lay(4, 2, D_RNN // 2, D_RNN // 2),
            lay(4, D_RNN), lay(2, D_RNN),
            pl.BlockSpec((None, 2, D_RNN), lambda s: (s, 0, 0)),
        ],
        out_specs=[
            pl.BlockSpec((seq_len, D_RNN), lambda s: (s, 0)),
            pl.BlockSpec((None, 2, D_RNN), lambda s: (s, 0, 0)),
        ],
        out_shape=[
            jax.ShapeDtypeStruct((n_seq * seq_len, D_RNN), BF16),
            jax.ShapeDtypeStruct((n_seq, 2, D_RNN), F32),
        ],
        compiler_params=pltpu.CompilerParams(
            dimension_semantics=("arbitrary",), vmem_limit_bytes=VMEM_LIMIT),
        name="rglru_%d" % seq_len,
    )(lx, lg, w["conv_w"], w["conv_b"], w["wbd"], w["gate_b"], w["lam"], h0)


def _softmax_parts(scores):
    m = scores[0].max(axis=-1, keepdims=True)
    for s in scores[1:]:
        m = jnp.maximum(m, s.max(axis=-1, keepdims=True))
    es = [jnp.exp(s - m) for s in scores]
    tot = es[0].sum(axis=-1, keepdims=True)
    for e in es[1:]:
        tot = tot + e.sum(axis=-1, keepdims=True)
    return es, tot


def _mla_attn_kernel(n_seg, q_ref, *refs):
    k_refs = refs[0:2 * n_seg:2]
    v_refs = refs[1:2 * n_seg:2]
    o_ref = refs[2 * n_seg]
    low = _lane_iota((1, LANES)) < MLA_V
    for pair in range(MLA_HEADS // 2):
        vs = slice(pair * LANES, (pair + 1) * LANES)
        out = None
        for hh in range(2):
            hs = slice((2 * pair + hh) * LANES, (2 * pair + hh + 1) * LANES)
            qh = q_ref[:, hs]
            es, tot = _softmax_parts([_dot_nt(qh, k[:, hs]) for k in k_refs])
            pv = None
            for e, v in zip(es, v_refs):
                t = _dot(e.astype(BF16), v[:, vs])
                pv = t if pv is None else pv + t
            oh = pv / tot
            out = oh if hh == 0 else jnp.where(low, out, oh)
        o_ref[:, vs] = out.astype(BF16)


def _diff_attn_kernel(n_seg, lam_init, q_ref, *refs):
    k_refs = refs[0:2 * n_seg:2]
    v_refs = refs[1:2 * n_seg:2]
    lam_ref, sub_ref, o_ref = refs[2 * n_seg:2 * n_seg + 3]
    lp = lam_ref[...]
    lam = (jnp.exp(jnp.sum(lp[0:1] * lp[1:2], axis=-1, keepdims=True))
           - jnp.exp(jnp.sum(lp[2:3] * lp[3:4], axis=-1, keepdims=True)) + lam_init)
    low = _lane_iota((1, LANES)) < DIFF_DH
    zero = jnp.zeros((), BF16)
    for hd in range(DIFF_HEADS):
        hs = slice(hd * LANES, (hd + 1) * LANES)
        qh = q_ref[:, hs]
        ks = [k[:, hs].astype(BF16) for k in k_refs]
        e0, t0 = _softmax_parts([_dot_nt(jnp.where(low, qh, zero), k) for k in ks])
        e1, t1 = _softmax_parts([_dot_nt(jnp.where(low, zero, qh), k) for k in ks])
        w0 = 1.0 / t0
        w1 = lam / t1
        o = None
        for a, b, v in zip(e0, e1, v_refs):
            t = _dot((a * w0 - b * w1).astype(BF16), v[:, hs].astype(BF16))
            o = t if o is None else o + t
        o = _rms(o) * sub_ref[...] * (1.0 - lam_init)
        o_ref[:, hs] = o.astype(BF16)


def _attn_call(kernel, name, q, segs, extra, extra_specs, q_block0, n_seq, tiles_per_seq, out_width):
    grid = (n_seq, tiles_per_seq)
    seq_len = tiles_per_seq * TILE
    in_specs = [pl.BlockSpec((TILE, q.shape[1]), lambda b, t: (q_block0 * tiles_per_seq + b * tiles_per_seq + t, 0))]
    args = [q]
    for k, v, kind in segs:
        for arr in (k, v):
            if kind == "tok":
                in_specs.append(pl.BlockSpec((seq_len, arr.shape[1]), lambda b, t: (q_block0 + b, 0)))
            else:
                in_specs.append(pl.BlockSpec((None,) + arr.shape[1:], lambda b, t: (b, 0, 0)))
            args.append(arr)
    in_specs += extra_specs
    args += extra
    return pl.pallas_call(
        functools.partial(kernel, len(segs)),
        grid=grid,
        in_specs=in_specs,
        out_specs=pl.BlockSpec((TILE, out_width), lambda b, t: (b * tiles_per_seq + t, 0)),
        out_shape=jax.ShapeDtypeStruct((n_seq * seq_len, out_width), BF16),
        compiler_params=pltpu.CompilerParams(
            dimension_semantics=("arbitrary", "arbitrary"), vmem_limit_bytes=VMEM_LIMIT),
        name=name,
    )(*args)


def _back_kernel(x_ref, mod_ref, n1_ref, n2_ref, om_ref, ul_ref, od_ref, wom_ref, wol_ref, wod_ref,
                 wmg_ref, bmg_ref, wout_ref, rw_ref, rb_ref, x1_ref, h2_ref, comb_ref):
    x = x_ref[...]
    mod = mod_ref[...]
    seg = lambda j: mod[:, j * D_MODEL:(j + 1) * D_MODEL]
    hb = _modulated(x, n1_ref[...], seg(0), seg(1)).astype(BF16)
    branches = (_dot(om_ref[...], wom_ref[...]), _dot(ul_ref[...], wol_ref[...]), _dot(od_ref[...], wod_ref[...]))
    merged = None
    for j, o in enumerate(branches):
        cs = slice(j * D_MODEL, (j + 1) * D_MODEL)
        g = jax.nn.sigmoid(_dot(hb, wmg_ref[:, cs]) + bmg_ref[:, cs])
        merged = g * o if merged is None else merged + g * o
    x1 = x + seg(2) * _dot(merged.astype(BF16), wout_ref[...])
    x1_ref[...] = x1
    h2 = _modulated(x1, n2_ref[...], seg(3), seg(4))
    h_hi = h2.astype(BF16)
    h2_ref[...] = h_hi

    h_lo = (h2 - h_hi.astype(F32)).astype(BF16)
    rw = rw_ref[...]
    w_hi = rw.astype(BF16)
    w_lo = (rw - w_hi.astype(F32)).astype(BF16)
    logits = _dot(h_hi, w_hi) + (_dot(h_lo, w_hi) + _dot(h_hi, w_lo)) + rb_ref[...]
    lane = _lane_iota(logits.shape)
    neg = jnp.float32(-jnp.inf)
    work = jnp.where(lane < N_EXPERTS, logits, neg)
    top = None
    comb = jnp.zeros_like(logits)
    denom = None
    for _ in range(TOP_K):
        m = work.max(axis=-1, keepdims=True)
        idx = jnp.min(jnp.where(work == m, lane, LANES), axis=-1, keepdims=True)
        sel = lane == idx
        if top is None:
            top = m
        p = jnp.exp(m - top)
        comb = comb + jnp.where(sel, p, 0.0)
        denom = p if denom is None else denom + p
        work = jnp.where(sel, neg, work)
    comb_ref[...] = comb / denom


def _back_call(l, x, mod3, om, ul, od, w):
    tok = lambda width: pl.BlockSpec((TILE, width), lambda i: (i, 0))
    lay = lambda *shape: _const_spec((None,) + shape, (l,) + (0,) * len(shape))
    return pl.pallas_call(
        _back_kernel,
        grid=(N_TILES,),
        in_specs=[
            tok(D_MODEL),
            pl.BlockSpec((None, 1, 6 * D_MODEL), lambda i: (l * COND_ROWS + _cond_row(i), 0, 0)),
            lay(1, D_MODEL), lay(1, D_MODEL), tok(512), tok(512), tok(512),
            lay(512, D_MODEL), lay(512, D_MODEL), lay(512, D_MODEL),
            lay(D_MODEL, 3 * D_MODEL), lay(1, 3 * D_MODEL), lay(D_MODEL, D_MODEL),
            lay(D_MODEL, LANES), lay(1, LANES),
        ],
        out_specs=[tok(D_MODEL), tok(D_MODEL), tok(LANES)],
        out_shape=[
            jax.ShapeDtypeStruct((N_TOK, D_MODEL), F32),
            jax.ShapeDtypeStruct((N_TOK, D_MODEL), BF16),
            jax.ShapeDtypeStruct((N_TOK, LANES), F32),
        ],
        compiler_params=pltpu.CompilerParams(
            dimension_semantics=("arbitrary",), vmem_limit_bytes=VMEM_LIMIT),
        name="back",
    )(x, mod3, w["norm1"], w["norm2"], om, ul, od, w["w_o_mla"], w["w_o_lru"], w["w_o_diff"],
      w["w_merge"], w["b_merge"], w["w_out"], w["router_w"], w["router_b"])


def _moe_kernel(h_ref, comb_ref, w1_ref, b1_ref, w2_ref, b2_ref, x1_ref, mod_ref, o_ref, acc_ref):
    e = pl.program_id(1)

    @pl.when(e == 0)
    def _():
        acc_ref[...] = jnp.zeros_like(acc_ref)

    rows = 256
    for r in range(MOE_TOK // rows):
        rs = slice(r * rows, (r + 1) * rows)
        comb = comb_ref[rs, :]
        ce = jnp.sum(jnp.where(_lane_iota(comb.shape) == e, comb, 0.0), axis=-1, keepdims=True)
        gu = _dot(h_ref[rs, :], w1_ref[...]) + b1_ref[...]
        gate = jnp.minimum(gu[:, :D_EXPERT], SWIGLU_LIMIT)
        up = jnp.clip(gu[:, D_EXPERT:], -SWIGLU_LIMIT, SWIGLU_LIMIT)
        act = (up + 1.0) * (gate * jax.nn.sigmoid(SWIGLU_ALPHA * gate))
        y = _dot(act.astype(BF16), w2_ref[...]) + b2_ref[...]
        acc_ref[rs, :] += ce * y

    @pl.when(e == N_EXPERTS - 1)
    def _():
        g2 = mod_ref[:, 5 * D_MODEL:6 * D_MODEL]
        o_ref[...] = x1_ref[...] + g2 * acc_ref[...]


def _moe_call(l, h2, comb, x1, mod3, w):
    blk = lambda width: pl.BlockSpec((MOE_TOK, width), lambda j, e: (j, 0))
    ctx_blocks = N_CTX_TOK // MOE_TOK
    cond = lambda j: jnp.where(j < ctx_blocks, CTX_COND_ROW, j - ctx_blocks)
    return pl.pallas_call(
        _moe_kernel,
        grid=(N_MOE_BLOCKS, N_EXPERTS),
        in_specs=[
            blk(D_MODEL), blk(LANES),
            pl.BlockSpec((None, None, D_MODEL, 2 * D_EXPERT), lambda j, e: (l, e, 0, 0)),
            pl.BlockSpec((None, None, 1, 2 * D_EXPERT), lambda j, e: (l, e, 0, 0)),
            pl.BlockSpec((None, None, D_EXPERT, D_MODEL), lambda j, e: (l, e, 0, 0)),
            pl.BlockSpec((None, None, 1, D_MODEL), lambda j, e: (l, e, 0, 0)),
            blk(D_MODEL),
            pl.BlockSpec((None, 1, 6 * D_MODEL), lambda j, e: (l * COND_ROWS + cond(j), 0, 0)),
        ],
        out_specs=blk(D_MODEL),
        out_shape=jax.ShapeDtypeStruct((N_TOK, D_MODEL), F32),
        scratch_shapes=[pltpu.VMEM((MOE_TOK, D_MODEL), F32)],
        compiler_params=pltpu.CompilerParams(
            dimension_semantics=("arbitrary", "arbitrary"), vmem_limit_bytes=VMEM_LIMIT),
        name="moe",
    )(h2, comb, w["exp_w1"], w["exp_b1"], w["exp_w2"], w["exp_b2"], x1, mod3)


def _axial_tables(n_tokens, dim):
    rows = n_tokens // GRID_W
    row = jnp.repeat(jnp.arange(rows), GRID_W)
    col = jnp.tile(jnp.arange(GRID_W), rows)
    half = dim // 2
    inv = 1.0 / (ROPE_BASE ** (jnp.arange(0, half, 2, dtype=F32) / half))

    def axis_angles(pos):
        ang = pos.astype(F32)[:, None] * inv[None, :]
        return jnp.concatenate([ang, ang], axis=-1)

    ang = jnp.concatenate([axis_angles(row), axis_angles(col)], axis=-1)
    return jnp.cos(ang), jnp.sin(ang)


def _rope_slot_tables(dim, lane0, copies):
    cos, sin = _axial_tables(DEC_SEQ, dim)
    quarter = dim // 4
    sign = jnp.where((jnp.arange(dim) % (dim // 2)) < quarter, -1.0, 1.0)
    sin = sin * sign
    cos_slot = jnp.ones((DEC_SEQ, LANES), F32)
    sin_slot = jnp.zeros((DEC_SEQ, LANES), F32)
    for c in range(copies):
        cos_slot = cos_slot.at[:, lane0 + c * dim:lane0 + (c + 1) * dim].set(cos)
        sin_slot = sin_slot.at[:, lane0 + c * dim:lane0 + (c + 1) * dim].set(sin)
    ident = (jnp.ones((1, TILE, LANES), F32), jnp.zeros((1, TILE, LANES), F32))
    cos_t = jnp.concatenate([ident[0], cos_slot.reshape(LAT_TILES_PER_SEQ, TILE, LANES)], axis=0)
    sin_t = jnp.concatenate([ident[1], sin_slot.reshape(LAT_TILES_PER_SEQ, TILE, LANES)], axis=0)
    return cos_t, sin_t


def _pad_last(a, width):
    return jnp.pad(a, [(0, 0)] * (a.ndim - 1) + [(0, width - a.shape[-1])])


def _prepare(p):
    L = DEPTH
    w = {}
    w_in = p["w_in"]
    kr_slot = jnp.pad(w_in[:, :, OFF_KVA + KV_LORA:OFF_LRU_X], ((0, 0), (0, 0), (MLA_NOPE, LANES - MLA_QK)))
    w["w_in"] = jnp.concatenate(
        [w_in[:, :, OFF_QA:OFF_KVA + KV_LORA], kr_slot, w_in[:, :, OFF_LRU_X:IN_COLS]], axis=-1).astype(BF16)
    w["wqb"] = _pad_last(p["w_q_b"].reshape(L, Q_LORA, MLA_HEADS, MLA_QK), LANES).reshape(
        L, Q_LORA, MLA_HEADS * LANES).astype(BF16)
    kvb = p["w_kv_b"].reshape(L, KV_LORA, MLA_HEADS, MLA_NOPE + MLA_V)
    w["wkvk"] = _pad_last(kvb[..., :MLA_NOPE], LANES).reshape(L, KV_LORA, MLA_HEADS * LANES).astype(BF16)
    w["wkvv"] = kvb[..., MLA_NOPE:].reshape(L, KV_LORA, MLA_HEADS * MLA_V).astype(BF16)
    row = lambda a: a[:, None, :]
    w["norm1"] = row(p["norm1_g"])
    w["norm2"] = row(p["norm2_g"])
    w["qa_norm"] = row(p["mla_qa_norm"])
    w["kva_norm"] = row(p["mla_kva_norm"])
    w["qn"] = row(_pad_last(p["mla_qn"], LANES))
    w["kn"] = row(_pad_last(p["mla_kn"], LANES))
    w["dqn"] = row(jnp.tile(p["diff_qn"], (1, 2)))
    w["dkn"] = row(jnp.tile(p["diff_kn"], (1, 2)))
    w["cos_m"], w["sin_m"] = _rope_slot_tables(MLA_ROPE, MLA_NOPE, 1)
    w["cos_d"], w["sin_d"] = _rope_slot_tables(DIFF_DH, 0, 2)
    w["conv_w"] = p["lru_conv_w"]
    w["conv_b"] = row(p["lru_conv_b"])
    per = (D_RNN // 2) // LRU_BW
    gw = p["lru_gate_w"].reshape(L, 4, 2, per, LRU_BW, LRU_BW)
    eye = jnp.eye(per, dtype=F32)
    w["wbd"] = jnp.einsum("lkhacd,ab->lkhacbd", gw, eye).reshape(L, 4, 2, D_RNN // 2, D_RNN // 2).astype(BF16)
    w["gate_b"] = p["lru_gate_b"].reshape(L, 4, D_RNN)
    w["lam"] = p["lru_lambda"]
    w["w_o_mla"] = p["w_o_mla"].astype(BF16)
    w["w_o_lru"] = p["w_o_lru"].astype(BF16)
    w["w_o_diff"] = p["w_o_diff"].astype(BF16)
    w["w_merge"] = p["w_merge"].astype(BF16)
    w["b_merge"] = row(p["b_merge"])
    w["w_out"] = p["w_out"].astype(BF16)
    w["router_w"] = _pad_last(p["router_w"], LANES)
    w["router_b"] = row(_pad_last(p["router_b"], LANES))
    w["exp_w1"] = p["exp_w1"].astype(BF16)
    w["exp_b1"] = p["exp_b1"][:, :, None, :]
    w["exp_w2"] = p["exp_w2"].astype(BF16)
    w["exp_b2"] = p["exp_b2"][:, :, None, :]
    w["diff_lambda"] = p["diff_lambda"]
    w["diff_subln"] = row(p["diff_subln"])
    return w


def kernel(x_prompt, x_sample, cache_mla_ckv, cache_mla_krope, state_lru, cache_diff_k, cache_diff_v,
           c, c_ctx, ada_w, ada_b, norm1_g, norm2_g, w_in, mla_qa_norm, w_q_b, mla_kva_norm, w_kv_b,
           mla_qn, mla_kn, w_o_mla, lru_conv_w, lru_conv_b, lru_gate_w, lru_gate_b, lru_lambda, w_o_lru,
           diff_qn, diff_kn, diff_lambda, diff_subln, w_o_diff, w_merge, b_merge, w_out,
           router_w, router_b, exp_w1, exp_b1, exp_w2, exp_b2):
    params = dict(
        norm1_g=norm1_g, norm2_g=norm2_g, w_in=w_in, mla_qa_norm=mla_qa_norm, w_q_b=w_q_b,
        mla_kva_norm=mla_kva_norm, w_kv_b=w_kv_b, mla_qn=mla_qn, mla_kn=mla_kn, w_o_mla=w_o_mla,
        lru_conv_w=lru_conv_w, lru_conv_b=lru_conv_b, lru_gate_w=lru_gate_w, lru_gate_b=lru_gate_b,
        lru_lambda=lru_lambda, w_o_lru=w_o_lru, diff_qn=diff_qn, diff_kn=diff_kn, diff_lambda=diff_lambda,
        diff_subln=diff_subln, w_o_diff=w_o_diff, w_merge=w_merge, b_merge=b_merge, w_out=w_out,
        router_w=router_w, router_b=router_b, exp_w1=exp_w1, exp_b1=exp_b1, exp_w2=exp_w2, exp_b2=exp_b2)
    w = _prepare(params)

    cond = jnp.concatenate(
        [c, c_ctx[None, :], jnp.zeros((COND_ROWS - DEC_BATCH - 1, D_MODEL), F32)], axis=0)
    mod = _ada_call(cond, ada_w, ada_b[:, None, :])
    mod3 = mod.reshape(DEPTH * COND_ROWS, 1, 6 * D_MODEL)

    place = jnp.pad(jnp.eye(MLA_ROPE, dtype=F32), ((0, 0), (MLA_NOPE, LANES - MLA_QK))).astype(BF16)
    k_ctx, v_ctx = _kvx_call(cache_mla_ckv, cache_mla_krope, w["wkvk"], w["wkvv"], w["kn"], place)
    dk_ctx = cache_diff_k.reshape(DEC_BATCH, DEPTH, PAST_LEN, 512)
    dv_ctx = cache_diff_v.reshape(DEC_BATCH, DEPTH, PAST_LEN, 512)

    x = jnp.concatenate([x_prompt.reshape(N_CTX_TOK, D_MODEL), x_sample.reshape(-1, D_MODEL)], axis=0)
    h0_ctx = jnp.zeros((BATCH, 2, D_RNN), F32)
    lat_blk = N_CTX_TOK // DEC_SEQ
    new_ckv, new_krope, new_lru, new_dk, new_dv = [], [], [], [], []
    for l in range(DEPTH):
        lam_init = 0.8 - 0.6 * math.exp(-0.3 * l)
        (qm, km, vm, qd, kd, vd, lx, lg, ckv_o, kro_o, dk_o, dv_o) = _front_call(l, x, mod3, w)

        u_ctx, hf_ctx = _lru_call(l, lx, lg, h0_ctx, w, BATCH, SEQ, 0)
        u_lat, _ = _lru_call(l, lx, lg, state_lru[:, l], w, DEC_BATCH, DEC_SEQ, lat_blk)
        ul = jnp.concatenate([u_ctx, u_lat], axis=0)

        om_ctx = _attn_call(_mla_attn_kernel, "mla_ctx", qm, [(km, vm, "tok")], [], [], 0, BATCH, 1, 512)
        om_lat = _attn_call(_mla_attn_kernel, "mla_lat", qm,
                            [(k_ctx[l], v_ctx[l], "cache"), (km, vm, "tok")], [], [],
                            lat_blk, DEC_BATCH, LAT_TILES_PER_SEQ, 512)
        om = jnp.concatenate([om_ctx, om_lat], axis=0)

        dk_extra = [w["diff_lambda"], w["diff_subln"]]
        dk_specs = lambda: [pl.BlockSpec((None, 4, DIFF_DH), lambda b, t: (l, 0, 0)),
                            pl.BlockSpec((None, 1, LANES), lambda b, t: (l, 0, 0))]
        dkern = functools.partial(_diff_attn_kernel, lam_init=lam_init)
        dkern_pos = lambda n_seg, *refs: _diff_attn_kernel(n_seg, lam_init, *refs)
        od_ctx = _attn_call(dkern_pos, "diff_ctx", qd, [(kd, vd, "tok")], dk_extra, dk_specs(),
                            0, BATCH, 1, 512)
        od_lat = _attn_call(dkern_pos, "diff_lat", qd,
                            [(dk_ctx[:, l], dv_ctx[:, l], "cache"), (kd, vd, "tok")], dk_extra, dk_specs(),
                            lat_blk, DEC_BATCH, LAT_TILES_PER_SEQ, 512)
        od = jnp.concatenate([od_ctx, od_lat], axis=0)

        x1, h2, comb = _back_call(l, x, mod3, om, ul, od, w)
        x = _moe_call(l, h2, comb, x1, mod3, w)

        new_ckv.append(ckv_o[:N_CTX_TOK].reshape(BATCH, SEQ, KV_LORA))
        new_krope.append(kro_o[:N_CTX_TOK].reshape(BATCH, SEQ, MLA_ROPE))
        new_lru.append(hf_ctx)
        new_dk.append(dk_o[:N_CTX_TOK].reshape(BATCH, SEQ, DIFF_HEADS, 2, DIFF_DH))
        new_dv.append(dv_o[:N_CTX_TOK].reshape(BATCH, SEQ, DIFF_HEADS, DIFF_DV))

    xp = x[:N_CTX_TOK].reshape(BATCH, SEQ, D_MODEL)
    xs = x[N_CTX_TOK:].reshape(DEC_BATCH, DEC_SEQ, D_MODEL)
    return (xp, xs, jnp.stack(new_ckv, axis=1), jnp.stack(new_krope, axis=1), jnp.stack(new_lru, axis=1),
            jnp.stack(new_dk, axis=1), jnp.stack(new_dv, axis=1))
```

```python
import functools
import math

import jax
import jax.numpy as jnp
from jax import lax
from jax.experimental import pallas as pl
from jax.experimental.pallas import tpu as pltpu

F32 = jnp.float32
BF16 = jnp.bfloat16

D_MODEL = 1024
BATCH = 16
SEQ = 256
DEPTH = 4
DEC_BATCH = 8
DEC_SEQ = 1024
PAST_LEN = 256
GRID_W = 64
ROPE_BASE = 10000.0
NORM_EPS = 1e-6

MLA_HEADS = 8
MLA_NOPE = 64
MLA_ROPE = 32
MLA_QK = MLA_NOPE + MLA_ROPE
MLA_V = 64
Q_LORA = 384
KV_LORA = 256
D_RNN = 512
LRU_BLOCKS = 8
LRU_BW = D_RNN // LRU_BLOCKS
CONV_W = 4
LRU_C = 8.0
DIFF_HEADS = 4
DIFF_DH = 64
DIFF_DV = 2 * DIFF_DH
N_EXPERTS = 32
TOP_K = 4
D_EXPERT = 512
SWIGLU_LIMIT = 7.0
SWIGLU_ALPHA = 1.702

OFF_QA = 0
OFF_KVA = OFF_QA + Q_LORA
OFF_LRU_X = OFF_KVA + KV_LORA + MLA_ROPE
OFF_LRU_G = OFF_LRU_X + D_RNN
OFF_DQ = OFF_LRU_G + D_RNN
OFF_DK = OFF_DQ + DIFF_HEADS * 2 * DIFF_DH
OFF_DV = OFF_DK + DIFF_HEADS * 2 * DIFF_DH
IN_COLS = OFF_DV + DIFF_HEADS * DIFF_DV

LANES = 128
TILE = 256
N_CTX_TILES = BATCH * SEQ // TILE
LAT_TILES_PER_SEQ = DEC_SEQ // TILE
N_LAT_TILES = DEC_BATCH * LAT_TILES_PER_SEQ
N_TILES = N_CTX_TILES + N_LAT_TILES
N_CTX_TOK = BATCH * SEQ
N_TOK = N_TILES * TILE
COND_ROWS = 16
CTX_COND_ROW = DEC_BATCH

P_QA = 0
P_CKV = P_QA + Q_LORA
P_KR = P_CKV + KV_LORA
P_LX = P_KR + LANES
P_LG = P_LX + D_RNN
P_DQ = P_LG + D_RNN
P_DK = P_DQ + 512
P_DV = P_DK + 512
P_COLS = P_DV + 512

SUB = 8
MOE_TILES = 8
MOE_TOK = MOE_TILES * TILE
N_MOE_BLOCKS = N_TOK // MOE_TOK
GRAN = 8
SLOTS = TILE * TOP_K + N_EXPERTS * GRAN
MOE_CHUNK = 256
CHUNK_GRAN = MOE_CHUNK // GRAN
CHUNK_SHIFT = CHUNK_GRAN.bit_length() - 1
MAX_GRAN = MOE_TILES * SLOTS // GRAN
VMEM_LIMIT = 56 * 1024 * 1024


def _cond_row(i):
    return jnp.where(i < N_CTX_TILES, CTX_COND_ROW, (i - N_CTX_TILES) // LAT_TILES_PER_SEQ)


def _rope_idx(i):
    return jnp.where(i < N_CTX_TILES, 0, 1 + (i - N_CTX_TILES) % LAT_TILES_PER_SEQ)


def _ctx_out_idx(i):
    return jnp.minimum(i, N_CTX_TILES)


def _const_spec(shape, index):
    return pl.BlockSpec(shape, lambda *_: index, pipeline_mode=pl.Buffered(1))


def _dot(a, b):
    return jnp.dot(a, b, preferred_element_type=F32)


def _dot_nt(a, b):
    return lax.dot_general(a, b, (((1,), (1,)), ((), ())), preferred_element_type=F32)


def _rms(x, n=None):
    n = x.shape[-1] if n is None else n
    ss = jnp.sum(x * x, axis=-1, keepdims=True)
    return x * lax.rsqrt(ss * (1.0 / n) + NORM_EPS)


def _rope(x, cos, sin_signed, half, first):
    rot = jnp.where(first, pltpu.roll(x, LANES - half, 1), pltpu.roll(x, half, 1))
    return x * cos + rot * sin_signed


def _lane_iota(shape):
    return lax.broadcasted_iota(jnp.int32, shape, len(shape) - 1)


def _ada_kernel(cond_ref, w_ref, b_ref, o_ref):
    c = cond_ref[...]
    s = c * jax.nn.sigmoid(c)
    o_ref[...] = _dot(s.astype(BF16), w_ref[...].astype(BF16)) + b_ref[...]


def _ada_call(cond, ada_w, ada_b3):
    cb = 1024
    return pl.pallas_call(
        _ada_kernel,
        grid=(DEPTH, 6 * D_MODEL // cb),
        in_specs=[
            pl.BlockSpec((COND_ROWS, D_MODEL), lambda l, j: (0, 0)),
            pl.BlockSpec((None, D_MODEL, cb), lambda l, j: (l, 0, j)),
            pl.BlockSpec((None, 1, cb), lambda l, j: (l, 0, j)),
        ],
        out_specs=pl.BlockSpec((None, COND_ROWS, cb), lambda l, j: (l, 0, j)),
        out_shape=jax.ShapeDtypeStruct((DEPTH, COND_ROWS, 6 * D_MODEL), F32),
        compiler_params=pltpu.CompilerParams(
            dimension_semantics=("arbitrary", "arbitrary"), vmem_limit_bytes=VMEM_LIMIT),
        name="ada_mod",
    )(cond, ada_w, ada_b3)


def _mla_k_heads(kn, krs, kn_gain, cos, sin, first, store):
    for hd in range(MLA_HEADS):
        kh = kn[:, hd * LANES:(hd + 1) * LANES] + krs
        kh = _rms(kh, MLA_QK) * kn_gain
        if cos is not None:
            kh = _rope(kh, cos, sin, MLA_ROPE // 4, first)
        store(hd, kh)


def _kvx_kernel(ckv_ref, kr_ref, wk_ref, wv_ref, kn_ref, place_ref, k_ref, v_ref):
    c = ckv_ref[...].astype(BF16)
    kn = _dot(c, wk_ref[...])
    v_ref[...] = _dot(c, wv_ref[...]).astype(BF16)
    krs = _dot(kr_ref[...].astype(BF16), place_ref[...])

    def store(hd, kh):
        k_ref[:, hd * LANES:(hd + 1) * LANES] = kh.astype(BF16)

    _mla_k_heads(kn, krs, kn_ref[...], None, None, None, store)


def _kvx_call(cache_ckv, cache_krope, wkvk, wkvv, kn_pad, place):
    return pl.pallas_call(
        _kvx_kernel,
        grid=(DEPTH, DEC_BATCH),
        in_specs=[
            pl.BlockSpec((None, None, PAST_LEN, KV_LORA), lambda l, b: (b, l, 0, 0)),
            pl.BlockSpec((None, None, PAST_LEN, MLA_ROPE), lambda l, b: (b, l, 0, 0)),
            pl.BlockSpec((None, KV_LORA, MLA_HEADS * LANES), lambda l, b: (l, 0, 0)),
            pl.BlockSpec((None, KV_LORA, MLA_HEADS * MLA_V), lambda l, b: (l, 0, 0)),
            pl.BlockSpec((None, 1, LANES), lambda l, b: (l, 0, 0)),
            pl.BlockSpec((MLA_ROPE, LANES), lambda l, b: (0, 0)),
        ],
        out_specs=[
            pl.BlockSpec((None, None, PAST_LEN, MLA_HEADS * LANES), lambda l, b: (l, b, 0, 0)),
            pl.BlockSpec((None, None, PAST_LEN, MLA_HEADS * MLA_V), lambda l, b: (l, b, 0, 0)),
        ],
        out_shape=[
            jax.ShapeDtypeStruct((DEPTH, DEC_BATCH, PAST_LEN, MLA_HEADS * LANES), BF16),
            jax.ShapeDtypeStruct((DEPTH, DEC_BATCH, PAST_LEN, MLA_HEADS * MLA_V), BF16),
        ],
        compiler_params=pltpu.CompilerParams(
            dimension_semantics=("arbitrary", "arbitrary"), vmem_limit_bytes=VMEM_LIMIT),
        name="ctx_kv_expand",
    )(cache_ckv, cache_krope, wkvk, wkvv, kn_pad, place)


def _modulated(x, gain, shift, scale):
    return _rms(x) * gain * (1.0 + scale) + shift


def _untile(t_ref):
    return jnp.concatenate([t_ref[:, s, :] for s in range(SUB)], axis=-1)


def _moe_residual(x1_ref, yt_ref, modp_ref):
    return x1_ref[...] + modp_ref[:, 5 * D_MODEL:6 * D_MODEL] * _untile(yt_ref)


def _front_kernel(has_moe, *refs):
    if has_moe:
        x1_ref, yt_ref, modp_ref = refs[:3]
        refs = refs[3:]
    else:
        x_ref = refs[0]
        refs = refs[1:]
    (mod_ref, n1_ref, win_ref, qan_ref, wqb_ref, kvan_ref, wkvk_ref, wkvv_ref,
     qn_ref, kn_ref, dqn_ref, dkn_ref, cosm_ref, sinm_ref, cosd_ref, sind_ref) = refs[:16]
    refs = refs[16:]
    if has_moe:
        xo_ref = refs[0]
        refs = refs[1:]
    (qm_ref, km_ref, vm_ref, qd_ref, kd_ref, vd_ref, lx_ref, lg_ref,
     ckv_ref, kro_ref, dko_ref, dvo_ref) = refs
    if has_moe:
        x = _moe_residual(x1_ref, yt_ref, modp_ref)
        xo_ref[...] = x
    else:
        x = x_ref[...]
    mod = mod_ref[...]
    h = _modulated(x, n1_ref[...], mod[:, 0:D_MODEL], mod[:, D_MODEL:2 * D_MODEL])
    hb = h.astype(BF16)

    lane = _lane_iota((1, LANES))
    first_m = (lane % (MLA_ROPE // 2)) < (MLA_ROPE // 4)
    first_d = (lane % (DIFF_DH // 2)) < (DIFF_DH // 4)
    cosm, sinm = cosm_ref[...], sinm_ref[...]
    cosd, sind = cosd_ref[...], sind_ref[...]

    qa = _rms(_dot(hb, win_ref[:, P_QA:P_QA + Q_LORA])) * qan_ref[...]
    q = _dot(qa.astype(BF16), wqb_ref[...])
    q_scale = MLA_QK ** -0.5
    for hd in range(MLA_HEADS):
        qh = _rms(q[:, hd * LANES:(hd + 1) * LANES], MLA_QK) * qn_ref[...]
        qh = _rope(qh, cosm, sinm, MLA_ROPE // 4, first_m)
        qm_ref[:, hd * LANES:(hd + 1) * LANES] = (qh * q_scale).astype(BF16)

    ckv = _rms(_dot(hb, win_ref[:, P_CKV:P_CKV + KV_LORA])) * kvan_ref[...]
    ckv_ref[...] = ckv
    krs = _dot(hb, win_ref[:, P_KR:P_KR + LANES])
    kro_ref[...] = krs[:, MLA_NOPE:MLA_NOPE + MLA_ROPE]
    cb = ckv.astype(BF16)
    vm_ref[...] = _dot(cb, wkvv_ref[...]).astype(BF16)
    kn = _dot(cb, wkvk_ref[...])

    def store_k(hd, kh):
        km_ref[:, hd * LANES:(hd + 1) * LANES] = kh.astype(BF16)

    _mla_k_heads(kn, krs, kn_ref[...], cosm, sinm, first_m, store_k)

    lx_ref[...] = _dot(hb, win_ref[:, P_LX:P_LX + D_RNN])
    lg_ref[...] = _dot(hb, win_ref[:, P_LG:P_LG + D_RNN])

    low = lane < DIFF_DH

    def pair_norm(t):
        sq = t * t
        s_all = jnp.sum(sq, axis=-1, keepdims=True)
        s_lo = jnp.sum(jnp.where(low, sq, 0.0), axis=-1, keepdims=True)
        r_lo = lax.rsqrt(s_lo * (1.0 / DIFF_DH) + NORM_EPS)
        r_hi = lax.rsqrt((s_all - s_lo) * (1.0 / DIFF_DH) + NORM_EPS)
        return t * jnp.where(low, r_lo, r_hi)

    dq = _dot(hb, win_ref[:, P_DQ:P_DQ + 512])
    dk = _dot(hb, win_ref[:, P_DK:P_DK + 512])
    d_scale = DIFF_DH ** -0.5
    for hd in range(DIFF_HEADS):
        sl = slice(hd * LANES, (hd + 1) * LANES)
        qh = _rope(pair_norm(dq[:, sl]) * dqn_ref[...], cosd, sind, DIFF_DH // 4, first_d)
        qd_ref[:, sl] = (qh * d_scale).astype(BF16)
        kh = _rope(pair_norm(dk[:, sl]) * dkn_ref[...], cosd, sind, DIFF_DH // 4, first_d)
        kd_ref[:, sl] = kh.astype(BF16)
        dko_ref[:, sl] = kh
    dv = _dot(hb, win_ref[:, P_DV:P_DV + 512])
    vd_ref[...] = dv.astype(BF16)
    dvo_ref[...] = dv


def _mod_spec(l):
    return pl.BlockSpec((None, 1, 6 * D_MODEL), lambda i: (l * COND_ROWS + _cond_row(i), 0, 0))


def _tile_spec():
    return pl.BlockSpec((TILE, SUB, LANES), lambda i: (i, 0, 0))


def _front_call(l, x, moe_out, mod3, w):
    tok = lambda width: pl.BlockSpec((TILE, width), lambda i: (i, 0))
    ctx = lambda width: pl.BlockSpec((TILE, width), lambda i: (_ctx_out_idx(i), 0))
    lay = lambda *shape: _const_spec((None,) + shape, (l,) + (0,) * len(shape))
    rope = pl.BlockSpec((None, TILE, LANES), lambda i: (_rope_idx(i), 0, 0))
    n_ctx_rows = (N_CTX_TILES + 1) * TILE
    has_moe = moe_out is not None
    if has_moe:
        lead_specs = [tok(D_MODEL), _tile_spec(), _mod_spec(l - 1)]
        lead_args = [x, moe_out, mod3]
        x_out_specs = [tok(D_MODEL)]
        x_out_shape = [jax.ShapeDtypeStruct((N_TOK, D_MODEL), F32)]
    else:
        lead_specs, lead_args, x_out_specs, x_out_shape = [tok(D_MODEL)], [x], [], []
    outs = pl.pallas_call(
        functools.partial(_front_kernel, has_moe),
        grid=(N_TILES,),
        in_specs=lead_specs + [
            _mod_spec(l),
            lay(1, D_MODEL), lay(D_MODEL, P_COLS), lay(1, Q_LORA), lay(Q_LORA, MLA_HEADS * LANES),
            lay(1, KV_LORA), lay(KV_LORA, MLA_HEADS * LANES), lay(KV_LORA, MLA_HEADS * MLA_V),
            lay(1, LANES), lay(1, LANES), lay(1, LANES), lay(1, LANES),
            rope, rope, rope, rope,
        ],
        out_specs=x_out_specs + [
            tok(1024), tok(1024), tok(512), tok(512), tok(512), tok(512), tok(512), tok(512),
            ctx(KV_LORA), ctx(MLA_ROPE), ctx(512), ctx(512),
        ],
        out_shape=x_out_shape + [
            jax.ShapeDtypeStruct((N_TOK, 1024), BF16), jax.ShapeDtypeStruct((N_TOK, 1024), BF16),
            jax.ShapeDtypeStruct((N_TOK, 512), BF16), jax.ShapeDtypeStruct((N_TOK, 512), BF16),
            jax.ShapeDtypeStruct((N_TOK, 512), BF16), jax.ShapeDtypeStruct((N_TOK, 512), BF16),
            jax.ShapeDtypeStruct((N_TOK, 512), F32), jax.ShapeDtypeStruct((N_TOK, 512), F32),
            jax.ShapeDtypeStruct((n_ctx_rows, KV_LORA), F32),
            jax.ShapeDtypeStruct((n_ctx_rows, MLA_ROPE), F32),
            jax.ShapeDtypeStruct((n_ctx_rows, 512), F32),
            jax.ShapeDtypeStruct((n_ctx_rows, 512), F32),
        ],
        compiler_params=pltpu.CompilerParams(
            dimension_semantics=("arbitrary",), vmem_limit_bytes=VMEM_LIMIT),
        name="front",
    )(*lead_args, mod3, w["norm1"], w["w_in"], w["qa_norm"], w["wqb"], w["kva_norm"], w["wkvk"], w["wkvv"],
      w["qn"], w["kn"], w["dqn"], w["dkn"], w["cos_m"], w["sin_m"], w["cos_d"], w["sin_d"])
    if has_moe:
        return outs[0], outs[1:]
    return x, outs


def _residual_kernel(x1_ref, yt_ref, modp_ref, o_ref):
    o_ref[...] = _moe_residual(x1_ref, yt_ref, modp_ref)


def _residual_call(l, x1, moe_out, mod3):
    tok = pl.BlockSpec((TILE, D_MODEL), lambda i: (i, 0))
    return pl.pallas_call(
        _residual_kernel,
        grid=(N_TILES,),
        in_specs=[tok, _tile_spec(), _mod_spec(l)],
        out_specs=tok,
        out_shape=jax.ShapeDtypeStruct((N_TOK, D_MODEL), F32),
        compiler_params=pltpu.CompilerParams(
            dimension_semantics=("arbitrary",), vmem_limit_bytes=VMEM_LIMIT),
        name="moe_residual",
    )(x1, moe_out, mod3)


def _scan(a, b, h0, reverse):
    n = a.shape[0]
    row = lax.broadcasted_iota(jnp.int32, (n, 1), 0)
    d = 1
    while d < n:
        if reverse:
            valid = row < n - d
            shift = n - d
        else:
            valid = row >= d
            shift = d
        a_s = jnp.where(valid, pltpu.roll(a, shift, 0), 1.0)
        b_s = jnp.where(valid, pltpu.roll(b, shift, 0), 0.0)
        b = a * b_s + b
        a = a * a_s
        d *= 2
    return a * h0 + b


def _gelu_tanh(x):
    return 0.5 * x * (1.0 + jnp.tanh(math.sqrt(2.0 / math.pi) * (x + 0.044715 * x * x * x)))


def _lru_kernel(lx_ref, lg_ref, cw_ref, cb_ref, wbd_ref, gb_ref, lam_ref, h0_ref, u_ref, hf_ref):
    x = lx_ref[...]
    n = x.shape[0]
    row = lax.broadcasted_iota(jnp.int32, (n, 1), 0)
    cw = cw_ref[...]
    xr = cb_ref[...] + cw[2:3] * x
    xr = xr + cw[0:1] * jnp.where(row >= 2, pltpu.roll(x, 2, 0), 0.0)
    xr = xr + cw[1:2] * jnp.where(row >= 1, pltpu.roll(x, 1, 0), 0.0)
    xr = xr + cw[3:4] * jnp.where(row < n - 1, pltpu.roll(x, n - 1, 0), 0.0)
    xb = xr.astype(BF16)
    half = D_RNN // 2
    lam = lam_ref[...]
    h0 = h0_ref[...]
    gb = gb_ref[...]
    total = None
    for d in range(2):
        pre = []
        for g in range(2):
            k = d * 2 + g
            p = jnp.concatenate([_dot(xb[:, :half], wbd_ref[k, 0]), _dot(xb[:, half:], wbd_ref[k, 1])],
                                axis=-1)
            pre.append(p + gb[k:k + 1])
        r = jax.nn.sigmoid(pre[0])
        i = jax.nn.sigmoid(pre[1])
        z = -lam[d:d + 1]
        softplus = jnp.maximum(z, 0.0) + jnp.log(1.0 + jnp.exp(-jnp.abs(z)))
        a = jnp.exp(-LRU_C * r * softplus)
        bx = jnp.sqrt(1.0 - a * a) * (i * xr)
        h = _scan(a, bx, h0[d:d + 1], reverse=(d == 1))
        if d == 0:
            hf_ref[0:1, :] = h[n - 1:n, :]
            total = h
        else:
            hf_ref[1:2, :] = h[0:1, :]
            total = total + h
    u_ref[...] = (total * _gelu_tanh(lg_ref[...])).astype(BF16)


def _lru_call(l, lx, lg, h0, w, n_seq, seq_len, row_block0):
    seq = lambda: pl.BlockSpec((seq_len, D_RNN), lambda s: (row_block0 + s, 0))
    lay = lambda *shape: _const_spec((None,) + shape, (l,) + (0,) * len(shape))
    return pl.pallas_call(
        _lru_kernel,
        grid=(n_seq,),
        in_specs=[
            seq(), seq(), lay(CONV_W, D_RNN), lay(1, D_RNN), lay(4, 2, D_RNN // 2, D_RNN // 2),
            lay(4, D_RNN), lay(2, D_RNN),
            pl.BlockSpec((None, 2, D_RNN), lambda s: (s, 0, 0)),
        ],
        out_specs=[
            pl.BlockSpec((seq_len, D_RNN), lambda s: (s, 0)),
            pl.BlockSpec((None, 2, D_RNN), lambda s: (s, 0, 0)),
        ],
        out_shape=[
            jax.ShapeDtypeStruct((n_seq * seq_len, D_RNN), BF16),
            jax.ShapeDtypeStruct((n_seq, 2, D_RNN), F32),
        ],
        compiler_params=pltpu.CompilerParams(
            dimension_semantics=("arbitrary",), vmem_limit_bytes=VMEM_LIMIT),
        name="rglru_%d" % seq_len,
    )(lx, lg, w["conv_w"], w["conv_b"], w["wbd"], w["gate_b"], w["lam"], h0)


def _softmax_parts(scores):
    m = scores[0].max(axis=-1, keepdims=True)
    for s in scores[1:]:
        m = jnp.maximum(m, s.max(axis=-1, keepdims=True))
    es = [jnp.exp(s - m) for s in scores]
    tot = es[0].sum(axis=-1, keepdims=True)
    for e in es[1:]:
        tot = tot + e.sum(axis=-1, keepdims=True)
    return es, tot


def _mla_attn_kernel(n_seg, q_ref, *refs):
    k_refs = refs[0:2 * n_seg:2]
    v_refs = refs[1:2 * n_seg:2]
    o_ref = refs[2 * n_seg]
    low = _lane_iota((1, LANES)) < MLA_V
    for pair in range(MLA_HEADS // 2):
        vs = slice(pair * LANES, (pair + 1) * LANES)
        out = None
        for hh in range(2):
            hs = slice((2 * pair + hh) * LANES, (2 * pair + hh + 1) * LANES)
            qh = q_ref[:, hs]
            es, tot = _softmax_parts([_dot_nt(qh, k[:, hs]) for k in k_refs])
            pv = None
            for e, v in zip(es, v_refs):
                t = _dot(e.astype(BF16), v[:, vs])
                pv = t if pv is None else pv + t
            oh = pv / tot
            out = oh if hh == 0 else jnp.where(low, out, oh)
        o_ref[:, vs] = out.astype(BF16)


def _diff_attn_kernel(n_seg, lam_init, q_ref, *refs):
    k_refs = refs[0:2 * n_seg:2]
    v_refs = refs[1:2 * n_seg:2]
    lam_ref, sub_ref, o_ref = refs[2 * n_seg:2 * n_seg + 3]
    lp = lam_ref[...]
    lam = (jnp.exp(jnp.sum(lp[0:1] * lp[1:2], axis=-1, keepdims=True))
           - jnp.exp(jnp.sum(lp[2:3] * lp[3:4], axis=-1, keepdims=True)) + lam_init)
    low = _lane_iota((1, LANES)) < DIFF_DH
    zero = jnp.zeros((), BF16)
    for hd in range(DIFF_HEADS):
        hs = slice(hd * LANES, (hd + 1) * LANES)
        qh = q_ref[:, hs]
        ks = [k[:, hs].astype(BF16) for k in k_refs]
        e0, t0 = _softmax_parts([_dot_nt(jnp.where(low, qh, zero), k) for k in ks])
        e1, t1 = _softmax_parts([_dot_nt(jnp.where(low, zero, qh), k) for k in ks])
        w0 = 1.0 / t0
        w1 = lam / t1
        o = None
        for a, b, v in zip(e0, e1, v_refs):
            t = _dot((a * w0 - b * w1).astype(BF16), v[:, hs].astype(BF16))
            o = t if o is None else o + t
        o = _rms(o) * sub_ref[...] * (1.0 - lam_init)
        o_ref[:, hs] = o.astype(BF16)


def _attn_call(kernel, name, q, segs, extra, extra_specs, q_block0, n_seq, tiles_per_seq, out_width):
    grid = (n_seq, tiles_per_seq)
    seq_len = tiles_per_seq * TILE
    in_specs = [pl.BlockSpec((TILE, q.shape[1]), lambda b, t: (q_block0 * tiles_per_seq + b * tiles_per_seq + t, 0))]
    args = [q]
    for k, v, kind in segs:
        for arr in (k, v):
            if kind == "tok":
                in_specs.append(pl.BlockSpec((seq_len, arr.shape[1]), lambda b, t: (q_block0 + b, 0)))
            else:
                in_specs.append(pl.BlockSpec((None,) + arr.shape[1:], lambda b, t: (b, 0, 0)))
            args.append(arr)
    in_specs += extra_specs
    args += extra
    return pl.pallas_call(
        functools.partial(kernel, len(segs)),
        grid=grid,
        in_specs=in_specs,
        out_specs=pl.BlockSpec((TILE, out_width), lambda b, t: (b * tiles_per_seq + t, 0)),
        out_shape=jax.ShapeDtypeStruct((n_seq * seq_len, out_width), BF16),
        compiler_params=pltpu.CompilerParams(
            dimension_semantics=("arbitrary", "arbitrary"), vmem_limit_bytes=VMEM_LIMIT),
        name=name,
    )(*args)


def _back_kernel(x_ref, mod_ref, n1_ref, n2_ref, omc_ref, oml_ref, ulc_ref, ull_ref, odc_ref, odl_ref,
                 wom_ref, wol_ref, wod_ref, wmg_ref, bmg_ref, wout_ref, rw_ref, rb_ref,
                 x1_ref, h2t_ref, ids_ref, sw_ref, off_ref, ng_ref):
    i = pl.program_id(0)
    is_ctx = i < N_CTX_TILES
    x = x_ref[...]
    mod = mod_ref[...]
    seg = lambda j: mod[:, j * D_MODEL:(j + 1) * D_MODEL]
    hb = _modulated(x, n1_ref[...], seg(0), seg(1)).astype(BF16)
    pick = lambda c_ref, l_ref: jnp.where(is_ctx, c_ref[...], l_ref[...])
    branches = (_dot(pick(omc_ref, oml_ref), wom_ref[...]), _dot(pick(ulc_ref, ull_ref), wol_ref[...]),
                _dot(pick(odc_ref, odl_ref), wod_ref[...]))
    merged = None
    for j, o in enumerate(branches):
        cs = slice(j * D_MODEL, (j + 1) * D_MODEL)
        g = jax.nn.sigmoid(_dot(hb, wmg_ref[:, cs]) + bmg_ref[:, cs])
        merged = g * o if merged is None else merged + g * o
    x1 = x + seg(2) * _dot(merged.astype(BF16), wout_ref[...])
    x1_ref[...] = x1
    h2 = _modulated(x1, n2_ref[...], seg(3), seg(4))
    for s in range(SUB):
        h2t_ref[:, s, :] = h2[:, s * LANES:(s + 1) * LANES]

    h_hi = h2.astype(BF16)
    h_lo = (h2 - h_hi.astype(F32)).astype(BF16)
    rw = rw_ref[...]
    w_hi = rw.astype(BF16)
    w_lo = (rw - w_hi.astype(F32)).astype(BF16)
    logits = _dot(h_hi, w_hi) + (_dot(h_lo, w_hi) + _dot(h_hi, w_lo)) + rb_ref[...]
    lane = _lane_iota(logits.shape).astype(F32)
    neg = jnp.float32(-jnp.inf)
    work = jnp.where(lane < N_EXPERTS, logits, neg)
    sels, probs = [], []
    top = None
    for _ in range(TOP_K):
        m = work.max(axis=-1, keepdims=True)
        idx = jnp.min(jnp.where(work == m, lane, float(LANES)), axis=-1, keepdims=True)
        sel = lane == idx
        top = m if top is None else top
        sels.append(sel)
        probs.append(jnp.exp(m - top))
        work = jnp.where(sel, neg, work)
    denom = probs[0] + probs[1] + probs[2] + probs[3]

    onehot = jnp.zeros_like(logits)
    for sel in sels:
        onehot = onehot + jnp.where(sel, 1.0, 0.0)
    r_i = lax.broadcasted_iota(jnp.int32, (TILE, TILE), 0)
    c_i = lax.broadcasted_iota(jnp.int32, (TILE, TILE), 1)
    earlier = jnp.where(c_i < r_i, 1.0, 0.0).astype(BF16)
    rank = _dot(earlier, onehot.astype(BF16))
    count = jnp.sum(onehot, axis=0, keepdims=True)
    n_gran = jnp.floor((count + (GRAN - 1.0)) * (1.0 / GRAN))
    r_l = lax.broadcasted_iota(jnp.int32, (LANES, LANES), 0)
    c_l = lax.broadcasted_iota(jnp.int32, (LANES, LANES), 1)
    before = jnp.where(r_l < c_l, 1.0, 0.0).astype(BF16)
    seg_off = _dot(jnp.broadcast_to(n_gran * GRAN, (SUB, LANES)).astype(BF16), before)[0:1]
    slot_base = seg_off + rank
    slot_lane = lax.broadcasted_iota(jnp.int32, (1, SLOTS), 1).astype(F32)
    tok_col = (lax.broadcasted_iota(jnp.int32, (TILE, 1), 0) + (i % MOE_TILES) * TILE).astype(F32)
    id_acc = jnp.zeros((TILE, SLOTS), F32)
    w_acc = jnp.zeros((TILE, SLOTS), F32)
    for sel, p in zip(sels, probs):
        slot = jnp.sum(jnp.where(sel, slot_base, 0.0), axis=-1, keepdims=True)
        hit = slot_lane == slot
        id_acc = jnp.where(hit, tok_col, id_acc)
        w_acc = jnp.where(hit, p / denom, w_acc)
    ids_ref[...] = jnp.sum(id_acc, axis=0, keepdims=True).astype(jnp.int32)
    sw_ref[...] = jnp.sum(w_acc, axis=0, keepdims=True)
    off_ref[...] = seg_off.astype(jnp.int32)
    ng_ref[...] = n_gran.astype(jnp.int32)


def _back_call(l, x, mod3, om, ul, od, w):
    tok = lambda width: pl.BlockSpec((TILE, width), lambda i: (i, 0))
    ctx_in = pl.BlockSpec((TILE, 512), lambda i: (jnp.minimum(i, N_CTX_TILES - 1), 0))
    lat_in = pl.BlockSpec((TILE, 512), lambda i: (jnp.maximum(i - N_CTX_TILES, 0), 0))
    lay = lambda *shape: _const_spec((None,) + shape, (l,) + (0,) * len(shape))
    row = lambda width: pl.BlockSpec((None, 1, width), lambda i: (i, 0, 0))
    return pl.pallas_call(
        _back_kernel,
        grid=(N_TILES,),
        in_specs=[
            tok(D_MODEL), _mod_spec(l),
            lay(1, D_MODEL), lay(1, D_MODEL), ctx_in, lat_in, ctx_in, lat_in, ctx_in, lat_in,
            lay(512, D_MODEL), lay(512, D_MODEL), lay(512, D_MODEL),
            lay(D_MODEL, 3 * D_MODEL), lay(1, 3 * D_MODEL), lay(D_MODEL, D_MODEL),
            lay(D_MODEL, LANES), lay(1, LANES),
        ],
        out_specs=[tok(D_MODEL), _tile_spec(), row(SLOTS), row(SLOTS), row(LANES), row(LANES)],
        out_shape=[
            jax.ShapeDtypeStruct((N_TOK, D_MODEL), F32),
            jax.ShapeDtypeStruct((N_TOK, SUB, LANES), F32),
            jax.ShapeDtypeStruct((N_TILES, 1, SLOTS), jnp.int32),
            jax.ShapeDtypeStruct((N_TILES, 1, SLOTS), F32),
            jax.ShapeDtypeStruct((N_TILES, 1, LANES), jnp.int32),
            jax.ShapeDtypeStruct((N_TILES, 1, LANES), jnp.int32),
        ],
        compiler_params=pltpu.CompilerParams(
            dimension_semantics=("arbitrary",), vmem_limit_bytes=VMEM_LIMIT),
        name="back",
    )(x, mod3, w["norm1"], w["norm2"], om[0], om[1], ul[0], ul[1], od[0], od[1],
      w["w_o_mla"], w["w_o_lru"], w["w_o_diff"],
      w["w_merge"], w["b_merge"], w["w_out"], w["router_w"], w["router_b"])


def _moe_kernel(ids_ref, sw_ref, off_ref, ng_ref, h2t_ref, w1_ref, b1_ref, w2_ref, b2_ref,
                y_ref, xg_ref, z_ref, gb_ref):
    e = pl.program_id(1)

    @pl.when(e == 0)
    def _():
        y_ref[...] = jnp.zeros_like(y_ref)
        xg_ref[...] = jnp.zeros_like(xg_ref)

    n_total = 0
    for t in range(MOE_TILES):
        first = t * SLOTS + off_ref[t * LANES + e]

        def add_gran(g, pos, first=first):
            gb_ref[pos] = first + g * GRAN
            return pos + 1

        n_total = lax.fori_loop(0, ng_ref[t * LANES + e], add_gran, n_total)

    def chunk(c, carry):
        g0 = c * CHUNK_GRAN
        n_here = jnp.minimum(n_total - g0, CHUNK_GRAN)

        def gather(g, cc):
            base = gb_ref[g0 + g]
            for u in range(GRAN):
                xg_ref[g * GRAN + u] = h2t_ref[ids_ref[base + u]]
            return cc

        lax.fori_loop(0, n_here, gather, 0)
        xb = _untile(xg_ref).astype(BF16)
        gu = _dot(xb, w1_ref[...]) + b1_ref[...]
        gate = jnp.minimum(gu[:, :D_EXPERT], SWIGLU_LIMIT)
        up = jnp.clip(gu[:, D_EXPERT:], -SWIGLU_LIMIT, SWIGLU_LIMIT)
        act = (up + 1.0) * (gate * jax.nn.sigmoid(SWIGLU_ALPHA * gate))
        y = _dot(act.astype(BF16), w2_ref[...]) + b2_ref[...]
        for s in range(SUB):
            z_ref[:, s, :] = y[:, s * LANES:(s + 1) * LANES]

        def combine(g, cc):
            base = gb_ref[g0 + g]
            toks = [ids_ref[base + u] for u in range(GRAN)]
            new = [y_ref[toks[u]] + sw_ref[base + u] * z_ref[g * GRAN + u] for u in range(GRAN)]
            for u in reversed(range(GRAN)):
                y_ref[toks[u]] = new[u]
            return cc

        lax.fori_loop(0, n_here, combine, 0)
        return carry

    lax.fori_loop(0, lax.shift_right_logical(n_total + (CHUNK_GRAN - 1), CHUNK_SHIFT), chunk, 0)


def _moe_call(l, h2t, ids, sw, seg_off, n_gran, w):
    smem = lambda n: pl.BlockSpec((MOE_TILES * n,), lambda j, e: (j,), memory_space=pltpu.SMEM)
    blk = pl.BlockSpec((MOE_TOK, SUB, LANES), lambda j, e: (j, 0, 0))
    return pl.pallas_call(
        _moe_kernel,
        grid=(N_MOE_BLOCKS, N_EXPERTS),
        in_specs=[
            smem(SLOTS), smem(SLOTS), smem(LANES), smem(LANES),
            pl.BlockSpec((MOE_TOK, SUB, LANES), lambda j, e: (j, 0, 0), pipeline_mode=pl.Buffered(1)),
            pl.BlockSpec((None, None, D_MODEL, 2 * D_EXPERT), lambda j, e: (l, e, 0, 0)),
            pl.BlockSpec((None, None, 1, 2 * D_EXPERT), lambda j, e: (l, e, 0, 0)),
            pl.BlockSpec((None, None, D_EXPERT, D_MODEL), lambda j, e: (l, e, 0, 0)),
            pl.BlockSpec((None, None, 1, D_MODEL), lambda j, e: (l, e, 0, 0)),
        ],
        out_specs=blk,
        out_shape=jax.ShapeDtypeStruct((N_TOK, SUB, LANES), F32),
        scratch_shapes=[
            pltpu.VMEM((MOE_CHUNK, SUB, LANES), F32),
            pltpu.VMEM((MOE_CHUNK, SUB, LANES), F32),
            pltpu.SMEM((MAX_GRAN,), jnp.int32),
        ],
        compiler_params=pltpu.CompilerParams(
            dimension_semantics=("arbitrary", "arbitrary"), vmem_limit_bytes=VMEM_LIMIT),
        name="moe",
    )(ids.reshape(-1), sw.reshape(-1), seg_off.reshape(-1), n_gran.reshape(-1), h2t,
      w["exp_w1"], w["exp_b1"], w["exp_w2"], w["exp_b2"])


def _axial_tables(n_tokens, dim):
    rows = n_tokens // GRID_W
    row = jnp.repeat(jnp.arange(rows), GRID_W)
    col = jnp.tile(jnp.arange(GRID_W), rows)
    half = dim // 2
    inv = 1.0 / (ROPE_BASE ** (jnp.arange(0, half, 2, dtype=F32) / half))

    def axis_angles(pos):
        ang = pos.astype(F32)[:, None] * inv[None, :]
        return jnp.concatenate([ang, ang], axis=-1)

    ang = jnp.concatenate([axis_angles(row), axis_angles(col)], axis=-1)
    return jnp.cos(ang), jnp.sin(ang)


def _rope_slot_tables(dim, lane0, copies):
    cos, sin = _axial_tables(DEC_SEQ, dim)
    quarter = dim // 4
    sign = jnp.where((jnp.arange(dim) % (dim // 2)) < quarter, -1.0, 1.0)
    sin = sin * sign
    cos_slot = jnp.ones((DEC_SEQ, LANES), F32)
    sin_slot = jnp.zeros((DEC_SEQ, LANES), F32)
    for c in range(copies):
        cos_slot = cos_slot.at[:, lane0 + c * dim:lane0 + (c + 1) * dim].set(cos)
        sin_slot = sin_slot.at[:, lane0 + c * dim:lane0 + (c + 1) * dim].set(sin)
    ident = (jnp.ones((1, TILE, LANES), F32), jnp.zeros((1, TILE, LANES), F32))
    cos_t = jnp.concatenate([ident[0], cos_slot.reshape(LAT_TILES_PER_SEQ, TILE, LANES)], axis=0)
    sin_t = jnp.concatenate([ident[1], sin_slot.reshape(LAT_TILES_PER_SEQ, TILE, LANES)], axis=0)
    return cos_t, sin_t


def _pad_last(a, width):
    return jnp.pad(a, [(0, 0)] * (a.ndim - 1) + [(0, width - a.shape[-1])])


def _prepare(p):
    L = DEPTH
    w = {}
    w_in = p["w_in"]
    kr_slot = jnp.pad(w_in[:, :, OFF_KVA + KV_LORA:OFF_LRU_X], ((0, 0), (0, 0), (MLA_NOPE, LANES - MLA_QK)))
    w["w_in"] = jnp.concatenate(
        [w_in[:, :, OFF_QA:OFF_KVA + KV_LORA], kr_slot, w_in[:, :, OFF_LRU_X:IN_COLS]], axis=-1).astype(BF16)
    w["wqb"] = _pad_last(p["w_q_b"].reshape(L, Q_LORA, MLA_HEADS, MLA_QK), LANES).reshape(
        L, Q_LORA, MLA_HEADS * LANES).astype(BF16)
    kvb = p["w_kv_b"].reshape(L, KV_LORA, MLA_HEADS, MLA_NOPE + MLA_V)
    w["wkvk"] = _pad_last(kvb[..., :MLA_NOPE], LANES).reshape(L, KV_LORA, MLA_HEADS * LANES).astype(BF16)
    w["wkvv"] = kvb[..., MLA_NOPE:].reshape(L, KV_LORA, MLA_HEADS * MLA_V).astype(BF16)
    row = lambda a: a[:, None, :]
    w["norm1"] = row(p["norm1_g"])
    w["norm2"] = row(p["norm2_g"])
    w["qa_norm"] = row(p["mla_qa_norm"])
    w["kva_norm"] = row(p["mla_kva_norm"])
    w["qn"] = row(_pad_last(p["mla_qn"], LANES))
    w["kn"] = row(_pad_last(p["mla_kn"], LANES))
    w["dqn"] = row(jnp.tile(p["diff_qn"], (1, 2)))
    w["dkn"] = row(jnp.tile(p["diff_kn"], (1, 2)))
    w["cos_m"], w["sin_m"] = _rope_slot_tables(MLA_ROPE, MLA_NOPE, 1)
    w["cos_d"], w["sin_d"] = _rope_slot_tables(DIFF_DH, 0, 2)
    w["conv_w"] = p["lru_conv_w"]
    w["conv_b"] = row(p["lru_conv_b"])
    per = (D_RNN // 2) // LRU_BW
    gw = p["lru_gate_w"].reshape(L, 4, 2, per, LRU_BW, LRU_BW)
    eye = jnp.eye(per, dtype=F32)
    w["wbd"] = jnp.einsum("lkhacd,ab->lkhacbd", gw, eye).reshape(L, 4, 2, D_RNN // 2, D_RNN // 2).astype(BF16)
    w["gate_b"] = p["lru_gate_b"].reshape(L, 4, D_RNN)
    w["lam"] = p["lru_lambda"]
    w["w_o_mla"] = p["w_o_mla"].astype(BF16)
    w["w_o_lru"] = p["w_o_lru"].astype(BF16)
    w["w_o_diff"] = p["w_o_diff"].astype(BF16)
    w["w_merge"] = p["w_merge"].astype(BF16)
    w["b_merge"] = row(p["b_merge"])
    w["w_out"] = p["w_out"].astype(BF16)
    w["router_w"] = _pad_last(p["router_w"], LANES)
    w["router_b"] = row(_pad_last(p["router_b"], LANES))
    w["exp_w1"] = p["exp_w1"].astype(BF16)
    w["exp_b1"] = p["exp_b1"][:, :, None, :]
    w["exp_w2"] = p["exp_w2"].astype(BF16)
    w["exp_b2"] = p["exp_b2"][:, :, None, :]
    w["diff_lambda"] = p["diff_lambda"]
    w["diff_subln"] = row(p["diff_subln"])
    return w


def kernel(x_prompt, x_sample, cache_mla_ckv, cache_mla_krope, state_lru, cache_diff_k, cache_diff_v,
           c, c_ctx, ada_w, ada_b, norm1_g, norm2_g, w_in, mla_qa_norm, w_q_b, mla_kva_norm, w_kv_b,
           mla_qn, mla_kn, w_o_mla, lru_conv_w, lru_conv_b, lru_gate_w, lru_gate_b, lru_lambda, w_o_lru,
           diff_qn, diff_kn, diff_lambda, diff_subln, w_o_diff, w_merge, b_merge, w_out,
           router_w, router_b, exp_w1, exp_b1, exp_w2, exp_b2):
    params = dict(
        norm1_g=norm1_g, norm2_g=norm2_g, w_in=w_in, mla_qa_norm=mla_qa_norm, w_q_b=w_q_b,
        mla_kva_norm=mla_kva_norm, w_kv_b=w_kv_b, mla_qn=mla_qn, mla_kn=mla_kn, w_o_mla=w_o_mla,
        lru_conv_w=lru_conv_w, lru_conv_b=lru_conv_b, lru_gate_w=lru_gate_w, lru_gate_b=lru_gate_b,
        lru_lambda=lru_lambda, w_o_lru=w_o_lru, diff_qn=diff_qn, diff_kn=diff_kn, diff_lambda=diff_lambda,
        diff_subln=diff_subln, w_o_diff=w_o_diff, w_merge=w_merge, b_merge=b_merge, w_out=w_out,
        router_w=router_w, router_b=router_b, exp_w1=exp_w1, exp_b1=exp_b1, exp_w2=exp_w2, exp_b2=exp_b2)
    w = _prepare(params)

    cond = jnp.concatenate(
        [c, c_ctx[None, :], jnp.zeros((COND_ROWS - DEC_BATCH - 1, D_MODEL), F32)], axis=0)
    mod = _ada_call(cond, ada_w, ada_b[:, None, :])
    mod3 = mod.reshape(DEPTH * COND_ROWS, 1, 6 * D_MODEL)

    place = jnp.pad(jnp.eye(MLA_ROPE, dtype=F32), ((0, 0), (MLA_NOPE, LANES - MLA_QK))).astype(BF16)
    k_ctx, v_ctx = _kvx_call(cache_mla_ckv, cache_mla_krope, w["wkvk"], w["wkvv"], w["kn"], place)
    dk_ctx = cache_diff_k.reshape(DEC_BATCH, DEPTH, PAST_LEN, 512)
    dv_ctx = cache_diff_v.reshape(DEC_BATCH, DEPTH, PAST_LEN, 512)

    x = jnp.concatenate([x_prompt.reshape(N_CTX_TOK, D_MODEL), x_sample.reshape(-1, D_MODEL)], axis=0)
    h0_ctx = jnp.zeros((BATCH, 2, D_RNN), F32)
    lat_blk = N_CTX_TOK // DEC_SEQ
    new_ckv, new_krope, new_lru, new_dk, new_dv = [], [], [], [], []
    moe_out = None
    for l in range(DEPTH):
        lam_init = 0.8 - 0.6 * math.exp(-0.3 * l)
        x, (qm, km, vm, qd, kd, vd, lx, lg, ckv_o, kro_o, dk_o, dv_o) = _front_call(l, x, moe_out, mod3, w)

        u_ctx, hf_ctx = _lru_call(l, lx, lg, h0_ctx, w, BATCH, SEQ, 0)
        u_lat, _ = _lru_call(l, lx, lg, state_lru[:, l], w, DEC_BATCH, DEC_SEQ, lat_blk)

        om_ctx = _attn_call(_mla_attn_kernel, "mla_ctx", qm, [(km, vm, "tok")], [], [], 0, BATCH, 1, 512)
        om_lat = _attn_call(_mla_attn_kernel, "mla_lat", qm,
                            [(k_ctx[l], v_ctx[l], "cache"), (km, vm, "tok")], [], [],
                            lat_blk, DEC_BATCH, LAT_TILES_PER_SEQ, 512)

        dk_extra = [w["diff_lambda"], w["diff_subln"]]
        dk_specs = lambda: [pl.BlockSpec((None, 4, DIFF_DH), lambda b, t: (l, 0, 0)),
                            pl.BlockSpec((None, 1, LANES), lambda b, t: (l, 0, 0))]
        dkern = lambda n_seg, *refs: _diff_attn_kernel(n_seg, lam_init, *refs)
        od_ctx = _attn_call(dkern, "diff_ctx", qd, [(kd, vd, "tok")], dk_extra, dk_specs(),
                            0, BATCH, 1, 512)
        od_lat = _attn_call(dkern, "diff_lat", qd,
                            [(dk_ctx[:, l], dv_ctx[:, l], "cache"), (kd, vd, "tok")], dk_extra, dk_specs(),
                            lat_blk, DEC_BATCH, LAT_TILES_PER_SEQ, 512)

        x, h2t, ids, sw, seg_off, n_gran = _back_call(
            l, x, mod3, (om_ctx, om_lat), (u_ctx, u_lat), (od_ctx, od_lat), w)
        moe_out = _moe_call(l, h2t, ids, sw, seg_off, n_gran, w)

        new_ckv.append(ckv_o[:N_CTX_TOK].reshape(BATCH, SEQ, KV_LORA))
        new_krope.append(kro_o[:N_CTX_TOK].reshape(BATCH, SEQ, MLA_ROPE))
        new_lru.append(hf_ctx)
        new_dk.append(dk_o[:N_CTX_TOK].reshape(BATCH, SEQ, DIFF_HEADS, 2, DIFF_DH))
        new_dv.append(dv_o[:N_CTX_TOK].reshape(BATCH, SEQ, DIFF_HEADS, DIFF_DV))

    x = _residual_call(DEPTH - 1, x, moe_out, mod3)
    xp = x[:N_CTX_TOK].reshape(BATCH, SEQ, D_MODEL)
    xs = x[N_CTX_TOK:].reshape(DEC_BATCH, DEC_SEQ, D_MODEL)
    return (xp, xs, jnp.stack(new_ckv, axis=1), jnp.stack(new_krope, axis=1), jnp.stack(new_lru, axis=1),
            jnp.stack(new_dk, axis=1), jnp.stack(new_dv, axis=1))
```

```python
import functools
import math

import jax
import jax.numpy as jnp
from jax import lax
from jax.experimental import pallas as pl
from jax.experimental.pallas import tpu as pltpu

F32 = jnp.float32
BF16 = jnp.bfloat16

D_MODEL = 1024
BATCH = 16
SEQ = 256
DEPTH = 4
DEC_BATCH = 8
DEC_SEQ = 1024
PAST_LEN = 256
GRID_W = 64
ROPE_BASE = 10000.0
NORM_EPS = 1e-6

MLA_HEADS = 8
MLA_NOPE = 64
MLA_ROPE = 32
MLA_QK = MLA_NOPE + MLA_ROPE
MLA_V = 64
Q_LORA = 384
KV_LORA = 256
D_RNN = 512
LRU_BLOCKS = 8
LRU_BW = D_RNN // LRU_BLOCKS
CONV_W = 4
LRU_C = 8.0
DIFF_HEADS = 4
DIFF_DH = 64
DIFF_DV = 2 * DIFF_DH
N_EXPERTS = 32
TOP_K = 4
D_EXPERT = 512
SWIGLU_LIMIT = 7.0
SWIGLU_ALPHA = 1.702

OFF_QA = 0
OFF_KVA = OFF_QA + Q_LORA
OFF_LRU_X = OFF_KVA + KV_LORA + MLA_ROPE
OFF_LRU_G = OFF_LRU_X + D_RNN
OFF_DQ = OFF_LRU_G + D_RNN
OFF_DK = OFF_DQ + DIFF_HEADS * 2 * DIFF_DH
OFF_DV = OFF_DK + DIFF_HEADS * 2 * DIFF_DH
IN_COLS = OFF_DV + DIFF_HEADS * DIFF_DV

LANES = 128
TILE = 256
N_CTX_TILES = BATCH * SEQ // TILE
LAT_TILES_PER_SEQ = DEC_SEQ // TILE
N_LAT_TILES = DEC_BATCH * LAT_TILES_PER_SEQ
N_TILES = N_CTX_TILES + N_LAT_TILES
N_CTX_TOK = BATCH * SEQ
N_TOK = N_TILES * TILE
COND_ROWS = 16
CTX_COND_ROW = DEC_BATCH

P_QA = 0
P_CKV = P_QA + Q_LORA
P_KR = P_CKV + KV_LORA
P_LX = P_KR + LANES
P_LG = P_LX + D_RNN
P_DQ = P_LG + D_RNN
P_DK = P_DQ + 512
P_DV = P_DK + 512
P_COLS = P_DV + 512

SUB = 8
MOE_TILES = 16
MOE_TOK = MOE_TILES * TILE
N_MOE_BLOCKS = N_TOK // MOE_TOK
GRAN = 8
SLOTS = TILE * TOP_K + N_EXPERTS * GRAN
MOE_CHUNK = 256
CHUNK_GRAN = MOE_CHUNK // GRAN
CHUNK_SHIFT = CHUNK_GRAN.bit_length() - 1
MAX_GRAN = MOE_TILES * SLOTS // GRAN
VMEM_LIMIT = 56 * 1024 * 1024


def _cond_row(i):
    return jnp.where(i < N_CTX_TILES, CTX_COND_ROW, (i - N_CTX_TILES) // LAT_TILES_PER_SEQ)


def _rope_idx(i):
    return jnp.where(i < N_CTX_TILES, 0, 1 + (i - N_CTX_TILES) % LAT_TILES_PER_SEQ)


def _ctx_out_idx(i):
    return jnp.minimum(i, N_CTX_TILES)


def _const_spec(shape, index):
    return pl.BlockSpec(shape, lambda *_: index, pipeline_mode=pl.Buffered(1))


def _dot(a, b):
    return jnp.dot(a, b, preferred_element_type=F32)


def _dot_nt(a, b):
    return lax.dot_general(a, b, (((1,), (1,)), ((), ())), preferred_element_type=F32)


def _rms(x, n=None):
    n = x.shape[-1] if n is None else n
    ss = jnp.sum(x * x, axis=-1, keepdims=True)
    return x * lax.rsqrt(ss * (1.0 / n) + NORM_EPS)


def _rope(x, cos, sin_signed, half, first):
    rot = jnp.where(first, pltpu.roll(x, LANES - half, 1), pltpu.roll(x, half, 1))
    return x * cos + rot * sin_signed


def _lane_iota(shape):
    return lax.broadcasted_iota(jnp.int32, shape, len(shape) - 1)


def _ada_kernel(cond_ref, w_ref, b_ref, o_ref):
    c = cond_ref[...]
    s = c * jax.nn.sigmoid(c)
    o_ref[...] = _dot(s.astype(BF16), w_ref[...].astype(BF16)) + b_ref[...]


def _ada_call(cond, ada_w, ada_b3):
    cb = 1024
    return pl.pallas_call(
        _ada_kernel,
        grid=(DEPTH, 6 * D_MODEL // cb),
        in_specs=[
            pl.BlockSpec((COND_ROWS, D_MODEL), lambda l, j: (0, 0)),
            pl.BlockSpec((None, D_MODEL, cb), lambda l, j: (l, 0, j)),
            pl.BlockSpec((None, 1, cb), lambda l, j: (l, 0, j)),
        ],
        out_specs=pl.BlockSpec((None, COND_ROWS, cb), lambda l, j: (l, 0, j)),
        out_shape=jax.ShapeDtypeStruct((DEPTH, COND_ROWS, 6 * D_MODEL), F32),
        compiler_params=pltpu.CompilerParams(
            dimension_semantics=("arbitrary", "arbitrary"), vmem_limit_bytes=VMEM_LIMIT),
        name="ada_mod",
    )(cond, ada_w, ada_b3)


def _mla_k_heads(kn, krs, kn_gain, cos, sin, first, store):
    for hd in range(MLA_HEADS):
        kh = kn[:, hd * LANES:(hd + 1) * LANES] + krs
        kh = _rms(kh, MLA_QK) * kn_gain
        if cos is not None:
            kh = _rope(kh, cos, sin, MLA_ROPE // 4, first)
        store(hd, kh)


def _kvx_kernel(ckv_ref, kr_ref, wk_ref, wv_ref, kn_ref, place_ref, k_ref, v_ref):
    c = ckv_ref[...].astype(BF16)
    kn = _dot(c, wk_ref[...])
    v_ref[...] = _dot(c, wv_ref[...]).astype(BF16)
    krs = _dot(kr_ref[...].astype(BF16), place_ref[...])

    def store(hd, kh):
        k_ref[:, hd * LANES:(hd + 1) * LANES] = kh.astype(BF16)

    _mla_k_heads(kn, krs, kn_ref[...], None, None, None, store)


def _kvx_call(cache_ckv, cache_krope, wkvk, wkvv, kn_pad, place):
    return pl.pallas_call(
        _kvx_kernel,
        grid=(DEPTH, DEC_BATCH),
        in_specs=[
            pl.BlockSpec((None, None, PAST_LEN, KV_LORA), lambda l, b: (b, l, 0, 0)),
            pl.BlockSpec((None, None, PAST_LEN, MLA_ROPE), lambda l, b: (b, l, 0, 0)),
            pl.BlockSpec((None, KV_LORA, MLA_HEADS * LANES), lambda l, b: (l, 0, 0)),
            pl.BlockSpec((None, KV_LORA, MLA_HEADS * MLA_V), lambda l, b: (l, 0, 0)),
            pl.BlockSpec((None, 1, LANES), lambda l, b: (l, 0, 0)),
            pl.BlockSpec((MLA_ROPE, LANES), lambda l, b: (0, 0)),
        ],
        out_specs=[
            pl.BlockSpec((None, None, PAST_LEN, MLA_HEADS * LANES), lambda l, b: (l, b, 0, 0)),
            pl.BlockSpec((None, None, PAST_LEN, MLA_HEADS * MLA_V), lambda l, b: (l, b, 0, 0)),
        ],
        out_shape=[
            jax.ShapeDtypeStruct((DEPTH, DEC_BATCH, PAST_LEN, MLA_HEADS * LANES), BF16),
            jax.ShapeDtypeStruct((DEPTH, DEC_BATCH, PAST_LEN, MLA_HEADS * MLA_V), BF16),
        ],
        compiler_params=pltpu.CompilerParams(
            dimension_semantics=("arbitrary", "arbitrary"), vmem_limit_bytes=VMEM_LIMIT),
        name="ctx_kv_expand",
    )(cache_ckv, cache_krope, wkvk, wkvv, kn_pad, place)


def _modulated(x, gain, shift, scale):
    return _rms(x) * gain * (1.0 + scale) + shift


def _untile(t_ref):
    rows = t_ref.shape[0] // SUB
    return jnp.concatenate([t_ref[pl.ds(s, rows, stride=SUB), :] for s in range(SUB)], axis=-1)


def _moe_residual(x1_ref, yt_ref, modp_ref):
    return x1_ref[...] + modp_ref[:, 5 * D_MODEL:6 * D_MODEL] * _untile(yt_ref)


def _front_kernel(has_moe, *refs):
    if has_moe:
        x1_ref, yt_ref, modp_ref = refs[:3]
        refs = refs[3:]
    else:
        x_ref = refs[0]
        refs = refs[1:]
    (mod_ref, n1_ref, win_ref, qan_ref, wqb_ref, kvan_ref, wkvk_ref, wkvv_ref,
     qn_ref, kn_ref, dqn_ref, dkn_ref, cosm_ref, sinm_ref, cosd_ref, sind_ref) = refs[:16]
    refs = refs[16:]
    if has_moe:
        xo_ref = refs[0]
        refs = refs[1:]
    (qm_ref, km_ref, vm_ref, qd_ref, kd_ref, vd_ref, lx_ref, lg_ref,
     ckv_ref, kro_ref, dko_ref, dvo_ref) = refs
    if has_moe:
        x = _moe_residual(x1_ref, yt_ref, modp_ref)
        xo_ref[...] = x
    else:
        x = x_ref[...]
    mod = mod_ref[...]
    h = _modulated(x, n1_ref[...], mod[:, 0:D_MODEL], mod[:, D_MODEL:2 * D_MODEL])
    hb = h.astype(BF16)

    lane = _lane_iota((1, LANES))
    first_m = (lane % (MLA_ROPE // 2)) < (MLA_ROPE // 4)
    first_d = (lane % (DIFF_DH // 2)) < (DIFF_DH // 4)
    cosm, sinm = cosm_ref[...], sinm_ref[...]
    cosd, sind = cosd_ref[...], sind_ref[...]

    qa = _rms(_dot(hb, win_ref[:, P_QA:P_QA + Q_LORA])) * qan_ref[...]
    q = _dot(qa.astype(BF16), wqb_ref[...])
    q_scale = MLA_QK ** -0.5
    for hd in range(MLA_HEADS):
        qh = _rms(q[:, hd * LANES:(hd + 1) * LANES], MLA_QK) * qn_ref[...]
        qh = _rope(qh, cosm, sinm, MLA_ROPE // 4, first_m)
        qm_ref[:, hd * LANES:(hd + 1) * LANES] = (qh * q_scale).astype(BF16)

    ckv = _rms(_dot(hb, win_ref[:, P_CKV:P_CKV + KV_LORA])) * kvan_ref[...]
    ckv_ref[...] = ckv
    krs = _dot(hb, win_ref[:, P_KR:P_KR + LANES])
    kro_ref[...] = krs[:, MLA_NOPE:MLA_NOPE + MLA_ROPE]
    cb = ckv.astype(BF16)
    vm_ref[...] = _dot(cb, wkvv_ref[...]).astype(BF16)
    kn = _dot(cb, wkvk_ref[...])

    def store_k(hd, kh):
        km_ref[:, hd * LANES:(hd + 1) * LANES] = kh.astype(BF16)

    _mla_k_heads(kn, krs, kn_ref[...], cosm, sinm, first_m, store_k)

    lx_ref[...] = _dot(hb, win_ref[:, P_LX:P_LX + D_RNN])
    lg_ref[...] = _dot(hb, win_ref[:, P_LG:P_LG + D_RNN])

    low = lane < DIFF_DH

    def pair_norm(t):
        sq = t * t
        s_all = jnp.sum(sq, axis=-1, keepdims=True)
        s_lo = jnp.sum(jnp.where(low, sq, 0.0), axis=-1, keepdims=True)
        r_lo = lax.rsqrt(s_lo * (1.0 / DIFF_DH) + NORM_EPS)
        r_hi = lax.rsqrt((s_all - s_lo) * (1.0 / DIFF_DH) + NORM_EPS)
        return t * jnp.where(low, r_lo, r_hi)

    dq = _dot(hb, win_ref[:, P_DQ:P_DQ + 512])
    dk = _dot(hb, win_ref[:, P_DK:P_DK + 512])
    d_scale = DIFF_DH ** -0.5
    for hd in range(DIFF_HEADS):
        sl = slice(hd * LANES, (hd + 1) * LANES)
        qh = _rope(pair_norm(dq[:, sl]) * dqn_ref[...], cosd, sind, DIFF_DH // 4, first_d)
        qd_ref[:, sl] = (qh * d_scale).astype(BF16)
        kh = _rope(pair_norm(dk[:, sl]) * dkn_ref[...], cosd, sind, DIFF_DH // 4, first_d)
        kd_ref[:, sl] = kh.astype(BF16)
        dko_ref[:, sl] = kh
    dv = _dot(hb, win_ref[:, P_DV:P_DV + 512])
    vd_ref[...] = dv.astype(BF16)
    dvo_ref[...] = dv


def _mod_spec(l):
    return pl.BlockSpec((None, 1, 6 * D_MODEL), lambda i: (l * COND_ROWS + _cond_row(i), 0, 0))


def _tile_spec():
    return pl.BlockSpec((TILE * SUB, LANES), lambda i: (i, 0))


def _front_call(l, x, moe_out, mod3, w):
    tok = lambda width: pl.BlockSpec((TILE, width), lambda i: (i, 0))
    ctx = lambda width: pl.BlockSpec((TILE, width), lambda i: (_ctx_out_idx(i), 0))
    lay = lambda *shape: _const_spec((None,) + shape, (l,) + (0,) * len(shape))
    rope = pl.BlockSpec((None, TILE, LANES), lambda i: (_rope_idx(i), 0, 0))
    n_ctx_rows = (N_CTX_TILES + 1) * TILE
    has_moe = moe_out is not None
    if has_moe:
        lead_specs = [tok(D_MODEL), _tile_spec(), _mod_spec(l - 1)]
        lead_args = [x, moe_out, mod3]
        x_out_specs = [tok(D_MODEL)]
        x_out_shape = [jax.ShapeDtypeStruct((N_TOK, D_MODEL), F32)]
    else:
        lead_specs, lead_args, x_out_specs, x_out_shape = [tok(D_MODEL)], [x], [], []
    outs = pl.pallas_call(
        functools.partial(_front_kernel, has_moe),
        grid=(N_TILES,),
        in_specs=lead_specs + [
            _mod_spec(l),
            lay(1, D_MODEL), lay(D_MODEL, P_COLS), lay(1, Q_LORA), lay(Q_LORA, MLA_HEADS * LANES),
            lay(1, KV_LORA), lay(KV_LORA, MLA_HEADS * LANES), lay(KV_LORA, MLA_HEADS * MLA_V),
            lay(1, LANES), lay(1, LANES), lay(1, LANES), lay(1, LANES),
            rope, rope, rope, rope,
        ],
        out_specs=x_out_specs + [
            tok(1024), tok(1024), tok(512), tok(512), tok(512), tok(512), tok(512), tok(512),
            ctx(KV_LORA), ctx(MLA_ROPE), ctx(512), ctx(512),
        ],
        out_shape=x_out_shape + [
            jax.ShapeDtypeStruct((N_TOK, 1024), BF16), jax.ShapeDtypeStruct((N_TOK, 1024), BF16),
            jax.ShapeDtypeStruct((N_TOK, 512), BF16), jax.ShapeDtypeStruct((N_TOK, 512), BF16),
            jax.ShapeDtypeStruct((N_TOK, 512), BF16), jax.ShapeDtypeStruct((N_TOK, 512), BF16),
            jax.ShapeDtypeStruct((N_TOK, 512), F32), jax.ShapeDtypeStruct((N_TOK, 512), F32),
            jax.ShapeDtypeStruct((n_ctx_rows, KV_LORA), F32),
            jax.ShapeDtypeStruct((n_ctx_rows, MLA_ROPE), F32),
            jax.ShapeDtypeStruct((n_ctx_rows, 512), F32),
            jax.ShapeDtypeStruct((n_ctx_rows, 512), F32),
        ],
        compiler_params=pltpu.CompilerParams(
            dimension_semantics=("arbitrary",), vmem_limit_bytes=VMEM_LIMIT),
        name="front",
    )(*lead_args, mod3, w["norm1"], w["w_in"], w["qa_norm"], w["wqb"], w["kva_norm"], w["wkvk"], w["wkvv"],
      w["qn"], w["kn"], w["dqn"], w["dkn"], w["cos_m"], w["sin_m"], w["cos_d"], w["sin_d"])
    if has_moe:
        return outs[0], outs[1:]
    return x, outs


def _residual_kernel(x1_ref, yt_ref, modp_ref, o_ref):
    o_ref[...] = _moe_residual(x1_ref, yt_ref, modp_ref)


def _residual_call(l, x1, moe_out, mod3):
    tok = pl.BlockSpec((TILE, D_MODEL), lambda i: (i, 0))
    return pl.pallas_call(
        _residual_kernel,
        grid=(N_TILES,),
        in_specs=[tok, _tile_spec(), _mod_spec(l)],
        out_specs=tok,
        out_shape=jax.ShapeDtypeStruct((N_TOK, D_MODEL), F32),
        compiler_params=pltpu.CompilerParams(
            dimension_semantics=("arbitrary",), vmem_limit_bytes=VMEM_LIMIT),
        name="moe_residual",
    )(x1, moe_out, mod3)


def _scan(a, b, h0, reverse):
    n = a.shape[0]
    row = lax.broadcasted_iota(jnp.int32, (n, 1), 0)
    d = 1
    while d < n:
        if reverse:
            valid = row < n - d
            shift = n - d
        else:
            valid = row >= d
            shift = d
        a_s = jnp.where(valid, pltpu.roll(a, shift, 0), 1.0)
        b_s = jnp.where(valid, pltpu.roll(b, shift, 0), 0.0)
        b = a * b_s + b
        a = a * a_s
        d *= 2
    return a * h0 + b


def _gelu_tanh(x):
    return 0.5 * x * (1.0 + jnp.tanh(math.sqrt(2.0 / math.pi) * (x + 0.044715 * x * x * x)))


def _lru_kernel(lx_ref, lg_ref, cw_ref, cb_ref, wbd_ref, gb_ref, lam_ref, h0_ref, u_ref, hf_ref):
    x = lx_ref[...]
    n = x.shape[0]
    row = lax.broadcasted_iota(jnp.int32, (n, 1), 0)
    cw = cw_ref[...]
    xr = cb_ref[...] + cw[2:3] * x
    xr = xr + cw[0:1] * jnp.where(row >= 2, pltpu.roll(x, 2, 0), 0.0)
    xr = xr + cw[1:2] * jnp.where(row >= 1, pltpu.roll(x, 1, 0), 0.0)
    xr = xr + cw[3:4] * jnp.where(row < n - 1, pltpu.roll(x, n - 1, 0), 0.0)
    xb = xr.astype(BF16)
    half = D_RNN // 2
    lam = lam_ref[...]
    h0 = h0_ref[...]
    gb = gb_ref[...]
    total = None
    for d in range(2):
        pre = []
        for g in range(2):
            k = d * 2 + g
            p = jnp.concatenate([_dot(xb[:, :half], wbd_ref[k, 0]), _dot(xb[:, half:], wbd_ref[k, 1])],
                                axis=-1)
            pre.append(p + gb[k:k + 1])
        r = jax.nn.sigmoid(pre[0])
        i = jax.nn.sigmoid(pre[1])
        z = -lam[d:d + 1]
        softplus = jnp.maximum(z, 0.0) + jnp.log(1.0 + jnp.exp(-jnp.abs(z)))
        a = jnp.exp(-LRU_C * r * softplus)
        bx = jnp.sqrt(1.0 - a * a) * (i * xr)
        h = _scan(a, bx, h0[d:d + 1], reverse=(d == 1))
        if d == 0:
            hf_ref[0:1, :] = h[n - 1:n, :]
            total = h
        else:
            hf_ref[1:2, :] = h[0:1, :]
            total = total + h
    u_ref[...] = (total * _gelu_tanh(lg_ref[...])).astype(BF16)


def _lru_call(l, lx, lg, h0, w, n_seq, seq_len, row_block0):
    seq = lambda: pl.BlockSpec((seq_len, D_RNN), lambda s: (row_block0 + s, 0))
    lay = lambda *shape: _const_spec((None,) + shape, (l,) + (0,) * len(shape))
    return pl.pallas_call(
        _lru_kernel,
        grid=(n_seq,),
        in_specs=[
            seq(), seq(), lay(CONV_W, D_RNN), lay(1, D_RNN), lay(4, 2, D_RNN // 2, D_RNN // 2),
            lay(4, D_RNN), lay(2, D_RNN),
            pl.BlockSpec((None, 2, D_RNN), lambda s: (s, 0, 0)),
        ],
        out_specs=[
            pl.BlockSpec((seq_len, D_RNN), lambda s: (s, 0)),
            pl.BlockSpec((None, 2, D_RNN), lambda s: (s, 0, 0)),
        ],
        out_shape=[
            jax.ShapeDtypeStruct((n_seq * seq_len, D_RNN), BF16),
            jax.ShapeDtypeStruct((n_seq, 2, D_RNN), F32),
        ],
        compiler_params=pltpu.CompilerParams(
            dimension_semantics=("arbitrary",), vmem_limit_bytes=VMEM_LIMIT),
        name="rglru_%d" % seq_len,
    )(lx, lg, w["conv_w"], w["conv_b"], w["wbd"], w["gate_b"], w["lam"], h0)


def _softmax_parts(scores):
    m = scores[0].max(axis=-1, keepdims=True)
    for s in scores[1:]:
        m = jnp.maximum(m, s.max(axis=-1, keepdims=True))
    es = [jnp.exp(s - m) for s in scores]
    tot = es[0].sum(axis=-1, keepdims=True)
    for e in es[1:]:
        tot = tot + e.sum(axis=-1, keepdims=True)
    return es, tot


def _mla_attn_kernel(n_seg, q_ref, *refs):
    k_refs = refs[0:2 * n_seg:2]
    v_refs = refs[1:2 * n_seg:2]
    o_ref = refs[2 * n_seg]
    low = _lane_iota((1, LANES)) < MLA_V
    for pair in range(MLA_HEADS // 2):
        vs = slice(pair * LANES, (pair + 1) * LANES)
        out = None
        for hh in range(2):
            hs = slice((2 * pair + hh) * LANES, (2 * pair + hh + 1) * LANES)
            qh = q_ref[:, hs]
            es, tot = _softmax_parts([_dot_nt(qh, k[:, hs]) for k in k_refs])
            pv = None
            for e, v in zip(es, v_refs):
                t = _dot(e.astype(BF16), v[:, vs])
                pv = t if pv is None else pv + t
            oh = pv / tot
            out = oh if hh == 0 else jnp.where(low, out, oh)
        o_ref[:, vs] = out.astype(BF16)


def _diff_attn_kernel(n_seg, lam_init, q_ref, *refs):
    k_refs = refs[0:2 * n_seg:2]
    v_refs = refs[1:2 * n_seg:2]
    lam_ref, sub_ref, o_ref = refs[2 * n_seg:2 * n_seg + 3]
    lp = lam_ref[...]
    lam = (jnp.exp(jnp.sum(lp[0:1] * lp[1:2], axis=-1, keepdims=True))
           - jnp.exp(jnp.sum(lp[2:3] * lp[3:4], axis=-1, keepdims=True)) + lam_init)
    low = _lane_iota((1, LANES)) < DIFF_DH
    zero = jnp.zeros((), BF16)
    for hd in range(DIFF_HEADS):
        hs = slice(hd * LANES, (hd + 1) * LANES)
        qh = q_ref[:, hs]
        ks = [k[:, hs].astype(BF16) for k in k_refs]
        e0, t0 = _softmax_parts([_dot_nt(jnp.where(low, qh, zero), k) for k in ks])
        e1, t1 = _softmax_parts([_dot_nt(jnp.where(low, zero, qh), k) for k in ks])
        w0 = 1.0 / t0
        w1 = lam / t1
        o = None
        for a, b, v in zip(e0, e1, v_refs):
            t = _dot((a * w0 - b * w1).astype(BF16), v[:, hs].astype(BF16))
            o = t if o is None else o + t
        o = _rms(o) * sub_ref[...] * (1.0 - lam_init)
        o_ref[:, hs] = o.astype(BF16)


def _attn_call(kernel, name, q, segs, extra, extra_specs, q_block0, n_seq, tiles_per_seq, out_width):
    grid = (n_seq, tiles_per_seq)
    seq_len = tiles_per_seq * TILE
    in_specs = [pl.BlockSpec((TILE, q.shape[1]), lambda b, t: (q_block0 * tiles_per_seq + b * tiles_per_seq + t, 0))]
    args = [q]
    for k, v, kind in segs:
        for arr in (k, v):
            if kind == "tok":
                in_specs.append(pl.BlockSpec((seq_len, arr.shape[1]), lambda b, t: (q_block0 + b, 0)))
            else:
                in_specs.append(pl.BlockSpec((None,) + arr.shape[1:], lambda b, t: (b, 0, 0)))
            args.append(arr)
    in_specs += extra_specs
    args += extra
    return pl.pallas_call(
        functools.partial(kernel, len(segs)),
        grid=grid,
        in_specs=in_specs,
        out_specs=pl.BlockSpec((TILE, out_width), lambda b, t: (b * tiles_per_seq + t, 0)),
        out_shape=jax.ShapeDtypeStruct((n_seq * seq_len, out_width), BF16),
        compiler_params=pltpu.CompilerParams(
            dimension_semantics=("arbitrary", "arbitrary"), vmem_limit_bytes=VMEM_LIMIT),
        name=name,
    )(*args)


def _back_kernel(x_ref, mod_ref, n1_ref, n2_ref, omc_ref, oml_ref, ulc_ref, ull_ref, odc_ref, odl_ref,
                 wom_ref, wol_ref, wod_ref, wmg_ref, bmg_ref, wout_ref, rw_ref, rb_ref,
                 x1_ref, h2t_ref, ids_ref, sw_ref, off_ref, ng_ref):
    i = pl.program_id(0)
    is_ctx = i < N_CTX_TILES
    x = x_ref[...]
    mod = mod_ref[...]
    seg = lambda j: mod[:, j * D_MODEL:(j + 1) * D_MODEL]
    hb = _modulated(x, n1_ref[...], seg(0), seg(1)).astype(BF16)
    pick = lambda c_ref, l_ref: jnp.where(is_ctx, c_ref[...], l_ref[...])
    branches = (_dot(pick(omc_ref, oml_ref), wom_ref[...]), _dot(pick(ulc_ref, ull_ref), wol_ref[...]),
                _dot(pick(odc_ref, odl_ref), wod_ref[...]))
    merged = None
    for j, o in enumerate(branches):
        cs = slice(j * D_MODEL, (j + 1) * D_MODEL)
        g = jax.nn.sigmoid(_dot(hb, wmg_ref[:, cs]) + bmg_ref[:, cs])
        merged = g * o if merged is None else merged + g * o
    x1 = x + seg(2) * _dot(merged.astype(BF16), wout_ref[...])
    x1_ref[...] = x1
    h2 = _modulated(x1, n2_ref[...], seg(3), seg(4))
    for s in range(SUB):
        h2t_ref[pl.ds(s, TILE, stride=SUB), :] = h2[:, s * LANES:(s + 1) * LANES]

    h_hi = h2.astype(BF16)
    h_lo = (h2 - h_hi.astype(F32)).astype(BF16)
    rw = rw_ref[...]
    w_hi = rw.astype(BF16)
    w_lo = (rw - w_hi.astype(F32)).astype(BF16)
    logits = _dot(h_hi, w_hi) + (_dot(h_lo, w_hi) + _dot(h_hi, w_lo)) + rb_ref[...]
    lane = _lane_iota(logits.shape).astype(F32)
    neg = jnp.float32(-jnp.inf)
    work = jnp.where(lane < N_EXPERTS, logits, neg)
    sels, probs = [], []
    top = None
    for _ in range(TOP_K):
        m = work.max(axis=-1, keepdims=True)
        idx = jnp.min(jnp.where(work == m, lane, float(LANES)), axis=-1, keepdims=True)
        sel = lane == idx
        top = m if top is None else top
        sels.append(sel)
        probs.append(jnp.exp(m - top))
        work = jnp.where(sel, neg, work)
    denom = probs[0] + probs[1] + probs[2] + probs[3]

    onehot = jnp.zeros_like(logits)
    for sel in sels:
        onehot = onehot + jnp.where(sel, 1.0, 0.0)
    r_i = lax.broadcasted_iota(jnp.int32, (TILE, TILE), 0)
    c_i = lax.broadcasted_iota(jnp.int32, (TILE, TILE), 1)
    earlier = jnp.where(c_i < r_i, 1.0, 0.0).astype(BF16)
    rank = _dot(earlier, onehot.astype(BF16))
    count = jnp.sum(onehot, axis=0, keepdims=True)
    n_gran = jnp.floor((count + (GRAN - 1.0)) * (1.0 / GRAN))
    r_l = lax.broadcasted_iota(jnp.int32, (LANES, LANES), 0)
    c_l = lax.broadcasted_iota(jnp.int32, (LANES, LANES), 1)
    before = jnp.where(r_l < c_l, 1.0, 0.0).astype(BF16)
    seg_off = _dot(jnp.broadcast_to(n_gran * GRAN, (SUB, LANES)).astype(BF16), before)[0:1]
    slot_base = seg_off + rank
    slot_lane = lax.broadcasted_iota(jnp.int32, (1, SLOTS), 1).astype(F32)
    tok_col = (lax.broadcasted_iota(jnp.int32, (TILE, 1), 0) + (i % MOE_TILES) * TILE).astype(F32)
    id_acc = jnp.zeros((TILE, SLOTS), F32)
    w_acc = jnp.zeros((TILE, SLOTS), F32)
    for sel, p in zip(sels, probs):
        slot = jnp.sum(jnp.where(sel, slot_base, 0.0), axis=-1, keepdims=True)
        hit = slot_lane == slot
        id_acc = jnp.where(hit, tok_col, id_acc)
        w_acc = jnp.where(hit, p / denom, w_acc)
    ids_ref[...] = jnp.sum(id_acc, axis=0, keepdims=True).astype(jnp.int32)
    sw_ref[...] = jnp.sum(w_acc, axis=0, keepdims=True)
    off_ref[...] = seg_off.astype(jnp.int32)
    ng_ref[...] = n_gran.astype(jnp.int32)


def _back_call(l, x, mod3, om, ul, od, w):
    tok = lambda width: pl.BlockSpec((TILE, width), lambda i: (i, 0))
    ctx_in = pl.BlockSpec((TILE, 512), lambda i: (jnp.minimum(i, N_CTX_TILES - 1), 0))
    lat_in = pl.BlockSpec((TILE, 512), lambda i: (jnp.maximum(i - N_CTX_TILES, 0), 0))
    lay = lambda *shape: _const_spec((None,) + shape, (l,) + (0,) * len(shape))
    row = lambda width: pl.BlockSpec((None, 1, width), lambda i: (i, 0, 0))
    return pl.pallas_call(
        _back_kernel,
        grid=(N_TILES,),
        in_specs=[
            tok(D_MODEL), _mod_spec(l),
            lay(1, D_MODEL), lay(1, D_MODEL), ctx_in, lat_in, ctx_in, lat_in, ctx_in, lat_in,
            lay(512, D_MODEL), lay(512, D_MODEL), lay(512, D_MODEL),
            lay(D_MODEL, 3 * D_MODEL), lay(1, 3 * D_MODEL), lay(D_MODEL, D_MODEL),
            lay(D_MODEL, LANES), lay(1, LANES),
        ],
        out_specs=[tok(D_MODEL), _tile_spec(), row(SLOTS), row(SLOTS), row(LANES), row(LANES)],
        out_shape=[
            jax.ShapeDtypeStruct((N_TOK, D_MODEL), F32),
            jax.ShapeDtypeStruct((N_TOK * SUB, LANES), F32),
            jax.ShapeDtypeStruct((N_TILES, 1, SLOTS), jnp.int32),
            jax.ShapeDtypeStruct((N_TILES, 1, SLOTS), F32),
            jax.ShapeDtypeStruct((N_TILES, 1, LANES), jnp.int32),
            jax.ShapeDtypeStruct((N_TILES, 1, LANES), jnp.int32),
        ],
        compiler_params=pltpu.CompilerParams(
            dimension_semantics=("arbitrary",), vmem_limit_bytes=VMEM_LIMIT),
        name="back",
    )(x, mod3, w["norm1"], w["norm2"], om[0], om[1], ul[0], ul[1], od[0], od[1],
      w["w_o_mla"], w["w_o_lru"], w["w_o_diff"],
      w["w_merge"], w["b_merge"], w["w_out"], w["router_w"], w["router_b"])


def _moe_kernel(ids_ref, sw_ref, off_ref, ng_ref, h2t_ref, w1_ref, b1_ref, w2_ref, b2_ref,
                y_ref, xg_ref, z_ref, gb_ref):
    e = pl.program_id(1)

    @pl.when(e == 0)
    def _():
        y_ref[...] = jnp.zeros_like(y_ref)
        xg_ref[...] = jnp.zeros_like(xg_ref)

    n_total = 0
    for t in range(MOE_TILES):
        first = t * SLOTS + off_ref[t * LANES + e]

        def add_gran(g, pos, first=first):
            gb_ref[pos] = first + g * GRAN
            return pos + 1

        n_total = lax.fori_loop(0, ng_ref[t * LANES + e], add_gran, n_total)

    def slot_rows(j):
        return pl.ds(pl.multiple_of(j * SUB, SUB), SUB)

    def pairs(n, body):
        def two(p, cc):
            body([2 * p, 2 * p + 1])
            return cc

        lax.fori_loop(0, lax.shift_right_logical(n, 1), two, 0)

        @pl.when((n & 1) == 1)
        def _():
            body([n - 1])

    def chunk(c, carry):
        g0 = c * CHUNK_GRAN
        n_here = jnp.minimum(n_total - g0, CHUNK_GRAN)

        def gather(gs):
            for g in gs:
                base = gb_ref[g0 + g]
                for u in range(GRAN):
                    xg_ref[slot_rows(g * GRAN + u), :] = h2t_ref[ids_ref[base + u]]

        pairs(n_here, gather)
        xb = jnp.concatenate([xg_ref[pl.ds(s, MOE_CHUNK, stride=SUB), :] for s in range(SUB)],
                             axis=-1).astype(BF16)
        gu = _dot(xb, w1_ref[...]) + b1_ref[...]
        gate = jnp.minimum(gu[:, :D_EXPERT], SWIGLU_LIMIT)
        up = jnp.clip(gu[:, D_EXPERT:], -SWIGLU_LIMIT, SWIGLU_LIMIT)
        act = (up + 1.0) * (gate * jax.nn.sigmoid(SWIGLU_ALPHA * gate))
        y = _dot(act.astype(BF16), w2_ref[...]) + b2_ref[...]
        for s in range(SUB):
            z_ref[pl.ds(s, MOE_CHUNK, stride=SUB), :] = y[:, s * LANES:(s + 1) * LANES]

        def combine(gs):
            toks, new = [], []
            for g in gs:
                base = gb_ref[g0 + g]
                for u in range(GRAN):
                    tok = ids_ref[base + u]
                    toks.append(tok)
                    new.append(y_ref[tok] + sw_ref[base + u] * z_ref[slot_rows(g * GRAN + u), :])
            for tok, val in reversed(list(zip(toks, new))):
                y_ref[tok] = val

        pairs(n_here, combine)
        return carry

    lax.fori_loop(0, lax.shift_right_logical(n_total + (CHUNK_GRAN - 1), CHUNK_SHIFT), chunk, 0)


def _moe_call(l, h2t, ids, sw, seg_off, n_gran, w):
    smem = lambda n: pl.BlockSpec((MOE_TILES * n,), lambda j, e: (j,), memory_space=pltpu.SMEM,
                                  pipeline_mode=pl.Buffered(1))
    blk = pl.BlockSpec((MOE_TOK, SUB, LANES), lambda j, e: (j, 0, 0), pipeline_mode=pl.Buffered(1))
    return pl.pallas_call(
        _moe_kernel,
        grid=(N_MOE_BLOCKS, N_EXPERTS),
        in_specs=[
            smem(SLOTS), smem(SLOTS), smem(LANES), smem(LANES),
            blk,
            pl.BlockSpec((None, None, D_MODEL, 2 * D_EXPERT), lambda j, e: (l, e, 0, 0)),
            pl.BlockSpec((None, None, 1, 2 * D_EXPERT), lambda j, e: (l, e, 0, 0)),
            pl.BlockSpec((None, None, D_EXPERT, D_MODEL), lambda j, e: (l, e, 0, 0)),
            pl.BlockSpec((None, None, 1, D_MODEL), lambda j, e: (l, e, 0, 0)),
        ],
        out_specs=blk,
        out_shape=jax.ShapeDtypeStruct((N_TOK, SUB, LANES), F32),
        scratch_shapes=[
            pltpu.VMEM((MOE_CHUNK * SUB, LANES), F32),
            pltpu.VMEM((MOE_CHUNK * SUB, LANES), F32),
            pltpu.SMEM((MAX_GRAN,), jnp.int32),
        ],
        compiler_params=pltpu.CompilerParams(
            dimension_semantics=("arbitrary", "arbitrary"), vmem_limit_bytes=VMEM_LIMIT),
        name="moe",
    )(ids.reshape(-1), sw.reshape(-1), seg_off.reshape(-1), n_gran.reshape(-1),
      h2t.reshape(N_TOK, SUB, LANES), w["exp_w1"], w["exp_b1"], w["exp_w2"], w["exp_b2"]
      ).reshape(N_TOK * SUB, LANES)


def _axial_tables(n_tokens, dim):
    rows = n_tokens // GRID_W
    row = jnp.repeat(jnp.arange(rows), GRID_W)
    col = jnp.tile(jnp.arange(GRID_W), rows)
    half = dim // 2
    inv = 1.0 / (ROPE_BASE ** (jnp.arange(0, half, 2, dtype=F32) / half))

    def axis_angles(pos):
        ang = pos.astype(F32)[:, None] * inv[None, :]
        return jnp.concatenate([ang, ang], axis=-1)

    ang = jnp.concatenate([axis_angles(row), axis_angles(col)], axis=-1)
    return jnp.cos(ang), jnp.sin(ang)


def _rope_slot_tables(dim, lane0, copies):
    cos, sin = _axial_tables(DEC_SEQ, dim)
    quarter = dim // 4
    sign = jnp.where((jnp.arange(dim) % (dim // 2)) < quarter, -1.0, 1.0)
    sin = sin * sign
    cos_slot = jnp.ones((DEC_SEQ, LANES), F32)
    sin_slot = jnp.zeros((DEC_SEQ, LANES), F32)
    for c in range(copies):
        cos_slot = cos_slot.at[:, lane0 + c * dim:lane0 + (c + 1) * dim].set(cos)
        sin_slot = sin_slot.at[:, lane0 + c * dim:lane0 + (c + 1) * dim].set(sin)
    ident = (jnp.ones((1, TILE, LANES), F32), jnp.zeros((1, TILE, LANES), F32))
    cos_t = jnp.concatenate([ident[0], cos_slot.reshape(LAT_TILES_PER_SEQ, TILE, LANES)], axis=0)
    sin_t = jnp.concatenate([ident[1], sin_slot.reshape(LAT_TILES_PER_SEQ, TILE, LANES)], axis=0)
    return cos_t, sin_t


def _pad_last(a, width):
    return jnp.pad(a, [(0, 0)] * (a.ndim - 1) + [(0, width - a.shape[-1])])


def _prepare(p):
    L = DEPTH
    w = {}
    w_in = p["w_in"]
    kr_slot = jnp.pad(w_in[:, :, OFF_KVA + KV_LORA:OFF_LRU_X], ((0, 0), (0, 0), (MLA_NOPE, LANES - MLA_QK)))
    w["w_in"] = jnp.concatenate(
        [w_in[:, :, OFF_QA:OFF_KVA + KV_LORA], kr_slot, w_in[:, :, OFF_LRU_X:IN_COLS]], axis=-1).astype(BF16)
    w["wqb"] = _pad_last(p["w_q_b"].reshape(L, Q_LORA, MLA_HEADS, MLA_QK), LANES).reshape(
        L, Q_LORA, MLA_HEADS * LANES).astype(BF16)
    kvb = p["w_kv_b"].reshape(L, KV_LORA, MLA_HEADS, MLA_NOPE + MLA_V)
    w["wkvk"] = _pad_last(kvb[..., :MLA_NOPE], LANES).reshape(L, KV_LORA, MLA_HEADS * LANES).astype(BF16)
    w["wkvv"] = kvb[..., MLA_NOPE:].reshape(L, KV_LORA, MLA_HEADS * MLA_V).astype(BF16)
    row = lambda a: a[:, None, :]
    w["norm1"] = row(p["norm1_g"])
    w["norm2"] = row(p["norm2_g"])
    w["qa_norm"] = row(p["mla_qa_norm"])
    w["kva_norm"] = row(p["mla_kva_norm"])
    w["qn"] = row(_pad_last(p["mla_qn"], LANES))
    w["kn"] = row(_pad_last(p["mla_kn"], LANES))
    w["dqn"] = row(jnp.tile(p["diff_qn"], (1, 2)))
    w["dkn"] = row(jnp.tile(p["diff_kn"], (1, 2)))
    w["cos_m"], w["sin_m"] = _rope_slot_tables(MLA_ROPE, MLA_NOPE, 1)
    w["cos_d"], w["sin_d"] = _rope_slot_tables(DIFF_DH, 0, 2)
    w["conv_w"] = p["lru_conv_w"]
    w["conv_b"] = row(p["lru_conv_b"])
    per = (D_RNN // 2) // LRU_BW
    gw = p["lru_gate_w"].reshape(L, 4, 2, per, LRU_BW, LRU_BW)
    eye = jnp.eye(per, dtype=F32)
    w["wbd"] = jnp.einsum("lkhacd,ab->lkhacbd", gw, eye).reshape(L, 4, 2, D_RNN // 2, D_RNN // 2).astype(BF16)
    w["gate_b"] = p["lru_gate_b"].reshape(L, 4, D_RNN)
    w["lam"] = p["lru_lambda"]
    w["w_o_mla"] = p["w_o_mla"].astype(BF16)
    w["w_o_lru"] = p["w_o_lru"].astype(BF16)
    w["w_o_diff"] = p["w_o_diff"].astype(BF16)
    w["w_merge"] = p["w_merge"].astype(BF16)
    w["b_merge"] = row(p["b_merge"])
    w["w_out"] = p["w_out"].astype(BF16)
    w["router_w"] = _pad_last(p["router_w"], LANES)
    w["router_b"] = row(_pad_last(p["router_b"], LANES))
    w["exp_w1"] = p["exp_w1"].astype(BF16)
    w["exp_b1"] = p["exp_b1"][:, :, None, :]
    w["exp_w2"] = p["exp_w2"].astype(BF16)
    w["exp_b2"] = p["exp_b2"][:, :, None, :]
    w["diff_lambda"] = p["diff_lambda"]
    w["diff_subln"] = row(p["diff_subln"])
    return w


def kernel(x_prompt, x_sample, cache_mla_ckv, cache_mla_krope, state_lru, cache_diff_k, cache_diff_v,
           c, c_ctx, ada_w, ada_b, norm1_g, norm2_g, w_in, mla_qa_norm, w_q_b, mla_kva_norm, w_kv_b,
           mla_qn, mla_kn, w_o_mla, lru_conv_w, lru_conv_b, lru_gate_w, lru_gate_b, lru_lambda, w_o_lru,
           diff_qn, diff_kn, diff_lambda, diff_subln, w_o_diff, w_merge, b_merge, w_out,
           router_w, router_b, exp_w1, exp_b1, exp_w2, exp_b2):
    params = dict(
        norm1_g=norm1_g, norm2_g=norm2_g, w_in=w_in, mla_qa_norm=mla_qa_norm, w_q_b=w_q_b,
        mla_kva_norm=mla_kva_norm, w_kv_b=w_kv_b, mla_qn=mla_qn, mla_kn=mla_kn, w_o_mla=w_o_mla,
        lru_conv_w=lru_conv_w, lru_conv_b=lru_conv_b, lru_gate_w=lru_gate_w, lru_gate_b=lru_gate_b,
        lru_lambda=lru_lambda, w_o_lru=w_o_lru, diff_qn=diff_qn, diff_kn=diff_kn, diff_lambda=diff_lambda,
        diff_subln=diff_subln, w_o_diff=w_o_diff, w_merge=w_merge, b_merge=b_merge, w_out=w_out,
        router_w=router_w, router_b=router_b, exp_w1=exp_w1, exp_b1=exp_b1, exp_w2=exp_w2, exp_b2=exp_b2)
    w = _prepare(params)

    cond = jnp.concatenate(
        [c, c_ctx[None, :], jnp.zeros((COND_ROWS - DEC_BATCH - 1, D_MODEL), F32)], axis=0)
    mod = _ada_call(cond, ada_w, ada_b[:, None, :])
    mod3 = mod.reshape(DEPTH * COND_ROWS, 1, 6 * D_MODEL)

    place = jnp.pad(jnp.eye(MLA_ROPE, dtype=F32), ((0, 0), (MLA_NOPE, LANES - MLA_QK))).astype(BF16)
    k_ctx, v_ctx = _kvx_call(cache_mla_ckv, cache_mla_krope, w["wkvk"], w["wkvv"], w["kn"], place)
    dk_ctx = cache_diff_k.reshape(DEC_BATCH, DEPTH, PAST_LEN, 512)
    dv_ctx = cache_diff_v.reshape(DEC_BATCH, DEPTH, PAST_LEN, 512)

    x = jnp.concatenate([x_prompt.reshape(N_CTX_TOK, D_MODEL), x_sample.reshape(-1, D_MODEL)], axis=0)
    h0_ctx = jnp.zeros((BATCH, 2, D_RNN), F32)
    lat_blk = N_CTX_TOK // DEC_SEQ
    new_ckv, new_krope, new_lru, new_dk, new_dv = [], [], [], [], []
    moe_out = None
    for l in range(DEPTH):
        lam_init = 0.8 - 0.6 * math.exp(-0.3 * l)
        x, (qm, km, vm, qd, kd, vd, lx, lg, ckv_o, kro_o, dk_o, dv_o) = _front_call(l, x, moe_out, mod3, w)

        u_ctx, hf_ctx = _lru_call(l, lx, lg, h0_ctx, w, BATCH, SEQ, 0)
        u_lat, _ = _lru_call(l, lx, lg, state_lru[:, l], w, DEC_BATCH, DEC_SEQ, lat_blk)

        om_ctx = _attn_call(_mla_attn_kernel, "mla_ctx", qm, [(km, vm, "tok")], [], [], 0, BATCH, 1, 512)
        om_lat = _attn_call(_mla_attn_kernel, "mla_lat", qm,
                            [(k_ctx[l], v_ctx[l], "cache"), (km, vm, "tok")], [], [],
                            lat_blk, DEC_BATCH, LAT_TILES_PER_SEQ, 512)

        dk_extra = [w["diff_lambda"], w["diff_subln"]]
        dk_specs = lambda: [pl.BlockSpec((None, 4, DIFF_DH), lambda b, t: (l, 0, 0)),
                            pl.BlockSpec((None, 1, LANES), lambda b, t: (l, 0, 0))]
        dkern = lambda n_seg, *refs: _diff_attn_kernel(n_seg, lam_init, *refs)
        od_ctx = _attn_call(dkern, "diff_ctx", qd, [(kd, vd, "tok")], dk_extra, dk_specs(),
                            0, BATCH, 1, 512)
        od_lat = _attn_call(dkern, "diff_lat", qd,
                            [(dk_ctx[:, l], dv_ctx[:, l], "cache"), (kd, vd, "tok")], dk_extra, dk_specs(),
                            lat_blk, DEC_BATCH, LAT_TILES_PER_SEQ, 512)

        x, h2t, ids, sw, seg_off, n_gran = _back_call(
            l, x, mod3, (om_ctx, om_lat), (u_ctx, u_lat), (od_ctx, od_lat), w)
        moe_out = _moe_call(l, h2t, ids, sw, seg_off, n_gran, w)

        new_ckv.append(ckv_o[:N_CTX_TOK].reshape(BATCH, SEQ, KV_LORA))
        new_krope.append(kro_o[:N_CTX_TOK].reshape(BATCH, SEQ, MLA_ROPE))
        new_lru.append(hf_ctx)
        new_dk.append(dk_o[:N_CTX_TOK].reshape(BATCH, SEQ, DIFF_HEADS, 2, DIFF_DH))
        new_dv.append(dv_o[:N_CTX_TOK].reshape(BATCH, SEQ, DIFF_HEADS, DIFF_DV))

    x = _residual_call(DEPTH - 1, x, moe_out, mod3)
    xp = x[:N_CTX_TOK].reshape(BATCH, SEQ, D_MODEL)
    xs = x[N_CTX_TOK:].reshape(DEC_BATCH, DEC_SEQ, D_MODEL)
    return (xp, xs, jnp.stack(new_ckv, axis=1), jnp.stack(new_krope, axis=1), jnp.stack(new_lru, axis=1),
            jnp.stack(new_dk, axis=1), jnp.stack(new_dv, axis=1))
```

```python
import functools
import math

import jax
import jax.numpy as jnp
from jax import lax
from jax.experimental import pallas as pl
from jax.experimental.pallas import tpu as pltpu

F32 = jnp.float32
BF16 = jnp.bfloat16

D_MODEL = 1024
BATCH = 16
SEQ = 256
DEPTH = 4
DEC_BATCH = 8
DEC_SEQ = 1024
PAST_LEN = 256
GRID_W = 64
ROPE_BASE = 10000.0
NORM_EPS = 1e-6

MLA_HEADS = 8
MLA_NOPE = 64
MLA_ROPE = 32
MLA_QK = MLA_NOPE + MLA_ROPE
MLA_V = 64
Q_LORA = 384
KV_LORA = 256
D_RNN = 512
LRU_BLOCKS = 8
LRU_BW = D_RNN // LRU_BLOCKS
CONV_W = 4
LRU_C = 8.0
DIFF_HEADS = 4
DIFF_DH = 64
DIFF_DV = 2 * DIFF_DH
N_EXPERTS = 32
TOP_K = 4
D_EXPERT = 512
SWIGLU_LIMIT = 7.0
SWIGLU_ALPHA = 1.702

OFF_QA = 0
OFF_KVA = OFF_QA + Q_LORA
OFF_LRU_X = OFF_KVA + KV_LORA + MLA_ROPE
OFF_LRU_G = OFF_LRU_X + D_RNN
OFF_DQ = OFF_LRU_G + D_RNN
OFF_DK = OFF_DQ + DIFF_HEADS * 2 * DIFF_DH
OFF_DV = OFF_DK + DIFF_HEADS * 2 * DIFF_DH
IN_COLS = OFF_DV + DIFF_HEADS * DIFF_DV

LANES = 128
TILE = 256
N_CTX_TILES = BATCH * SEQ // TILE
LAT_TILES_PER_SEQ = DEC_SEQ // TILE
N_LAT_TILES = DEC_BATCH * LAT_TILES_PER_SEQ
N_TILES = N_CTX_TILES + N_LAT_TILES
N_CTX_TOK = BATCH * SEQ
N_TOK = N_TILES * TILE
COND_ROWS = 16
CTX_COND_ROW = DEC_BATCH

P_QA = 0
P_CKV = P_QA + Q_LORA
P_KR = P_CKV + KV_LORA
P_LX = P_KR + LANES
P_LG = P_LX + D_RNN
P_DQ = P_LG + D_RNN
P_DK = P_DQ + 512
P_DV = P_DK + 512
P_COLS = P_DV + 512

SUB = 8
MOE_TILES = 16
MOE_TOK = MOE_TILES * TILE
N_MOE_BLOCKS = N_TOK // MOE_TOK
GRAN = 8
SLOTS = TILE * TOP_K + N_EXPERTS * GRAN
MOE_CHUNK = 256
CHUNK_GRAN = MOE_CHUNK // GRAN
CHUNK_SHIFT = CHUNK_GRAN.bit_length() - 1
MAX_GRAN = MOE_TILES * SLOTS // GRAN
VMEM_LIMIT = 56 * 1024 * 1024


def _cond_row(i):
    return jnp.where(i < N_CTX_TILES, CTX_COND_ROW, (i - N_CTX_TILES) // LAT_TILES_PER_SEQ)


def _rope_idx(i):
    return jnp.where(i < N_CTX_TILES, 0, 1 + (i - N_CTX_TILES) % LAT_TILES_PER_SEQ)


def _ctx_out_idx(i):
    return jnp.minimum(i, N_CTX_TILES - 1)


def _const_spec(shape, index):
    return pl.BlockSpec(shape, lambda *_: index, pipeline_mode=pl.Buffered(1))


def _dot(a, b):
    return jnp.dot(a, b, preferred_element_type=F32)


def _dot_nt(a, b):
    return lax.dot_general(a, b, (((1,), (1,)), ((), ())), preferred_element_type=F32)


def _rms(x, n=None):
    n = x.shape[-1] if n is None else n
    ss = jnp.sum(x * x, axis=-1, keepdims=True)
    return x * lax.rsqrt(ss * (1.0 / n) + NORM_EPS)


def _rope(x, cos, sin_signed, half, first):
    rot = jnp.where(first, pltpu.roll(x, LANES - half, 1), pltpu.roll(x, half, 1))
    return x * cos + rot * sin_signed


def _lane_iota(shape):
    return lax.broadcasted_iota(jnp.int32, shape, len(shape) - 1)


def _ada_kernel(cond_ref, w_ref, b_ref, o_ref):
    c = cond_ref[...]
    s = c * jax.nn.sigmoid(c)
    o_ref[...] = _dot(s.astype(BF16), w_ref[...].astype(BF16)) + b_ref[...]


def _ada_call(cond, ada_w, ada_b3):
    cb = 1024
    return pl.pallas_call(
        _ada_kernel,
        grid=(DEPTH, 6 * D_MODEL // cb),
        in_specs=[
            pl.BlockSpec((COND_ROWS, D_MODEL), lambda l, j: (0, 0)),
            pl.BlockSpec((None, D_MODEL, cb), lambda l, j: (l, 0, j)),
            pl.BlockSpec((None, 1, cb), lambda l, j: (l, 0, j)),
        ],
        out_specs=pl.BlockSpec((None, COND_ROWS, cb), lambda l, j: (l, 0, j)),
        out_shape=jax.ShapeDtypeStruct((DEPTH, COND_ROWS, 6 * D_MODEL), F32),
        compiler_params=pltpu.CompilerParams(
            dimension_semantics=("arbitrary", "arbitrary"), vmem_limit_bytes=VMEM_LIMIT),
        name="ada_mod",
    )(cond, ada_w, ada_b3)


def _mla_k_heads(kn, krs, kn_gain, cos, sin, first, store):
    for hd in range(MLA_HEADS):
        kh = kn[:, hd * LANES:(hd + 1) * LANES] + krs
        kh = _rms(kh, MLA_QK) * kn_gain
        if cos is not None:
            kh = _rope(kh, cos, sin, MLA_ROPE // 4, first)
        store(hd, kh)


def _kvx_kernel(ckv_ref, kr_ref, wk_ref, wv_ref, kn_ref, place_ref, k_ref, v_ref):
    c = ckv_ref[...].astype(BF16)
    kn = _dot(c, wk_ref[...])
    v_ref[...] = _dot(c, wv_ref[...]).astype(BF16)
    krs = _dot(kr_ref[...].astype(BF16), place_ref[...])

    def store(hd, kh):
        k_ref[:, hd * LANES:(hd + 1) * LANES] = kh.astype(BF16)

    _mla_k_heads(kn, krs, kn_ref[...], None, None, None, store)


def _kvx_call(cache_ckv, cache_krope, wkvk, wkvv, kn_pad, place):
    return pl.pallas_call(
        _kvx_kernel,
        grid=(DEPTH, DEC_BATCH),
        in_specs=[
            pl.BlockSpec((None, None, PAST_LEN, KV_LORA), lambda l, b: (b, l, 0, 0)),
            pl.BlockSpec((None, None, PAST_LEN, MLA_ROPE), lambda l, b: (b, l, 0, 0)),
            pl.BlockSpec((None, KV_LORA, MLA_HEADS * LANES), lambda l, b: (l, 0, 0)),
            pl.BlockSpec((None, KV_LORA, MLA_HEADS * MLA_V), lambda l, b: (l, 0, 0)),
            pl.BlockSpec((None, 1, LANES), lambda l, b: (l, 0, 0)),
            pl.BlockSpec((MLA_ROPE, LANES), lambda l, b: (0, 0)),
        ],
        out_specs=[
            pl.BlockSpec((None, None, PAST_LEN, MLA_HEADS * LANES), lambda l, b: (l, b, 0, 0)),
            pl.BlockSpec((None, None, PAST_LEN, MLA_HEADS * MLA_V), lambda l, b: (l, b, 0, 0)),
        ],
        out_shape=[
            jax.ShapeDtypeStruct((DEPTH, DEC_BATCH, PAST_LEN, MLA_HEADS * LANES), BF16),
            jax.ShapeDtypeStruct((DEPTH, DEC_BATCH, PAST_LEN, MLA_HEADS * MLA_V), BF16),
        ],
        compiler_params=pltpu.CompilerParams(
            dimension_semantics=("arbitrary", "arbitrary"), vmem_limit_bytes=VMEM_LIMIT),
        name="ctx_kv_expand",
    )(cache_ckv, cache_krope, wkvk, wkvv, kn_pad, place)


def _modulated(x, gain, shift, scale):
    return _rms(x) * gain * (1.0 + scale) + shift


def _untile(t_ref):
    rows = t_ref.shape[0] // SUB
    return jnp.concatenate([t_ref[pl.ds(s, rows, stride=SUB), :] for s in range(SUB)], axis=-1)


def _moe_residual(x1_ref, yt_ref, modp_ref):
    return x1_ref[...] + modp_ref[:, 5 * D_MODEL:6 * D_MODEL] * _untile(yt_ref)


def _front_kernel(has_moe, *refs):
    if has_moe:
        x1_ref, yt_ref, modp_ref = refs[:3]
        refs = refs[3:]
    else:
        x_ref = refs[0]
        refs = refs[1:]
    (mod_ref, n1_ref, win_ref, qan_ref, wqb_ref, kvan_ref, wkvk_ref, wkvv_ref,
     qn_ref, kn_ref, dqn_ref, dkn_ref, cosm_ref, sinm_ref, cosd_ref, sind_ref) = refs[:16]
    refs = refs[16:]
    if has_moe:
        xo_ref = refs[0]
        refs = refs[1:]
    (qm_ref, km_ref, vm_ref, qd_ref, kd_ref, vd_ref, lx_ref, lg_ref,
     ckv_ref, kro_ref, dko_ref, dvo_ref) = refs
    if has_moe:
        x = _moe_residual(x1_ref, yt_ref, modp_ref)
        xo_ref[...] = x
    else:
        x = x_ref[...]
    mod = mod_ref[...]
    h = _modulated(x, n1_ref[...], mod[:, 0:D_MODEL], mod[:, D_MODEL:2 * D_MODEL])
    hb = h.astype(BF16)

    lane = _lane_iota((1, LANES))
    first_m = (lane % (MLA_ROPE // 2)) < (MLA_ROPE // 4)
    first_d = (lane % (DIFF_DH // 2)) < (DIFF_DH // 4)
    cosm, sinm = cosm_ref[...], sinm_ref[...]
    cosd, sind = cosd_ref[...], sind_ref[...]

    qa = _rms(_dot(hb, win_ref[:, P_QA:P_QA + Q_LORA])) * qan_ref[...]
    q = _dot(qa.astype(BF16), wqb_ref[...])
    q_scale = MLA_QK ** -0.5
    for hd in range(MLA_HEADS):
        qh = _rms(q[:, hd * LANES:(hd + 1) * LANES], MLA_QK) * qn_ref[...]
        qh = _rope(qh, cosm, sinm, MLA_ROPE // 4, first_m)
        qm_ref[:, hd * LANES:(hd + 1) * LANES] = (qh * q_scale).astype(BF16)

    ckv = _rms(_dot(hb, win_ref[:, P_CKV:P_CKV + KV_LORA])) * kvan_ref[...]
    krs = _dot(hb, win_ref[:, P_KR:P_KR + LANES])
    cb = ckv.astype(BF16)
    vm_ref[...] = _dot(cb, wkvv_ref[...]).astype(BF16)
    kn = _dot(cb, wkvk_ref[...])

    def store_k(hd, kh):
        km_ref[:, hd * LANES:(hd + 1) * LANES] = kh.astype(BF16)

    _mla_k_heads(kn, krs, kn_ref[...], cosm, sinm, first_m, store_k)

    lx_ref[...] = _dot(hb, win_ref[:, P_LX:P_LX + D_RNN])
    lg_ref[...] = _dot(hb, win_ref[:, P_LG:P_LG + D_RNN])

    low = lane < DIFF_DH

    def pair_norm(t):
        sq = t * t
        s_all = jnp.sum(sq, axis=-1, keepdims=True)
        s_lo = jnp.sum(jnp.where(low, sq, 0.0), axis=-1, keepdims=True)
        r_lo = lax.rsqrt(s_lo * (1.0 / DIFF_DH) + NORM_EPS)
        r_hi = lax.rsqrt((s_all - s_lo) * (1.0 / DIFF_DH) + NORM_EPS)
        return t * jnp.where(low, r_lo, r_hi)

    dq = _dot(hb, win_ref[:, P_DQ:P_DQ + 512])
    dk = _dot(hb, win_ref[:, P_DK:P_DK + 512])
    d_scale = DIFF_DH ** -0.5
    k_heads = []
    for hd in range(DIFF_HEADS):
        sl = slice(hd * LANES, (hd + 1) * LANES)
        qh = _rope(pair_norm(dq[:, sl]) * dqn_ref[...], cosd, sind, DIFF_DH // 4, first_d)
        qd_ref[:, sl] = (qh * d_scale).astype(BF16)
        kh = _rope(pair_norm(dk[:, sl]) * dkn_ref[...], cosd, sind, DIFF_DH // 4, first_d)
        kd_ref[:, sl] = kh.astype(BF16)
        k_heads.append(kh)
    dv = _dot(hb, win_ref[:, P_DV:P_DV + 512])
    vd_ref[...] = dv.astype(BF16)

    @pl.when(pl.program_id(0) < N_CTX_TILES)
    def _():
        ckv_ref[...] = ckv
        kro_ref[...] = krs[:, MLA_NOPE:MLA_NOPE + MLA_ROPE]
        for hd in range(DIFF_HEADS):
            dko_ref[:, hd * LANES:(hd + 1) * LANES] = k_heads[hd]
        dvo_ref[...] = dv


def _mod_spec(l):
    return pl.BlockSpec((None, 1, 6 * D_MODEL), lambda i: (l * COND_ROWS + _cond_row(i), 0, 0))


def _tile_spec():
    return pl.BlockSpec((TILE * SUB, LANES), lambda i: (i, 0))


def _front_call(l, x, moe_out, mod3, w):
    tok = lambda width: pl.BlockSpec((TILE, width), lambda i: (i, 0))
    ctx = lambda width: pl.BlockSpec((TILE, width), lambda i: (_ctx_out_idx(i), 0))
    lay = lambda *shape: _const_spec((None,) + shape, (l,) + (0,) * len(shape))
    rope = pl.BlockSpec((None, TILE, LANES), lambda i: (_rope_idx(i), 0, 0))
    n_ctx_rows = N_CTX_TOK
    has_moe = moe_out is not None
    if has_moe:
        lead_specs = [tok(D_MODEL), _tile_spec(), _mod_spec(l - 1)]
        lead_args = [x, moe_out, mod3]
        x_out_specs = [tok(D_MODEL)]
        x_out_shape = [jax.ShapeDtypeStruct((N_TOK, D_MODEL), F32)]
    else:
        lead_specs, lead_args, x_out_specs, x_out_shape = [tok(D_MODEL)], [x], [], []
    outs = pl.pallas_call(
        functools.partial(_front_kernel, has_moe),
        grid=(N_TILES,),
        in_specs=lead_specs + [
            _mod_spec(l),
            lay(1, D_MODEL), lay(D_MODEL, P_COLS), lay(1, Q_LORA), lay(Q_LORA, MLA_HEADS * LANES),
            lay(1, KV_LORA), lay(KV_LORA, MLA_HEADS * LANES), lay(KV_LORA, MLA_HEADS * MLA_V),
            lay(1, LANES), lay(1, LANES), lay(1, LANES), lay(1, LANES),
            rope, rope, rope, rope,
        ],
        out_specs=x_out_specs + [
            tok(1024), tok(1024), tok(512), tok(512), tok(512), tok(512), tok(512), tok(512),
            ctx(KV_LORA), ctx(MLA_ROPE), ctx(512), ctx(512),
        ],
        out_shape=x_out_shape + [
            jax.ShapeDtypeStruct((N_TOK, 1024), BF16), jax.ShapeDtypeStruct((N_TOK, 1024), BF16),
            jax.ShapeDtypeStruct((N_TOK, 512), BF16), jax.ShapeDtypeStruct((N_TOK, 512), BF16),
            jax.ShapeDtypeStruct((N_TOK, 512), BF16), jax.ShapeDtypeStruct((N_TOK, 512), BF16),
            jax.ShapeDtypeStruct((N_TOK, 512), F32), jax.ShapeDtypeStruct((N_TOK, 512), F32),
            jax.ShapeDtypeStruct((n_ctx_rows, KV_LORA), F32),
            jax.ShapeDtypeStruct((n_ctx_rows, MLA_ROPE), F32),
            jax.ShapeDtypeStruct((n_ctx_rows, 512), F32),
            jax.ShapeDtypeStruct((n_ctx_rows, 512), F32),
        ],
        compiler_params=pltpu.CompilerParams(
            dimension_semantics=("arbitrary",), vmem_limit_bytes=VMEM_LIMIT),
        name="front",
    )(*lead_args, mod3, w["norm1"], w["w_in"], w["qa_norm"], w["wqb"], w["kva_norm"], w["wkvk"], w["wkvv"],
      w["qn"], w["kn"], w["dqn"], w["dkn"], w["cos_m"], w["sin_m"], w["cos_d"], w["sin_d"])
    if has_moe:
        return outs[0], outs[1:]
    return x, outs


def _residual_kernel(x1_ref, yt_ref, modp_ref, o_ref):
    o_ref[...] = _moe_residual(x1_ref, yt_ref, modp_ref)


def _residual_call(l, x1, moe_out, mod3):
    tok = pl.BlockSpec((TILE, D_MODEL), lambda i: (i, 0))
    return pl.pallas_call(
        _residual_kernel,
        grid=(N_TILES,),
        in_specs=[tok, _tile_spec(), _mod_spec(l)],
        out_specs=tok,
        out_shape=jax.ShapeDtypeStruct((N_TOK, D_MODEL), F32),
        compiler_params=pltpu.CompilerParams(
            dimension_semantics=("arbitrary",), vmem_limit_bytes=VMEM_LIMIT),
        name="moe_residual",
    )(x1, moe_out, mod3)


def _scan(a, b, h0, reverse):
    n = a.shape[0]
    row = lax.broadcasted_iota(jnp.int32, (n, 1), 0)
    d = 1
    while d < n:
        if reverse:
            valid = row < n - d
            shift = n - d
        else:
            valid = row >= d
            shift = d
        a_s = jnp.where(valid, pltpu.roll(a, shift, 0), 1.0)
        b_s = jnp.where(valid, pltpu.roll(b, shift, 0), 0.0)
        b = a * b_s + b
        a = a * a_s
        d *= 2
    return a * h0 + b


def _gelu_tanh(x):
    return 0.5 * x * (1.0 + jnp.tanh(math.sqrt(2.0 / math.pi) * (x + 0.044715 * x * x * x)))


N_LANE_TILES = D_RNN // LANES


def _group_scan(a, b, h0, reverse, sa_ref, sb_ref, sc_ref):
    n = a.shape[0]
    g = n // SUB
    a3 = a.reshape(g, SUB, D_RNN)
    b3 = b.reshape(g, SUB, D_RNN)
    r8 = lax.broadcasted_iota(jnp.int32, (1, SUB, 1), 1)
    d = 1
    while d < SUB:
        if reverse:
            valid, shift = r8 < SUB - d, SUB - d
        else:
            valid, shift = r8 >= d, d
        a_s = jnp.where(valid, pltpu.roll(a3, shift, 1), 1.0)
        b_s = jnp.where(valid, pltpu.roll(b3, shift, 1), 0.0)
        b3 = a3 * b_s + b3
        a3 = a3 * a_s
        d *= 2
    a2 = a3.reshape(n, D_RNN)
    b2 = b3.reshape(n, D_RNN)
    for c in range(N_LANE_TILES):
        sa_ref[pl.ds(c * n, n), :] = a2[:, c * LANES:(c + 1) * LANES]
        sb_ref[pl.ds(c * n, n), :] = b2[:, c * LANES:(c + 1) * LANES]
    last = 0 if reverse else SUB - 1
    ta = jnp.concatenate([sa_ref[pl.ds(c * n + last, g, stride=SUB), :] for c in range(N_LANE_TILES)], axis=-1)
    tb = jnp.concatenate([sb_ref[pl.ds(c * n + last, g, stride=SUB), :] for c in range(N_LANE_TILES)], axis=-1)
    hg = _scan(ta, tb, h0, reverse)
    row = lax.broadcasted_iota(jnp.int32, (g, 1), 0)
    if reverse:
        carry = jnp.where(row == g - 1, h0, pltpu.roll(hg, g - 1, 0))
    else:
        carry = jnp.where(row == 0, h0, pltpu.roll(hg, 1, 0))
    for c in range(N_LANE_TILES):
        sc_ref[pl.ds(c * g, g), :] = carry[:, c * LANES:(c + 1) * LANES]
    return hg


def _lru_kernel(lx_ref, lg_ref, cw_ref, cb_ref, wbd_ref, gb_ref, lam_ref, h0_ref, u_ref, hf_ref,
                sa_ref, sb_ref, sc_ref, tot_ref):
    x = lx_ref[...]
    n = x.shape[0]
    row = lax.broadcasted_iota(jnp.int32, (n, 1), 0)
    cw = cw_ref[...]
    xr = cb_ref[...] + cw[2:3] * x
    xr = xr + cw[0:1] * jnp.where(row >= 2, pltpu.roll(x, 2, 0), 0.0)
    xr = xr + cw[1:2] * jnp.where(row >= 1, pltpu.roll(x, 1, 0), 0.0)
    xr = xr + cw[3:4] * jnp.where(row < n - 1, pltpu.roll(x, n - 1, 0), 0.0)
    xb = xr.astype(BF16)
    half = D_RNN // 2
    lam = lam_ref[...]
    h0 = h0_ref[...]
    gb = gb_ref[...]
    total = None
    for d in range(2):
        pre = []
        for g in range(2):
            k = d * 2 + g
            p = jnp.concatenate([_dot(xb[:, :half], wbd_ref[k, 0]), _dot(xb[:, half:], wbd_ref[k, 1])],
                                axis=-1)
            pre.append(p + gb[k:k + 1])
        r = jax.nn.sigmoid(pre[0])
        i = jax.nn.sigmoid(pre[1])
        z = -lam[d:d + 1]
        softplus = jnp.maximum(z, 0.0) + jnp.log(1.0 + jnp.exp(-jnp.abs(z)))
        a = jnp.exp(-LRU_C * r * softplus)
        bx = jnp.sqrt(1.0 - a * a) * (i * xr)
        g = n // SUB
        hg = _group_scan(a, bx, h0[d:d + 1], d == 1, sa_ref, sb_ref, sc_ref)
        hf_ref[d:d + 1, :] = hg[g - 1:g, :] if d == 0 else hg[0:1, :]
        for c in range(N_LANE_TILES):
            cols = slice(c * LANES, (c + 1) * LANES)
            for j in range(g):
                rows = pl.ds(c * n + j * SUB, SUB)
                h = sa_ref[rows, :] * sc_ref[pl.ds(c * g + j, 1), :] + sb_ref[rows, :]
                if d == 0:
                    tot_ref[pl.ds(j * SUB, SUB), cols] = h
                else:
                    tot_ref[pl.ds(j * SUB, SUB), cols] += h
    u_ref[...] = (tot_ref[...] * _gelu_tanh(lg_ref[...])).astype(BF16)


def _lru_call(l, lx, lg, h0, w, n_seq, seq_len, row_block0):
    seq = lambda: pl.BlockSpec((seq_len, D_RNN), lambda s: (row_block0 + s, 0))
    lay = lambda *shape: _const_spec((None,) + shape, (l,) + (0,) * len(shape))
    return pl.pallas_call(
        _lru_kernel,
        grid=(n_seq,),
        in_specs=[
            seq(), seq(), lay(CONV_W, D_RNN), lay(1, D_RNN), lay(4, 2, D_RNN // 2, D_RNN // 2),
            lay(4, D_RNN), lay(2, D_RNN),
            pl.BlockSpec((None, 2, D_RNN), lambda s: (s, 0, 0)),
        ],
        out_specs=[
            pl.BlockSpec((seq_len, D_RNN), lambda s: (s, 0)),
            pl.BlockSpec((None, 2, D_RNN), lambda s: (s, 0, 0)),
        ],
        out_shape=[
            jax.ShapeDtypeStruct((n_seq * seq_len, D_RNN), BF16),
            jax.ShapeDtypeStruct((n_seq, 2, D_RNN), F32),
        ],
        scratch_shapes=[
            pltpu.VMEM((N_LANE_TILES * seq_len, LANES), F32),
            pltpu.VMEM((N_LANE_TILES * seq_len, LANES), F32),
            pltpu.VMEM((N_LANE_TILES * seq_len // SUB, LANES), F32),
            pltpu.VMEM((seq_len, D_RNN), F32),
        ],
        compiler_params=pltpu.CompilerParams(
            dimension_semantics=("arbitrary",), vmem_limit_bytes=VMEM_LIMIT),
        name="rglru_%d" % seq_len,
    )(lx, lg, w["conv_w"], w["conv_b"], w["wbd"], w["gate_b"], w["lam"], h0)


def _softmax_parts(scores):
    m = scores[0].max(axis=-1, keepdims=True)
    for s in scores[1:]:
        m = jnp.maximum(m, s.max(axis=-1, keepdims=True))
    es = [jnp.exp(s - m) for s in scores]
    tot = es[0].sum(axis=-1, keepdims=True)
    for e in es[1:]:
        tot = tot + e.sum(axis=-1, keepdims=True)
    return es, tot


def _mla_attn_kernel(n_seg, q_ref, *refs):
    k_refs = refs[0:2 * n_seg:2]
    v_refs = refs[1:2 * n_seg:2]
    o_ref = refs[2 * n_seg]
    low = _lane_iota((1, LANES)) < MLA_V
    one = jnp.ones((), BF16)
    for pair in range(MLA_HEADS // 2):
        vs = slice(pair * LANES, (pair + 1) * LANES)
        out = None
        for hh in range(2):
            hs = slice((2 * pair + hh) * LANES, (2 * pair + hh + 1) * LANES)
            qh = q_ref[:, hs]
            scores = [_dot_nt(qh, k[:, hs]) for k in k_refs]
            m = scores[0].max(axis=-1, keepdims=True)
            for s in scores[1:]:
                m = jnp.maximum(m, s.max(axis=-1, keepdims=True))
            pv = None
            for s, v in zip(scores, v_refs):
                v_aug = jnp.where(low, v[:, vs], one) if hh == 0 else jnp.where(low, one, v[:, vs])
                t = _dot(jnp.exp(s - m).astype(BF16), v_aug)
                pv = t if pv is None else pv + t
            tot = pv[:, MLA_V:MLA_V + 1] if hh == 0 else pv[:, 0:1]
            oh = pv / tot
            out = oh if hh == 0 else jnp.where(low, out, oh)
        o_ref[:, vs] = out.astype(BF16)


def _diff_attn_kernel(n_seg, lam_init, q_ref, *refs):
    k_refs = refs[0:2 * n_seg:2]
    v_refs = refs[1:2 * n_seg:2]
    lam_ref, sub_ref, o_ref = refs[2 * n_seg:2 * n_seg + 3]
    lp = lam_ref[...]
    lam = (jnp.exp(jnp.sum(lp[0:1] * lp[1:2], axis=-1, keepdims=True))
           - jnp.exp(jnp.sum(lp[2:3] * lp[3:4], axis=-1, keepdims=True)) + lam_init)
    low = _lane_iota((1, LANES)) < DIFF_DH
    zero = jnp.zeros((), BF16)
    for hd in range(DIFF_HEADS):
        hs = slice(hd * LANES, (hd + 1) * LANES)
        qh = q_ref[:, hs]
        ks = [k[:, hs].astype(BF16) for k in k_refs]
        e0, t0 = _softmax_parts([_dot_nt(jnp.where(low, qh, zero), k) for k in ks])
        e1, t1 = _softmax_parts([_dot_nt(jnp.where(low, zero, qh), k) for k in ks])
        w0 = 1.0 / t0
        w1 = lam / t1
        o = None
        for a, b, v in zip(e0, e1, v_refs):
            t = _dot((a * w0 - b * w1).astype(BF16), v[:, hs].astype(BF16))
            o = t if o is None else o + t
        o = _rms(o) * sub_ref[...] * (1.0 - lam_init)
        o_ref[:, hs] = o.astype(BF16)


def _attn_call(kernel, name, q, segs, extra, extra_specs, q_block0, n_seq, tiles_per_seq, out_width):
    grid = (n_seq, tiles_per_seq)
    seq_len = tiles_per_seq * TILE
    in_specs = [pl.BlockSpec((TILE, q.shape[1]), lambda b, t: (q_block0 * tiles_per_seq + b * tiles_per_seq + t, 0))]
    args = [q]
    for k, v, kind in segs:
        for arr in (k, v):
            if kind == "tok":
                in_specs.append(pl.BlockSpec((seq_len, arr.shape[1]), lambda b, t: (q_block0 + b, 0)))
            else:
                in_specs.append(pl.BlockSpec((None, None) + arr.shape[2:], kind))
            args.append(arr)
    in_specs += extra_specs
    args += extra
    return pl.pallas_call(
        functools.partial(kernel, len(segs)),
        grid=grid,
        in_specs=in_specs,
        out_specs=pl.BlockSpec((TILE, out_width), lambda b, t: (b * tiles_per_seq + t, 0)),
        out_shape=jax.ShapeDtypeStruct((n_seq * seq_len, out_width), BF16),
        compiler_params=pltpu.CompilerParams(
            dimension_semantics=("arbitrary", "arbitrary"), vmem_limit_bytes=VMEM_LIMIT),
        name=name,
    )(*args)


def _back_kernel(x_ref, mod_ref, n1_ref, n2_ref, omc_ref, oml_ref, ulc_ref, ull_ref, odc_ref, odl_ref,
                 wom_ref, wol_ref, wod_ref, wmg_ref, bmg_ref, wout_ref, rw_ref, rb_ref,
                 x1_ref, h2t_ref, ids_ref, sw_ref, off_ref, ng_ref):
    i = pl.program_id(0)
    is_ctx = i < N_CTX_TILES
    x = x_ref[...]
    mod = mod_ref[...]
    seg = lambda j: mod[:, j * D_MODEL:(j + 1) * D_MODEL]
    hb = _modulated(x, n1_ref[...], seg(0), seg(1)).astype(BF16)
    pick = lambda c_ref, l_ref: jnp.where(is_ctx, c_ref[...], l_ref[...])
    branches = (_dot(pick(omc_ref, oml_ref), wom_ref[...]), _dot(pick(ulc_ref, ull_ref), wol_ref[...]),
                _dot(pick(odc_ref, odl_ref), wod_ref[...]))
    merged = None
    for j, o in enumerate(branches):
        cs = slice(j * D_MODEL, (j + 1) * D_MODEL)
        g = jax.nn.sigmoid(_dot(hb, wmg_ref[:, cs]) + bmg_ref[:, cs])
        merged = g * o if merged is None else merged + g * o
    x1 = x + seg(2) * _dot(merged.astype(BF16), wout_ref[...])
    x1_ref[...] = x1
    h2 = _modulated(x1, n2_ref[...], seg(3), seg(4))
    for s in range(SUB):
        h2t_ref[pl.ds(s, TILE, stride=SUB), :] = h2[:, s * LANES:(s + 1) * LANES]

    h_hi = h2.astype(BF16)
    h_lo = (h2 - h_hi.astype(F32)).astype(BF16)
    rw = rw_ref[...]
    w_hi = rw.astype(BF16)
    w_lo = (rw - w_hi.astype(F32)).astype(BF16)
    logits = _dot(h_hi, w_hi) + (_dot(h_lo, w_hi) + _dot(h_hi, w_lo)) + rb_ref[...]
    lane = _lane_iota(logits.shape).astype(F32)
    neg = jnp.float32(-jnp.inf)
    work = jnp.where(lane < N_EXPERTS, logits, neg)
    sels, probs = [], []
    top = None
    for _ in range(TOP_K):
        m = work.max(axis=-1, keepdims=True)
        idx = jnp.min(jnp.where(work == m, lane, float(LANES)), axis=-1, keepdims=True)
        sel = lane == idx
        top = m if top is None else top
        sels.append(sel)
        probs.append(jnp.exp(m - top))
        work = jnp.where(sel, neg, work)
    denom = probs[0] + probs[1] + probs[2] + probs[3]

    onehot = jnp.zeros_like(logits)
    for sel in sels:
        onehot = onehot + jnp.where(sel, 1.0, 0.0)
    r_i = lax.broadcasted_iota(jnp.int32, (TILE, TILE), 0)
    c_i = lax.broadcasted_iota(jnp.int32, (TILE, TILE), 1)
    earlier = jnp.where(c_i < r_i, 1.0, 0.0).astype(BF16)
    rank = _dot(earlier, onehot.astype(BF16))
    count = jnp.sum(onehot, axis=0, keepdims=True)
    n_gran = jnp.floor((count + (GRAN - 1.0)) * (1.0 / GRAN))
    r_l = lax.broadcasted_iota(jnp.int32, (LANES, LANES), 0)
    c_l = lax.broadcasted_iota(jnp.int32, (LANES, LANES), 1)
    before = jnp.where(r_l < c_l, 1.0, 0.0).astype(BF16)
    seg_off = _dot(jnp.broadcast_to(n_gran * GRAN, (SUB, LANES)).astype(BF16), before)[0:1]
    slot_base = seg_off + rank
    slot_lane = lax.broadcasted_iota(jnp.int32, (1, SLOTS), 1).astype(F32)
    tok_col = (lax.broadcasted_iota(jnp.int32, (TILE, 1), 0) + (i % MOE_TILES) * TILE).astype(F32)
    id_acc = jnp.zeros((TILE, SLOTS), F32)
    w_acc = jnp.zeros((TILE, SLOTS), F32)
    for sel, p in zip(sels, probs):
        slot = jnp.sum(jnp.where(sel, slot_base, 0.0), axis=-1, keepdims=True)
        hit = slot_lane == slot
        id_acc = jnp.where(hit, tok_col, id_acc)
        w_acc = jnp.where(hit, p / denom, w_acc)
    ids_ref[...] = jnp.sum(id_acc, axis=0, keepdims=True).astype(jnp.int32)
    sw_ref[...] = jnp.sum(w_acc, axis=0, keepdims=True)
    off_ref[...] = seg_off.astype(jnp.int32)
    ng_ref[...] = n_gran.astype(jnp.int32)


def _back_call(l, x, mod3, om, ul, od, w):
    tok = lambda width: pl.BlockSpec((TILE, width), lambda i: (i, 0))
    ctx_in = pl.BlockSpec((TILE, 512), lambda i: (jnp.minimum(i, N_CTX_TILES - 1), 0))
    lat_in = pl.BlockSpec((TILE, 512), lambda i: (jnp.maximum(i - N_CTX_TILES, 0), 0))
    lay = lambda *shape: _const_spec((None,) + shape, (l,) + (0,) * len(shape))
    row = lambda width: pl.BlockSpec((None, 1, width), lambda i: (i, 0, 0))
    return pl.pallas_call(
        _back_kernel,
        grid=(N_TILES,),
        in_specs=[
            tok(D_MODEL), _mod_spec(l),
            lay(1, D_MODEL), lay(1, D_MODEL), ctx_in, lat_in, ctx_in, lat_in, ctx_in, lat_in,
            lay(512, D_MODEL), lay(512, D_MODEL), lay(512, D_MODEL),
            lay(D_MODEL, 3 * D_MODEL), lay(1, 3 * D_MODEL), lay(D_MODEL, D_MODEL),
            lay(D_MODEL, LANES), lay(1, LANES),
        ],
        out_specs=[tok(D_MODEL), _tile_spec(), row(SLOTS), row(SLOTS), row(LANES), row(LANES)],
        out_shape=[
            jax.ShapeDtypeStruct((N_TOK, D_MODEL), F32),
            jax.ShapeDtypeStruct((N_TOK * SUB, LANES), F32),
            jax.ShapeDtypeStruct((N_TILES, 1, SLOTS), jnp.int32),
            jax.ShapeDtypeStruct((N_TILES, 1, SLOTS), F32),
            jax.ShapeDtypeStruct((N_TILES, 1, LANES), jnp.int32),
            jax.ShapeDtypeStruct((N_TILES, 1, LANES), jnp.int32),
        ],
        compiler_params=pltpu.CompilerParams(
            dimension_semantics=("arbitrary",), vmem_limit_bytes=VMEM_LIMIT),
        name="back",
    )(x, mod3, w["norm1"], w["norm2"], om[0], om[1], ul[0], ul[1], od[0], od[1],
      w["w_o_mla"], w["w_o_lru"], w["w_o_diff"],
      w["w_merge"], w["b_merge"], w["w_out"], w["router_w"], w["router_b"])


def _moe_kernel(ids_ref, sw_ref, off_ref, ng_ref, h2t_ref, w1_ref, b1_ref, w2_ref, b2_ref,
                y_ref, xga_ref, xgb_ref, za_ref, zb_ref, gb_ref):
    e = pl.program_id(1)

    @pl.when(e == 0)
    def _():
        y_ref[...] = jnp.zeros_like(y_ref)
        for ref in (xga_ref, xgb_ref, za_ref, zb_ref):
            ref[...] = jnp.zeros_like(ref)

    n_total = 0
    for t in range(MOE_TILES):
        first = t * SLOTS + off_ref[t * LANES + e]

        def add_gran(g, pos, first=first):
            gb_ref[pos] = first + g * GRAN
            return pos + 1

        n_total = lax.fori_loop(0, ng_ref[t * LANES + e], add_gran, n_total)
    for i in range(2 * CHUNK_GRAN):
        gb_ref[n_total + i] = SLOTS - GRAN
    n_chunks = lax.shift_right_logical(n_total + (CHUNK_GRAN - 1), CHUNK_SHIFT)

    def slot_rows(j):
        return pl.ds(j * SUB, SUB)

    def gather(t, xg_ref):
        for g in range(CHUNK_GRAN):
            base = gb_ref[t * CHUNK_GRAN + g]
            for u in range(GRAN):
                xg_ref[slot_rows(g * GRAN + u), :] = h2t_ref[ids_ref[base + u]]

    def expert(xg_ref, z_ref):
        xb = jnp.concatenate([xg_ref[pl.ds(s, MOE_CHUNK, stride=SUB), :] for s in range(SUB)],
                             axis=-1).astype(BF16)
        gu = _dot(xb, w1_ref[...]) + b1_ref[...]
        gate = jnp.minimum(gu[:, :D_EXPERT], SWIGLU_LIMIT)
        up = jnp.clip(gu[:, D_EXPERT:], -SWIGLU_LIMIT, SWIGLU_LIMIT)
        act = (up + 1.0) * (gate * jax.nn.sigmoid(SWIGLU_ALPHA * gate))
        y = _dot(act.astype(BF16), w2_ref[...]) + b2_ref[...]
        for s in range(SUB):
            z_ref[pl.ds(s, MOE_CHUNK, stride=SUB), :] = y[:, s * LANES:(s + 1) * LANES]

    def combine(t, z_ref):
        for g in range(CHUNK_GRAN):
            base = gb_ref[t * CHUNK_GRAN + g]
            toks = [ids_ref[base + u] for u in range(GRAN)]
            new = [y_ref[toks[u]] + sw_ref[base + u] * z_ref[slot_rows(g * GRAN + u), :] for u in range(GRAN)]
            for u in reversed(range(GRAN)):
                y_ref[toks[u]] = new[u]

    gather(0, xga_ref)

    @pl.when(n_chunks > 0)
    def _():
        expert(xga_ref, za_ref)
        gather(1, xgb_ref)

    def tick_pair(p, carry):
        t = 2 * p + 1
        expert(xgb_ref, zb_ref)
        gather(t + 1, xga_ref)
        combine(t - 1, za_ref)

        @pl.when(t + 1 < n_chunks)
        def _():
            expert(xga_ref, za_ref)
            gather(t + 2, xgb_ref)
            combine(t, zb_ref)

        return carry

    lax.fori_loop(0, lax.shift_right_logical(n_chunks, 1), tick_pair, 0)

    @pl.when((n_chunks & 1) == 1)
    def _():
        combine(n_chunks - 1, za_ref)

    @pl.when(jnp.logical_and(n_chunks > 0, (n_chunks & 1) == 0))
    def _():
        combine(n_chunks - 1, zb_ref)


def _moe_call(l, h2t, ids, sw, seg_off, n_gran, w):
    smem = lambda n: pl.BlockSpec((MOE_TILES * n,), lambda j, e: (j,), memory_space=pltpu.SMEM,
                                  pipeline_mode=pl.Buffered(1))
    blk = pl.BlockSpec((MOE_TOK, SUB, LANES), lambda j, e: (j, 0, 0), pipeline_mode=pl.Buffered(1))
    return pl.pallas_call(
        _moe_kernel,
        grid=(N_MOE_BLOCKS, N_EXPERTS),
        in_specs=[
            smem(SLOTS), smem(SLOTS), smem(LANES), smem(LANES),
            blk,
            pl.BlockSpec((None, None, D_MODEL, 2 * D_EXPERT), lambda j, e: (l, e, 0, 0)),
            pl.BlockSpec((None, None, 1, 2 * D_EXPERT), lambda j, e: (l, e, 0, 0)),
            pl.BlockSpec((None, None, D_EXPERT, D_MODEL), lambda j, e: (l, e, 0, 0)),
            pl.BlockSpec((None, None, 1, D_MODEL), lambda j, e: (l, e, 0, 0)),
        ],
        out_specs=blk,
        out_shape=jax.ShapeDtypeStruct((N_TOK, SUB, LANES), F32),
        scratch_shapes=[pltpu.VMEM((MOE_CHUNK * SUB, LANES), F32)] * 4
        + [pltpu.SMEM((MAX_GRAN + 2 * CHUNK_GRAN,), jnp.int32)],
        compiler_params=pltpu.CompilerParams(
            dimension_semantics=("arbitrary", "arbitrary"), vmem_limit_bytes=VMEM_LIMIT),
        name="moe",
    )(ids.reshape(-1), sw.reshape(-1), seg_off.reshape(-1), n_gran.reshape(-1),
      h2t.reshape(N_TOK, SUB, LANES), w["exp_w1"], w["exp_b1"], w["exp_w2"], w["exp_b2"]
      ).reshape(N_TOK * SUB, LANES)


def _axial_tables(n_tokens, dim):
    rows = n_tokens // GRID_W
    row = jnp.repeat(jnp.arange(rows), GRID_W)
    col = jnp.tile(jnp.arange(GRID_W), rows)
    half = dim // 2
    inv = 1.0 / (ROPE_BASE ** (jnp.arange(0, half, 2, dtype=F32) / half))

    def axis_angles(pos):
        ang = pos.astype(F32)[:, None] * inv[None, :]
        return jnp.concatenate([ang, ang], axis=-1)

    ang = jnp.concatenate([axis_angles(row), axis_angles(col)], axis=-1)
    return jnp.cos(ang), jnp.sin(ang)


def _rope_slot_tables(dim, lane0, copies):
    cos, sin = _axial_tables(DEC_SEQ, dim)
    quarter = dim // 4
    sign = jnp.where((jnp.arange(dim) % (dim // 2)) < quarter, -1.0, 1.0)
    sin = sin * sign
    cos_slot = jnp.ones((DEC_SEQ, LANES), F32)
    sin_slot = jnp.zeros((DEC_SEQ, LANES), F32)
    for c in range(copies):
        cos_slot = cos_slot.at[:, lane0 + c * dim:lane0 + (c + 1) * dim].set(cos)
        sin_slot = sin_slot.at[:, lane0 + c * dim:lane0 + (c + 1) * dim].set(sin)
    ident = (jnp.ones((1, TILE, LANES), F32), jnp.zeros((1, TILE, LANES), F32))
    cos_t = jnp.concatenate([ident[0], cos_slot.reshape(LAT_TILES_PER_SEQ, TILE, LANES)], axis=0)
    sin_t = jnp.concatenate([ident[1], sin_slot.reshape(LAT_TILES_PER_SEQ, TILE, LANES)], axis=0)
    return cos_t, sin_t


def _pad_last(a, width):
    return jnp.pad(a, [(0, 0)] * (a.ndim - 1) + [(0, width - a.shape[-1])])


def _prepare(p):
    L = DEPTH
    w = {}
    w_in = p["w_in"]
    kr_slot = jnp.pad(w_in[:, :, OFF_KVA + KV_LORA:OFF_LRU_X], ((0, 0), (0, 0), (MLA_NOPE, LANES - MLA_QK)))
    w["w_in"] = jnp.concatenate(
        [w_in[:, :, OFF_QA:OFF_KVA + KV_LORA], kr_slot, w_in[:, :, OFF_LRU_X:IN_COLS]], axis=-1).astype(BF16)
    w["wqb"] = _pad_last(p["w_q_b"].reshape(L, Q_LORA, MLA_HEADS, MLA_QK), LANES).reshape(
        L, Q_LORA, MLA_HEADS * LANES).astype(BF16)
    kvb = p["w_kv_b"].reshape(L, KV_LORA, MLA_HEADS, MLA_NOPE + MLA_V)
    w["wkvk"] = _pad_last(kvb[..., :MLA_NOPE], LANES).reshape(L, KV_LORA, MLA_HEADS * LANES).astype(BF16)
    w["wkvv"] = kvb[..., MLA_NOPE:].reshape(L, KV_LORA, MLA_HEADS * MLA_V).astype(BF16)
    row = lambda a: a[:, None, :]
    w["norm1"] = row(p["norm1_g"])
    w["norm2"] = row(p["norm2_g"])
    w["qa_norm"] = row(p["mla_qa_norm"])
    w["kva_norm"] = row(p["mla_kva_norm"])
    w["qn"] = row(_pad_last(p["mla_qn"], LANES))
    w["kn"] = row(_pad_last(p["mla_kn"], LANES))
    w["dqn"] = row(jnp.tile(p["diff_qn"], (1, 2)))
    w["dkn"] = row(jnp.tile(p["diff_kn"], (1, 2)))
    w["cos_m"], w["sin_m"] = _rope_slot_tables(MLA_ROPE, MLA_NOPE, 1)
    w["cos_d"], w["sin_d"] = _rope_slot_tables(DIFF_DH, 0, 2)
    w["conv_w"] = p["lru_conv_w"]
    w["conv_b"] = row(p["lru_conv_b"])
    per = (D_RNN // 2) // LRU_BW
    gw = p["lru_gate_w"].reshape(L, 4, 2, per, LRU_BW, LRU_BW)
    eye = jnp.eye(per, dtype=F32)
    w["wbd"] = jnp.einsum("lkhacd,ab->lkhacbd", gw, eye).reshape(L, 4, 2, D_RNN // 2, D_RNN // 2).astype(BF16)
    w["gate_b"] = p["lru_gate_b"].reshape(L, 4, D_RNN)
    w["lam"] = p["lru_lambda"]
    w["w_o_mla"] = p["w_o_mla"].astype(BF16)
    w["w_o_lru"] = p["w_o_lru"].astype(BF16)
    w["w_o_diff"] = p["w_o_diff"].astype(BF16)
    w["w_merge"] = p["w_merge"].astype(BF16)
    w["b_merge"] = row(p["b_merge"])
    w["w_out"] = p["w_out"].astype(BF16)
    w["router_w"] = _pad_last(p["router_w"], LANES)
    w["router_b"] = row(_pad_last(p["router_b"], LANES))
    w["exp_w1"] = p["exp_w1"].astype(BF16)
    w["exp_b1"] = p["exp_b1"][:, :, None, :]
    w["exp_w2"] = p["exp_w2"].astype(BF16)
    w["exp_b2"] = p["exp_b2"][:, :, None, :]
    w["diff_lambda"] = p["diff_lambda"]
    w["diff_subln"] = row(p["diff_subln"])
    return w


def kernel(x_prompt, x_sample, cache_mla_ckv, cache_mla_krope, state_lru, cache_diff_k, cache_diff_v,
           c, c_ctx, ada_w, ada_b, norm1_g, norm2_g, w_in, mla_qa_norm, w_q_b, mla_kva_norm, w_kv_b,
           mla_qn, mla_kn, w_o_mla, lru_conv_w, lru_conv_b, lru_gate_w, lru_gate_b, lru_lambda, w_o_lru,
           diff_qn, diff_kn, diff_lambda, diff_subln, w_o_diff, w_merge, b_merge, w_out,
           router_w, router_b, exp_w1, exp_b1, exp_w2, exp_b2):
    params = dict(
        norm1_g=norm1_g, norm2_g=norm2_g, w_in=w_in, mla_qa_norm=mla_qa_norm, w_q_b=w_q_b,
        mla_kva_norm=mla_kva_norm, w_kv_b=w_kv_b, mla_qn=mla_qn, mla_kn=mla_kn, w_o_mla=w_o_mla,
        lru_conv_w=lru_conv_w, lru_conv_b=lru_conv_b, lru_gate_w=lru_gate_w, lru_gate_b=lru_gate_b,
        lru_lambda=lru_lambda, w_o_lru=w_o_lru, diff_qn=diff_qn, diff_kn=diff_kn, diff_lambda=diff_lambda,
        diff_subln=diff_subln, w_o_diff=w_o_diff, w_merge=w_merge, b_merge=b_merge, w_out=w_out,
        router_w=router_w, router_b=router_b, exp_w1=exp_w1, exp_b1=exp_b1, exp_w2=exp_w2, exp_b2=exp_b2)
    w = _prepare(params)

    cond = jnp.concatenate(
        [c, c_ctx[None, :], jnp.zeros((COND_ROWS - DEC_BATCH - 1, D_MODEL), F32)], axis=0)
    mod = _ada_call(cond, ada_w, ada_b[:, None, :])
    mod3 = mod.reshape(DEPTH * COND_ROWS, 1, 6 * D_MODEL)

    place = jnp.pad(jnp.eye(MLA_ROPE, dtype=F32), ((0, 0), (MLA_NOPE, LANES - MLA_QK))).astype(BF16)
    k_ctx, v_ctx = _kvx_call(cache_mla_ckv, cache_mla_krope, w["wkvk"], w["wkvv"], w["kn"], place)
    dk_ctx = cache_diff_k.reshape(DEC_BATCH, DEPTH, PAST_LEN, 512)
    dv_ctx = cache_diff_v.reshape(DEC_BATCH, DEPTH, PAST_LEN, 512)

    x = jnp.concatenate([x_prompt.reshape(N_CTX_TOK, D_MODEL), x_sample.reshape(-1, D_MODEL)], axis=0)
    h0_ctx = jnp.zeros((BATCH, 2, D_RNN), F32)
    lat_blk = N_CTX_TOK // DEC_SEQ
    new_ckv, new_krope, new_lru, new_dk, new_dv = [], [], [], [], []
    moe_out = None
    for l in range(DEPTH):
        lam_init = 0.8 - 0.6 * math.exp(-0.3 * l)
        x, (qm, km, vm, qd, kd, vd, lx, lg, ckv_o, kro_o, dk_o, dv_o) = _front_call(l, x, moe_out, mod3, w)

        u_ctx, hf_ctx = _lru_call(l, lx, lg, h0_ctx, w, BATCH, SEQ, 0)
        u_lat, _ = _lru_call(l, lx, lg, state_lru[:, l], w, DEC_BATCH, DEC_SEQ, lat_blk)

        om_ctx = _attn_call(_mla_attn_kernel, "mla_ctx", qm, [(km, vm, "tok")], [], [], 0, BATCH, 1, 512)
        om_lat = _attn_call(_mla_attn_kernel, "mla_lat", qm,
                            [(k_ctx, v_ctx, lambda b, t: (l, b, 0, 0)), (km, vm, "tok")], [], [],
                            lat_blk, DEC_BATCH, LAT_TILES_PER_SEQ, 512)

        dk_extra = [w["diff_lambda"], w["diff_subln"]]
        dk_specs = lambda: [pl.BlockSpec((None, 4, DIFF_DH), lambda b, t: (l, 0, 0)),
                            pl.BlockSpec((None, 1, LANES), lambda b, t: (l, 0, 0))]
        dkern = lambda n_seg, *refs: _diff_attn_kernel(n_seg, lam_init, *refs)
        od_ctx = _attn_call(dkern, "diff_ctx", qd, [(kd, vd, "tok")], dk_extra, dk_specs(),
                            0, BATCH, 1, 512)
        od_lat = _attn_call(dkern, "diff_lat", qd,
                            [(dk_ctx, dv_ctx, lambda b, t: (b, l, 0, 0)), (kd, vd, "tok")], dk_extra, dk_specs(),
                            lat_blk, DEC_BATCH, LAT_TILES_PER_SEQ, 512)

        x, h2t, ids, sw, seg_off, n_gran = _back_call(
            l, x, mod3, (om_ctx, om_lat), (u_ctx, u_lat), (od_ctx, od_lat), w)
        moe_out = _moe_call(l, h2t, ids, sw, seg_off, n_gran, w)

        new_ckv.append(ckv_o.reshape(BATCH, SEQ, KV_LORA))
        new_krope.append(kro_o.reshape(BATCH, SEQ, MLA_ROPE))
        new_lru.append(hf_ctx)
        new_dk.append(dk_o.reshape(BATCH, SEQ, DIFF_HEADS, 2, DIFF_DH))
        new_dv.append(dv_o.reshape(BATCH, SEQ, DIFF_HEADS, DIFF_DV))

    x = _residual_call(DEPTH - 1, x, moe_out, mod3)
    xp = x[:N_CTX_TOK].reshape(BATCH, SEQ, D_MODEL)
    xs = x[N_CTX_TOK:].reshape(DEC_BATCH, DEC_SEQ, D_MODEL)
    return (xp, xs, jnp.stack(new_ckv, axis=1), jnp.stack(new_krope, axis=1), jnp.stack(new_lru, axis=1),
            jnp.stack(new_dk, axis=1), jnp.stack(new_dv, axis=1))
```

```python
import functools
import math

import jax
import jax.numpy as jnp
from jax import lax
from jax.experimental import pallas as pl
from jax.experimental.pallas import tpu as pltpu

F32 = jnp.float32
BF16 = jnp.bfloat16

D_MODEL = 1024
BATCH = 16
SEQ = 256
DEPTH = 4
DEC_BATCH = 8
DEC_SEQ = 1024
PAST_LEN = 256
GRID_W = 64
ROPE_BASE = 10000.0
NORM_EPS = 1e-6

MLA_HEADS = 8
MLA_NOPE = 64
MLA_ROPE = 32
MLA_QK = MLA_NOPE + MLA_ROPE
MLA_V = 64
Q_LORA = 384
KV_LORA = 256
D_RNN = 512
LRU_BLOCKS = 8
LRU_BW = D_RNN // LRU_BLOCKS
CONV_W = 4
LRU_C = 8.0
DIFF_HEADS = 4
DIFF_DH = 64
DIFF_DV = 2 * DIFF_DH
N_EXPERTS = 32
TOP_K = 4
D_EXPERT = 512
SWIGLU_LIMIT = 7.0
SWIGLU_ALPHA = 1.702

OFF_QA = 0
OFF_KVA = OFF_QA + Q_LORA
OFF_LRU_X = OFF_KVA + KV_LORA + MLA_ROPE
OFF_LRU_G = OFF_LRU_X + D_RNN
OFF_DQ = OFF_LRU_G + D_RNN
OFF_DK = OFF_DQ + DIFF_HEADS * 2 * DIFF_DH
OFF_DV = OFF_DK + DIFF_HEADS * 2 * DIFF_DH
IN_COLS = OFF_DV + DIFF_HEADS * DIFF_DV

LANES = 128
TILE = 256
N_CTX_TILES = BATCH * SEQ // TILE
LAT_TILES_PER_SEQ = DEC_SEQ // TILE
N_LAT_TILES = DEC_BATCH * LAT_TILES_PER_SEQ
N_TILES = N_CTX_TILES + N_LAT_TILES
N_CTX_TOK = BATCH * SEQ
N_TOK = N_TILES * TILE
COND_ROWS = 16
CTX_COND_ROW = DEC_BATCH

P_QA = 0
P_CKV = P_QA + Q_LORA
P_KR = P_CKV + KV_LORA
P_LX = P_KR + LANES
P_LG = P_LX + D_RNN
P_DQ = P_LG + D_RNN
P_DK = P_DQ + 512
P_DV = P_DK + 512
P_COLS = P_DV + 512

SUB = 8
MOE_TILES = 16
MOE_TOK = MOE_TILES * TILE
N_MOE_BLOCKS = N_TOK // MOE_TOK
GRAN = 8
SLOTS = TILE * TOP_K + N_EXPERTS * GRAN
MOE_CHUNK = 256
CHUNK_GRAN = MOE_CHUNK // GRAN
CHUNK_SHIFT = CHUNK_GRAN.bit_length() - 1
MAX_GRAN = MOE_TILES * SLOTS // GRAN
VMEM_LIMIT = 56 * 1024 * 1024


def _cond_row(i):
    return jnp.where(i < N_CTX_TILES, CTX_COND_ROW, (i - N_CTX_TILES) // LAT_TILES_PER_SEQ)


def _rope_idx(i):
    return jnp.where(i < N_CTX_TILES, 0, 1 + (i - N_CTX_TILES) % LAT_TILES_PER_SEQ)


def _ctx_out_idx(i):
    return jnp.minimum(i, N_CTX_TILES - 1)


def _const_spec(shape, index):
    return pl.BlockSpec(shape, lambda *_: index, pipeline_mode=pl.Buffered(1))


def _dot(a, b):
    return jnp.dot(a, b, preferred_element_type=F32)


def _dot_nt(a, b):
    return lax.dot_general(a, b, (((1,), (1,)), ((), ())), preferred_element_type=F32)


def _rms(x, n=None):
    n = x.shape[-1] if n is None else n
    ss = jnp.sum(x * x, axis=-1, keepdims=True)
    return x * lax.rsqrt(ss * (1.0 / n) + NORM_EPS)


def _rope(x, cos, sin_signed, swap):
    return x * cos + _dot(x.astype(BF16), swap) * sin_signed


def _sigmoid(x):
    return 0.5 * jnp.tanh(0.5 * x) + 0.5


def _lane_iota(shape):
    return lax.broadcasted_iota(jnp.int32, shape, len(shape) - 1)


def _ada_kernel(cond_ref, w_ref, b_ref, o_ref):
    c = cond_ref[...]
    s = c * jax.nn.sigmoid(c)
    o_ref[...] = _dot(s.astype(BF16), w_ref[...].astype(BF16)) + b_ref[...]


def _ada_call(cond, ada_w, ada_b3):
    cb = 1024
    return pl.pallas_call(
        _ada_kernel,
        grid=(DEPTH, 6 * D_MODEL // cb),
        in_specs=[
            pl.BlockSpec((COND_ROWS, D_MODEL), lambda l, j: (0, 0)),
            pl.BlockSpec((None, D_MODEL, cb), lambda l, j: (l, 0, j)),
            pl.BlockSpec((None, 1, cb), lambda l, j: (l, 0, j)),
        ],
        out_specs=pl.BlockSpec((None, COND_ROWS, cb), lambda l, j: (l, 0, j)),
        out_shape=jax.ShapeDtypeStruct((DEPTH, COND_ROWS, 6 * D_MODEL), F32),
        compiler_params=pltpu.CompilerParams(
            dimension_semantics=("arbitrary", "arbitrary"), vmem_limit_bytes=VMEM_LIMIT),
        name="ada_mod",
    )(cond, ada_w, ada_b3)


def _mla_k_heads(kn, krs, kn_gain, cos, sin, swap, store):
    for hd in range(MLA_HEADS):
        kh = kn[:, hd * LANES:(hd + 1) * LANES] + krs
        kh = _rms(kh, MLA_QK) * kn_gain
        if cos is not None:
            kh = _rope(kh, cos, sin, swap)
        store(hd, kh)


def _kvx_kernel(ckv_ref, kr_ref, wk_ref, wv_ref, kn_ref, place_ref, k_ref, v_ref):
    c = ckv_ref[...].astype(BF16)
    kn = _dot(c, wk_ref[...])
    v_ref[...] = _dot(c, wv_ref[...]).astype(BF16)
    krs = _dot(kr_ref[...].astype(BF16), place_ref[...])

    def store(hd, kh):
        k_ref[:, hd * LANES:(hd + 1) * LANES] = kh.astype(BF16)

    _mla_k_heads(kn, krs, kn_ref[...], None, None, None, store)


def _kvx_call(cache_ckv, cache_krope, wkvk, wkvv, kn_pad, place):
    return pl.pallas_call(
        _kvx_kernel,
        grid=(DEPTH, DEC_BATCH),
        in_specs=[
            pl.BlockSpec((None, None, PAST_LEN, KV_LORA), lambda l, b: (b, l, 0, 0)),
            pl.BlockSpec((None, None, PAST_LEN, MLA_ROPE), lambda l, b: (b, l, 0, 0)),
            pl.BlockSpec((None, KV_LORA, MLA_HEADS * LANES), lambda l, b: (l, 0, 0)),
            pl.BlockSpec((None, KV_LORA, MLA_HEADS * MLA_V), lambda l, b: (l, 0, 0)),
            pl.BlockSpec((None, 1, LANES), lambda l, b: (l, 0, 0)),
            pl.BlockSpec((MLA_ROPE, LANES), lambda l, b: (0, 0)),
        ],
        out_specs=[
            pl.BlockSpec((None, None, PAST_LEN, MLA_HEADS * LANES), lambda l, b: (l, b, 0, 0)),
            pl.BlockSpec((None, None, PAST_LEN, MLA_HEADS * MLA_V), lambda l, b: (l, b, 0, 0)),
        ],
        out_shape=[
            jax.ShapeDtypeStruct((DEPTH, DEC_BATCH, PAST_LEN, MLA_HEADS * LANES), BF16),
            jax.ShapeDtypeStruct((DEPTH, DEC_BATCH, PAST_LEN, MLA_HEADS * MLA_V), BF16),
        ],
        compiler_params=pltpu.CompilerParams(
            dimension_semantics=("arbitrary", "arbitrary"), vmem_limit_bytes=VMEM_LIMIT),
        name="ctx_kv_expand",
    )(cache_ckv, cache_krope, wkvk, wkvv, kn_pad, place)


def _modulated(x, gain, shift, scale):
    return _rms(x) * gain * (1.0 + scale) + shift


def _untile(t_ref):
    rows = t_ref.shape[0] // SUB
    return jnp.concatenate([t_ref[pl.ds(s, rows, stride=SUB), :] for s in range(SUB)], axis=-1)


def _moe_residual(x1_ref, yt_ref, modp_ref):
    return x1_ref[...] + modp_ref[:, 5 * D_MODEL:6 * D_MODEL] * _untile(yt_ref)


def _front_kernel(has_moe, *refs):
    if has_moe:
        x1_ref, yt_ref, modp_ref = refs[:3]
        refs = refs[3:]
    else:
        x_ref = refs[0]
        refs = refs[1:]
    (mod_ref, n1_ref, win_ref, qan_ref, wqb_ref, kvan_ref, wkvk_ref, wkvv_ref,
     qn_ref, kn_ref, dqn_ref, dkn_ref, cosm_ref, sinm_ref, cosd_ref, sind_ref, swapm_ref, swapd_ref) = refs[:18]
    refs = refs[18:]
    if has_moe:
        xo_ref = refs[0]
        refs = refs[1:]
    (qm_ref, km_ref, vm_ref, qd_ref, kd_ref, vd_ref, lx_ref, lg_ref,
     ckv_ref, kro_ref, dko_ref, dvo_ref) = refs
    if has_moe:
        x = _moe_residual(x1_ref, yt_ref, modp_ref)
        xo_ref[...] = x
    else:
        x = x_ref[...]
    mod = mod_ref[...]
    h = _modulated(x, n1_ref[...], mod[:, 0:D_MODEL], mod[:, D_MODEL:2 * D_MODEL])
    hb = h.astype(BF16)

    lane = _lane_iota((1, LANES))
    cosm, sinm, swap_m = cosm_ref[...], sinm_ref[...], swapm_ref[...]
    cosd, sind, swap_d = cosd_ref[...], sind_ref[...], swapd_ref[...]

    qa = _rms(_dot(hb, win_ref[:, P_QA:P_QA + Q_LORA])) * qan_ref[...]
    q = _dot(qa.astype(BF16), wqb_ref[...])
    q_scale = MLA_QK ** -0.5
    for hd in range(MLA_HEADS):
        qh = _rms(q[:, hd * LANES:(hd + 1) * LANES], MLA_QK) * qn_ref[...]
        qh = _rope(qh, cosm, sinm, swap_m)
        qm_ref[:, hd * LANES:(hd + 1) * LANES] = (qh * q_scale).astype(BF16)

    ckv = _rms(_dot(hb, win_ref[:, P_CKV:P_CKV + KV_LORA])) * kvan_ref[...]
    krs = _dot(hb, win_ref[:, P_KR:P_KR + LANES])
    cb = ckv.astype(BF16)
    vm_ref[...] = _dot(cb, wkvv_ref[...]).astype(BF16)
    kn = _dot(cb, wkvk_ref[...])

    def store_k(hd, kh):
        km_ref[:, hd * LANES:(hd + 1) * LANES] = kh.astype(BF16)

    _mla_k_heads(kn, krs, kn_ref[...], cosm, sinm, swap_m, store_k)

    lx_ref[...] = _dot(hb, win_ref[:, P_LX:P_LX + D_RNN])
    lg_ref[...] = _dot(hb, win_ref[:, P_LG:P_LG + D_RNN])

    low = lane < DIFF_DH

    def pair_norm(t):
        sq = t * t
        s_all = jnp.sum(sq, axis=-1, keepdims=True)
        s_lo = jnp.sum(jnp.where(low, sq, 0.0), axis=-1, keepdims=True)
        r_lo = lax.rsqrt(s_lo * (1.0 / DIFF_DH) + NORM_EPS)
        r_hi = lax.rsqrt((s_all - s_lo) * (1.0 / DIFF_DH) + NORM_EPS)
        return t * jnp.where(low, r_lo, r_hi)

    dq = _dot(hb, win_ref[:, P_DQ:P_DQ + 512])
    dk = _dot(hb, win_ref[:, P_DK:P_DK + 512])
    d_scale = DIFF_DH ** -0.5
    k_heads = []
    for hd in range(DIFF_HEADS):
        sl = slice(hd * LANES, (hd + 1) * LANES)
        qh = _rope(pair_norm(dq[:, sl]) * dqn_ref[...], cosd, sind, swap_d)
        qd_ref[:, sl] = (qh * d_scale).astype(BF16)
        kh = _rope(pair_norm(dk[:, sl]) * dkn_ref[...], cosd, sind, swap_d)
        kd_ref[:, sl] = kh.astype(BF16)
        k_heads.append(kh)
    dv = _dot(hb, win_ref[:, P_DV:P_DV + 512])
    vd_ref[...] = dv.astype(BF16)

    @pl.when(pl.program_id(0) < N_CTX_TILES)
    def _():
        ckv_ref[...] = ckv
        kro_ref[...] = krs[:, MLA_NOPE:MLA_NOPE + MLA_ROPE]
        for hd in range(DIFF_HEADS):
            dko_ref[:, hd * LANES:(hd + 1) * LANES] = k_heads[hd]
        dvo_ref[...] = dv


def _mod_spec(l):
    return pl.BlockSpec((None, 1, 6 * D_MODEL), lambda i: (l * COND_ROWS + _cond_row(i), 0, 0))


def _tile_spec():
    return pl.BlockSpec((TILE * SUB, LANES), lambda i: (i, 0))


def _front_call(l, x, moe_out, mod3, w):
    tok = lambda width: pl.BlockSpec((TILE, width), lambda i: (i, 0))
    ctx = lambda width: pl.BlockSpec((TILE, width), lambda i: (_ctx_out_idx(i), 0))
    lay = lambda *shape: _const_spec((None,) + shape, (l,) + (0,) * len(shape))
    rope = pl.BlockSpec((None, TILE, LANES), lambda i: (_rope_idx(i), 0, 0))
    swap = pl.BlockSpec((LANES, LANES), lambda i: (0, 0))
    n_ctx_rows = N_CTX_TOK
    has_moe = moe_out is not None
    if has_moe:
        lead_specs = [tok(D_MODEL), _tile_spec(), _mod_spec(l - 1)]
        lead_args = [x, moe_out, mod3]
        x_out_specs = [tok(D_MODEL)]
        x_out_shape = [jax.ShapeDtypeStruct((N_TOK, D_MODEL), F32)]
    else:
        lead_specs, lead_args, x_out_specs, x_out_shape = [tok(D_MODEL)], [x], [], []
    outs = pl.pallas_call(
        functools.partial(_front_kernel, has_moe),
        grid=(N_TILES,),
        in_specs=lead_specs + [
            _mod_spec(l),
            lay(1, D_MODEL), lay(D_MODEL, P_COLS), lay(1, Q_LORA), lay(Q_LORA, MLA_HEADS * LANES),
            lay(1, KV_LORA), lay(KV_LORA, MLA_HEADS * LANES), lay(KV_LORA, MLA_HEADS * MLA_V),
            lay(1, LANES), lay(1, LANES), lay(1, LANES), lay(1, LANES),
            rope, rope, rope, rope, swap, swap,
        ],
        out_specs=x_out_specs + [
            tok(1024), tok(1024), tok(512), tok(512), tok(512), tok(512), tok(512), tok(512),
            ctx(KV_LORA), ctx(MLA_ROPE), ctx(512), ctx(512),
        ],
        out_shape=x_out_shape + [
            jax.ShapeDtypeStruct((N_TOK, 1024), BF16), jax.ShapeDtypeStruct((N_TOK, 1024), BF16),
            jax.ShapeDtypeStruct((N_TOK, 512), BF16), jax.ShapeDtypeStruct((N_TOK, 512), BF16),
            jax.ShapeDtypeStruct((N_TOK, 512), BF16), jax.ShapeDtypeStruct((N_TOK, 512), BF16),
            jax.ShapeDtypeStruct((N_TOK, 512), F32), jax.ShapeDtypeStruct((N_TOK, 512), F32),
            jax.ShapeDtypeStruct((n_ctx_rows, KV_LORA), F32),
            jax.ShapeDtypeStruct((n_ctx_rows, MLA_ROPE), F32),
            jax.ShapeDtypeStruct((n_ctx_rows, 512), F32),
            jax.ShapeDtypeStruct((n_ctx_rows, 512), F32),
        ],
        compiler_params=pltpu.CompilerParams(
            dimension_semantics=("arbitrary",), vmem_limit_bytes=VMEM_LIMIT),
        name="front",
    )(*lead_args, mod3, w["norm1"], w["w_in"], w["qa_norm"], w["wqb"], w["kva_norm"], w["wkvk"], w["wkvv"],
      w["qn"], w["kn"], w["dqn"], w["dkn"], w["cos_m"], w["sin_m"], w["cos_d"], w["sin_d"],
      w["swap_m"], w["swap_d"])
    if has_moe:
        return outs[0], outs[1:]
    return x, outs


def _residual_kernel(x1_ref, yt_ref, modp_ref, oc_ref, ol_ref):
    x = _moe_residual(x1_ref, yt_ref, modp_ref)
    is_ctx = pl.program_id(0) < N_CTX_TILES

    @pl.when(is_ctx)
    def _():
        oc_ref[...] = x

    @pl.when(jnp.logical_not(is_ctx))
    def _():
        ol_ref[...] = x


def _residual_call(l, x1, moe_out, mod3):
    tok = pl.BlockSpec((TILE, D_MODEL), lambda i: (i, 0))
    return pl.pallas_call(
        _residual_kernel,
        grid=(N_TILES,),
        in_specs=[tok, _tile_spec(), _mod_spec(l)],
        out_specs=[
            pl.BlockSpec((TILE, D_MODEL), lambda i: (jnp.minimum(i, N_CTX_TILES - 1), 0)),
            pl.BlockSpec((TILE, D_MODEL), lambda i: (jnp.maximum(i - N_CTX_TILES, 0), 0)),
        ],
        out_shape=[
            jax.ShapeDtypeStruct((N_CTX_TOK, D_MODEL), F32),
            jax.ShapeDtypeStruct((N_TOK - N_CTX_TOK, D_MODEL), F32),
        ],
        compiler_params=pltpu.CompilerParams(
            dimension_semantics=("arbitrary",), vmem_limit_bytes=VMEM_LIMIT),
        name="moe_residual",
    )(x1, moe_out, mod3)


def _scan(a, b, h0, reverse):
    n = a.shape[0]
    row = lax.broadcasted_iota(jnp.int32, (n, 1), 0)
    d = 1
    while d < n:
        if reverse:
            valid = row < n - d
            shift = n - d
        else:
            valid = row >= d
            shift = d
        a_s = jnp.where(valid, pltpu.roll(a, shift, 0), 1.0)
        b_s = jnp.where(valid, pltpu.roll(b, shift, 0), 0.0)
        b = a * b_s + b
        a = a * a_s
        d *= 2
    return a * h0 + b


def _gelu_tanh(x):
    return 0.5 * x * (1.0 + jnp.tanh(math.sqrt(2.0 / math.pi) * (x + 0.044715 * x * x * x)))


N_LANE_TILES = D_RNN // LANES


def _group_scan(a, b, h0, reverse, sa_ref, sb_ref, sc_ref):
    n = a.shape[0]
    g = n // SUB
    a3 = a.reshape(g, SUB, D_RNN)
    b3 = b.reshape(g, SUB, D_RNN)
    r8 = lax.broadcasted_iota(jnp.int32, (1, SUB, 1), 1)
    d = 1
    while d < SUB:
        if reverse:
            valid, shift = r8 < SUB - d, SUB - d
        else:
            valid, shift = r8 >= d, d
        a_s = jnp.where(valid, pltpu.roll(a3, shift, 1), 1.0)
        b_s = jnp.where(valid, pltpu.roll(b3, shift, 1), 0.0)
        b3 = a3 * b_s + b3
        a3 = a3 * a_s
        d *= 2
    a2 = a3.reshape(n, D_RNN)
    b2 = b3.reshape(n, D_RNN)
    for c in range(N_LANE_TILES):
        sa_ref[pl.ds(c * n, n), :] = a2[:, c * LANES:(c + 1) * LANES]
        sb_ref[pl.ds(c * n, n), :] = b2[:, c * LANES:(c + 1) * LANES]
    last = 0 if reverse else SUB - 1
    ta = jnp.concatenate([sa_ref[pl.ds(c * n + last, g, stride=SUB), :] for c in range(N_LANE_TILES)], axis=-1)
    tb = jnp.concatenate([sb_ref[pl.ds(c * n + last, g, stride=SUB), :] for c in range(N_LANE_TILES)], axis=-1)
    hg = _scan(ta, tb, h0, reverse)
    row = lax.broadcasted_iota(jnp.int32, (g, 1), 0)
    if reverse:
        carry = jnp.where(row == g - 1, h0, pltpu.roll(hg, g - 1, 0))
    else:
        carry = jnp.where(row == 0, h0, pltpu.roll(hg, 1, 0))
    for c in range(N_LANE_TILES):
        sc_ref[pl.ds(c * g, g), :] = carry[:, c * LANES:(c + 1) * LANES]
    return hg


def _lru_kernel(lx_ref, lg_ref, cw_ref, cb_ref, wbd_ref, gb_ref, lam_ref, h0_ref, u_ref, hf_ref,
                sa_ref, sb_ref, sc_ref, tot_ref):
    x = lx_ref[...]
    n = x.shape[0]
    row = lax.broadcasted_iota(jnp.int32, (n, 1), 0)
    cw = cw_ref[...]
    xr = cb_ref[...] + cw[2:3] * x
    xr = xr + cw[0:1] * jnp.where(row >= 2, pltpu.roll(x, 2, 0), 0.0)
    xr = xr + cw[1:2] * jnp.where(row >= 1, pltpu.roll(x, 1, 0), 0.0)
    xr = xr + cw[3:4] * jnp.where(row < n - 1, pltpu.roll(x, n - 1, 0), 0.0)
    xb = xr.astype(BF16)
    half = D_RNN // 2
    lam = lam_ref[...]
    h0 = h0_ref[...]
    gb = gb_ref[...]
    total = None
    for d in range(2):
        pre = []
        for g in range(2):
            k = d * 2 + g
            p = jnp.concatenate([_dot(xb[:, :half], wbd_ref[k, 0]), _dot(xb[:, half:], wbd_ref[k, 1])],
                                axis=-1)
            pre.append(p + gb[k:k + 1])
        r = _sigmoid(pre[0])
        i = _sigmoid(pre[1])
        z = -lam[d:d + 1]
        softplus = jnp.maximum(z, 0.0) + jnp.log(1.0 + jnp.exp(-jnp.abs(z)))
        a = jnp.exp(-LRU_C * r * softplus)
        bx = jnp.sqrt(1.0 - a * a) * (i * xr)
        g = n // SUB
        hg = _group_scan(a, bx, h0[d:d + 1], d == 1, sa_ref, sb_ref, sc_ref)
        hf_ref[d:d + 1, :] = hg[g - 1:g, :] if d == 0 else hg[0:1, :]
        for c in range(N_LANE_TILES):
            cols = slice(c * LANES, (c + 1) * LANES)
            for j in range(g):
                rows = pl.ds(c * n + j * SUB, SUB)
                h = sa_ref[rows, :] * sc_ref[pl.ds(c * g + j, 1), :] + sb_ref[rows, :]
                if d == 0:
                    tot_ref[pl.ds(j * SUB, SUB), cols] = h
                else:
                    tot_ref[pl.ds(j * SUB, SUB), cols] += h
    u_ref[...] = (tot_ref[...] * _gelu_tanh(lg_ref[...])).astype(BF16)


def _lru_call(l, lx, lg, h0, w, n_seq, seq_len, row_block0):
    seq = lambda: pl.BlockSpec((seq_len, D_RNN), lambda s: (row_block0 + s, 0))
    lay = lambda *shape: _const_spec((None,) + shape, (l,) + (0,) * len(shape))
    return pl.pallas_call(
        _lru_kernel,
        grid=(n_seq,),
        in_specs=[
            seq(), seq(), lay(CONV_W, D_RNN), lay(1, D_RNN), lay(4, 2, D_RNN // 2, D_RNN // 2),
            lay(4, D_RNN), lay(2, D_RNN),
            pl.BlockSpec((None, 2, D_RNN), lambda s: (s, 0, 0)),
        ],
        out_specs=[
            pl.BlockSpec((seq_len, D_RNN), lambda s: (s, 0)),
            pl.BlockSpec((None, 2, D_RNN), lambda s: (s, 0, 0)),
        ],
        out_shape=[
            jax.ShapeDtypeStruct((n_seq * seq_len, D_RNN), BF16),
            jax.ShapeDtypeStruct((n_seq, 2, D_RNN), F32),
        ],
        scratch_shapes=[
            pltpu.VMEM((N_LANE_TILES * seq_len, LANES), F32),
            pltpu.VMEM((N_LANE_TILES * seq_len, LANES), F32),
            pltpu.VMEM((N_LANE_TILES * seq_len // SUB, LANES), F32),
            pltpu.VMEM((seq_len, D_RNN), F32),
        ],
        compiler_params=pltpu.CompilerParams(
            dimension_semantics=("arbitrary",), vmem_limit_bytes=VMEM_LIMIT),
        name="rglru_%d" % seq_len,
    )(lx, lg, w["conv_w"], w["conv_b"], w["wbd"], w["gate_b"], w["lam"], h0)


def _softmax_parts(scores):
    m = scores[0].max(axis=-1, keepdims=True)
    for s in scores[1:]:
        m = jnp.maximum(m, s.max(axis=-1, keepdims=True))
    es = [jnp.exp(s - m) for s in scores]
    tot = es[0].sum(axis=-1, keepdims=True)
    for e in es[1:]:
        tot = tot + e.sum(axis=-1, keepdims=True)
    return es, tot


def _mla_attn_kernel(n_seg, q_ref, *refs):
    k_refs = refs[0:2 * n_seg:2]
    v_refs = refs[1:2 * n_seg:2]
    o_ref = refs[2 * n_seg]
    low = _lane_iota((1, LANES)) < MLA_V
    for pair in range(MLA_HEADS // 2):
        vs = slice(pair * LANES, (pair + 1) * LANES)
        out = None
        for hh in range(2):
            hs = slice((2 * pair + hh) * LANES, (2 * pair + hh + 1) * LANES)
            qh = q_ref[:, hs]
            es, tot = _softmax_parts([_dot_nt(qh, k[:, hs]) for k in k_refs])
            pv = None
            for e, v in zip(es, v_refs):
                t = _dot(e.astype(BF16), v[:, vs])
                pv = t if pv is None else pv + t
            oh = pv / tot
            out = oh if hh == 0 else jnp.where(low, out, oh)
        o_ref[:, vs] = out.astype(BF16)


def _diff_attn_kernel(n_seg, lam_init, q_ref, *refs):
    k_refs = refs[0:2 * n_seg:2]
    v_refs = refs[1:2 * n_seg:2]
    lam_ref, sub_ref, o_ref = refs[2 * n_seg:2 * n_seg + 3]
    lp = lam_ref[...]
    lam = (jnp.exp(jnp.sum(lp[0:1] * lp[1:2], axis=-1, keepdims=True))
           - jnp.exp(jnp.sum(lp[2:3] * lp[3:4], axis=-1, keepdims=True)) + lam_init)
    low = _lane_iota((1, LANES)) < DIFF_DH
    zero = jnp.zeros((), BF16)
    for hd in range(DIFF_HEADS):
        hs = slice(hd * LANES, (hd + 1) * LANES)
        qh = q_ref[:, hs]
        ks = [k[:, hs].astype(BF16) for k in k_refs]
        e0, t0 = _softmax_parts([_dot_nt(jnp.where(low, qh, zero), k) for k in ks])
        e1, t1 = _softmax_parts([_dot_nt(jnp.where(low, zero, qh), k) for k in ks])
        w0 = 1.0 / t0
        w1 = lam / t1
        o = None
        for a, b, v in zip(e0, e1, v_refs):
            t = _dot((a * w0 - b * w1).astype(BF16), v[:, hs].astype(BF16))
            o = t if o is None else o + t
        o = _rms(o) * sub_ref[...] * (1.0 - lam_init)
        o_ref[:, hs] = o.astype(BF16)


def _attn_call(kernel, name, q, segs, extra, extra_specs, q_block0, n_seq, tiles_per_seq, out_width):
    grid = (n_seq, tiles_per_seq)
    seq_len = tiles_per_seq * TILE
    in_specs = [pl.BlockSpec((TILE, q.shape[1]), lambda b, t: (q_block0 * tiles_per_seq + b * tiles_per_seq + t, 0))]
    args = [q]
    for k, v, kind in segs:
        for arr in (k, v):
            if kind == "tok":
                in_specs.append(pl.BlockSpec((seq_len, arr.shape[1]), lambda b, t: (q_block0 + b, 0)))
            else:
                in_specs.append(pl.BlockSpec((None, None) + arr.shape[2:], kind))
            args.append(arr)
    in_specs += extra_specs
    args += extra
    return pl.pallas_call(
        functools.partial(kernel, len(segs)),
        grid=grid,
        in_specs=in_specs,
        out_specs=pl.BlockSpec((TILE, out_width), lambda b, t: (b * tiles_per_seq + t, 0)),
        out_shape=jax.ShapeDtypeStruct((n_seq * seq_len, out_width), BF16),
        compiler_params=pltpu.CompilerParams(
            dimension_semantics=("arbitrary", "arbitrary"), vmem_limit_bytes=VMEM_LIMIT),
        name=name,
    )(*args)


def _back_kernel(x_ref, mod_ref, n1_ref, n2_ref, omc_ref, oml_ref, ulc_ref, ull_ref, odc_ref, odl_ref,
                 wom_ref, wol_ref, wod_ref, wmg_ref, bmg_ref, wout_ref, rw_ref, rb_ref,
                 x1_ref, h2t_ref, ids_ref, sw_ref, off_ref, ng_ref):
    i = pl.program_id(0)
    is_ctx = i < N_CTX_TILES
    x = x_ref[...]
    mod = mod_ref[...]
    seg = lambda j: mod[:, j * D_MODEL:(j + 1) * D_MODEL]
    hb = _modulated(x, n1_ref[...], seg(0), seg(1)).astype(BF16)
    pick = lambda c_ref, l_ref: jnp.where(is_ctx, c_ref[...], l_ref[...])
    branches = (_dot(pick(omc_ref, oml_ref), wom_ref[...]), _dot(pick(ulc_ref, ull_ref), wol_ref[...]),
                _dot(pick(odc_ref, odl_ref), wod_ref[...]))
    merged = None
    for j, o in enumerate(branches):
        cs = slice(j * D_MODEL, (j + 1) * D_MODEL)
        g = _sigmoid(_dot(hb, wmg_ref[:, cs]) + bmg_ref[:, cs])
        merged = g * o if merged is None else merged + g * o
    x1 = x + seg(2) * _dot(merged.astype(BF16), wout_ref[...])
    x1_ref[...] = x1
    h2 = _modulated(x1, n2_ref[...], seg(3), seg(4))
    for s in range(SUB):
        h2t_ref[pl.ds(s, TILE, stride=SUB), :] = h2[:, s * LANES:(s + 1) * LANES]

    h_hi = h2.astype(BF16)
    h_lo = (h2 - h_hi.astype(F32)).astype(BF16)
    rw = rw_ref[...]
    w_hi = rw.astype(BF16)
    w_lo = (rw - w_hi.astype(F32)).astype(BF16)
    logits = _dot(h_hi, w_hi) + (_dot(h_lo, w_hi) + _dot(h_hi, w_lo)) + rb_ref[...]
    lane = _lane_iota(logits.shape).astype(F32)
    neg = jnp.float32(-jnp.inf)
    work = jnp.where(lane < N_EXPERTS, logits, neg)
    sels, probs = [], []
    top = None
    for _ in range(TOP_K):
        m = work.max(axis=-1, keepdims=True)
        idx = jnp.min(jnp.where(work == m, lane, float(LANES)), axis=-1, keepdims=True)
        sel = lane == idx
        top = m if top is None else top
        sels.append(sel)
        probs.append(jnp.exp(m - top))
        work = jnp.where(sel, neg, work)
    denom = probs[0] + probs[1] + probs[2] + probs[3]

    onehot = jnp.zeros_like(logits)
    for sel in sels:
        onehot = onehot + jnp.where(sel, 1.0, 0.0)
    r_i = lax.broadcasted_iota(jnp.int32, (TILE, TILE), 0)
    c_i = lax.broadcasted_iota(jnp.int32, (TILE, TILE), 1)
    earlier = jnp.where(c_i < r_i, 1.0, 0.0).astype(BF16)
    rank = _dot(earlier, onehot.astype(BF16))
    count = jnp.sum(onehot, axis=0, keepdims=True)
    n_gran = jnp.floor((count + (GRAN - 1.0)) * (1.0 / GRAN))
    r_l = lax.broadcasted_iota(jnp.int32, (LANES, LANES), 0)
    c_l = lax.broadcasted_iota(jnp.int32, (LANES, LANES), 1)
    before = jnp.where(r_l < c_l, 1.0, 0.0).astype(BF16)
    seg_off = _dot(jnp.broadcast_to(n_gran * GRAN, (SUB, LANES)).astype(BF16), before)[0:1]
    slot_base = seg_off + rank
    slot_lane = lax.broadcasted_iota(jnp.int32, (1, SLOTS), 1).astype(F32)
    tok_col = (lax.broadcasted_iota(jnp.int32, (TILE, 1), 0) + (i % MOE_TILES) * TILE).astype(F32)
    id_acc = jnp.zeros((TILE, SLOTS), F32)
    w_acc = jnp.zeros((TILE, SLOTS), F32)
    for sel, p in zip(sels, probs):
        slot = jnp.sum(jnp.where(sel, slot_base, 0.0), axis=-1, keepdims=True)
        hit = slot_lane == slot
        id_acc = jnp.where(hit, tok_col, id_acc)
        w_acc = jnp.where(hit, p / denom, w_acc)
    ids_ref[...] = jnp.sum(id_acc, axis=0, keepdims=True).astype(jnp.int32)
    sw_ref[...] = jnp.sum(w_acc, axis=0, keepdims=True)
    off_ref[...] = seg_off.astype(jnp.int32)
    ng_ref[...] = n_gran.astype(jnp.int32)


def _back_call(l, x, mod3, om, ul, od, w):
    tok = lambda width: pl.BlockSpec((TILE, width), lambda i: (i, 0))
    ctx_in = pl.BlockSpec((TILE, 512), lambda i: (jnp.minimum(i, N_CTX_TILES - 1), 0))
    lat_in = pl.BlockSpec((TILE, 512), lambda i: (jnp.maximum(i - N_CTX_TILES, 0), 0))
    lay = lambda *shape: _const_spec((None,) + shape, (l,) + (0,) * len(shape))
    row = lambda width: pl.BlockSpec((None, 1, width), lambda i: (i, 0, 0))
    return pl.pallas_call(
        _back_kernel,
        grid=(N_TILES,),
        in_specs=[
            tok(D_MODEL), _mod_spec(l),
            lay(1, D_MODEL), lay(1, D_MODEL), ctx_in, lat_in, ctx_in, lat_in, ctx_in, lat_in,
            lay(512, D_MODEL), lay(512, D_MODEL), lay(512, D_MODEL),
            lay(D_MODEL, 3 * D_MODEL), lay(1, 3 * D_MODEL), lay(D_MODEL, D_MODEL),
            lay(D_MODEL, LANES), lay(1, LANES),
        ],
        out_specs=[tok(D_MODEL), _tile_spec(), row(SLOTS), row(SLOTS), row(LANES), row(LANES)],
        out_shape=[
            jax.ShapeDtypeStruct((N_TOK, D_MODEL), F32),
            jax.ShapeDtypeStruct((N_TOK * SUB, LANES), F32),
            jax.ShapeDtypeStruct((N_TILES, 1, SLOTS), jnp.int32),
            jax.ShapeDtypeStruct((N_TILES, 1, SLOTS), F32),
            jax.ShapeDtypeStruct((N_TILES, 1, LANES), jnp.int32),
            jax.ShapeDtypeStruct((N_TILES, 1, LANES), jnp.int32),
        ],
        compiler_params=pltpu.CompilerParams(
            dimension_semantics=("arbitrary",), vmem_limit_bytes=VMEM_LIMIT),
        name="back",
    )(x, mod3, w["norm1"], w["norm2"], om[0], om[1], ul[0], ul[1], od[0], od[1],
      w["w_o_mla"], w["w_o_lru"], w["w_o_diff"],
      w["w_merge"], w["b_merge"], w["w_out"], w["router_w"], w["router_b"])


def _moe_kernel(ids_ref, sw_ref, off_ref, ng_ref, h2t_ref, w1_ref, b1_ref, w2_ref, b2_ref,
                y_ref, xga_ref, xgb_ref, za_ref, zb_ref, gb_ref):
    e = pl.program_id(1)

    @pl.when(e == 0)
    def _():
        y_ref[...] = jnp.zeros_like(y_ref)
        for ref in (xga_ref, xgb_ref, za_ref, zb_ref):
            ref[...] = jnp.zeros_like(ref)

    n_total = 0
    for t in range(MOE_TILES):
        first = t * SLOTS + off_ref[t * LANES + e]

        def add_gran(g, pos, first=first):
            gb_ref[pos] = first + g * GRAN
            return pos + 1

        n_total = lax.fori_loop(0, ng_ref[t * LANES + e], add_gran, n_total)
    for i in range(2 * CHUNK_GRAN):
        gb_ref[n_total + i] = SLOTS - GRAN
    n_chunks = lax.shift_right_logical(n_total + (CHUNK_GRAN - 1), CHUNK_SHIFT)

    def slot_rows(j):
        return pl.ds(j * SUB, SUB)

    def gather(t, xg_ref):
        for g in range(CHUNK_GRAN):
            base = gb_ref[t * CHUNK_GRAN + g]
            for u in range(GRAN):
                xg_ref[slot_rows(g * GRAN + u), :] = h2t_ref[ids_ref[base + u]]

    def expert(xg_ref, z_ref):
        xb = jnp.concatenate([xg_ref[pl.ds(s, MOE_CHUNK, stride=SUB), :] for s in range(SUB)],
                             axis=-1).astype(BF16)
        gu = _dot(xb, w1_ref[...].astype(BF16)) + b1_ref[...]
        gate = jnp.minimum(gu[:, :D_EXPERT], SWIGLU_LIMIT)
        up = jnp.clip(gu[:, D_EXPERT:], -SWIGLU_LIMIT, SWIGLU_LIMIT)
        act = (up + 1.0) * (gate * _sigmoid(SWIGLU_ALPHA * gate))
        y = _dot(act.astype(BF16), w2_ref[...].astype(BF16)) + b2_ref[...]
        for s in range(SUB):
            z_ref[pl.ds(s, MOE_CHUNK, stride=SUB), :] = y[:, s * LANES:(s + 1) * LANES]

    def combine(t, z_ref):
        for g in range(CHUNK_GRAN):
            base = gb_ref[t * CHUNK_GRAN + g]
            toks = [ids_ref[base + u] for u in range(GRAN)]
            new = [y_ref[toks[u]] + sw_ref[base + u] * z_ref[slot_rows(g * GRAN + u), :] for u in range(GRAN)]
            for u in reversed(range(GRAN)):
                y_ref[toks[u]] = new[u]

    gather(0, xga_ref)

    @pl.when(n_chunks > 0)
    def _():
        expert(xga_ref, za_ref)
        gather(1, xgb_ref)

    def tick_pair(p, carry):
        t = 2 * p + 1
        expert(xgb_ref, zb_ref)
        gather(t + 1, xga_ref)
        combine(t - 1, za_ref)

        @pl.when(t + 1 < n_chunks)
        def _():
            expert(xga_ref, za_ref)
            gather(t + 2, xgb_ref)
            combine(t, zb_ref)

        return carry

    lax.fori_loop(0, lax.shift_right_logical(n_chunks, 1), tick_pair, 0)

    @pl.when((n_chunks & 1) == 1)
    def _():
        combine(n_chunks - 1, za_ref)

    @pl.when(jnp.logical_and(n_chunks > 0, (n_chunks & 1) == 0))
    def _():
        combine(n_chunks - 1, zb_ref)


def _moe_call(l, h2t, ids, sw, seg_off, n_gran, w):
    smem = lambda n: pl.BlockSpec((MOE_TILES * n,), lambda j, e: (j,), memory_space=pltpu.SMEM,
                                  pipeline_mode=pl.Buffered(1))
    blk = pl.BlockSpec((MOE_TOK, SUB, LANES), lambda j, e: (j, 0, 0), pipeline_mode=pl.Buffered(1))
    return pl.pallas_call(
        _moe_kernel,
        grid=(N_MOE_BLOCKS, N_EXPERTS),
        in_specs=[
            smem(SLOTS), smem(SLOTS), smem(LANES), smem(LANES),
            blk,
            pl.BlockSpec((None, None, D_MODEL, 2 * D_EXPERT), lambda j, e: (l, e, 0, 0)),
            pl.BlockSpec((None, None, 1, 2 * D_EXPERT), lambda j, e: (l, e, 0, 0)),
            pl.BlockSpec((None, None, D_EXPERT, D_MODEL), lambda j, e: (l, e, 0, 0)),
            pl.BlockSpec((None, None, 1, D_MODEL), lambda j, e: (l, e, 0, 0)),
        ],
        out_specs=blk,
        out_shape=jax.ShapeDtypeStruct((N_TOK, SUB, LANES), F32),
        scratch_shapes=[pltpu.VMEM((MOE_CHUNK * SUB, LANES), F32)] * 4
        + [pltpu.SMEM((MAX_GRAN + 2 * CHUNK_GRAN,), jnp.int32)],
        compiler_params=pltpu.CompilerParams(
            dimension_semantics=("arbitrary", "arbitrary"), vmem_limit_bytes=VMEM_LIMIT),
        name="moe",
    )(ids.reshape(-1), sw.reshape(-1), seg_off.reshape(-1), n_gran.reshape(-1),
      h2t.reshape(N_TOK, SUB, LANES), w["exp_w1"], w["exp_b1"], w["exp_w2"], w["exp_b2"]
      ).reshape(N_TOK * SUB, LANES)


def _axial_tables(n_tokens, dim):
    rows = n_tokens // GRID_W
    row = jnp.repeat(jnp.arange(rows), GRID_W)
    col = jnp.tile(jnp.arange(GRID_W), rows)
    half = dim // 2
    inv = 1.0 / (ROPE_BASE ** (jnp.arange(0, half, 2, dtype=F32) / half))

    def axis_angles(pos):
        ang = pos.astype(F32)[:, None] * inv[None, :]
        return jnp.concatenate([ang, ang], axis=-1)

    ang = jnp.concatenate([axis_angles(row), axis_angles(col)], axis=-1)
    return jnp.cos(ang), jnp.sin(ang)


def _rope_slot_tables(dim, lane0, copies):
    cos, sin = _axial_tables(DEC_SEQ, dim)
    quarter = dim // 4
    sign = jnp.where((jnp.arange(dim) % (dim // 2)) < quarter, -1.0, 1.0)
    sin = sin * sign
    cos_slot = jnp.ones((DEC_SEQ, LANES), F32)
    sin_slot = jnp.zeros((DEC_SEQ, LANES), F32)
    for c in range(copies):
        cos_slot = cos_slot.at[:, lane0 + c * dim:lane0 + (c + 1) * dim].set(cos)
        sin_slot = sin_slot.at[:, lane0 + c * dim:lane0 + (c + 1) * dim].set(sin)
    ident = (jnp.ones((1, TILE, LANES), F32), jnp.zeros((1, TILE, LANES), F32))
    cos_t = jnp.concatenate([ident[0], cos_slot.reshape(LAT_TILES_PER_SEQ, TILE, LANES)], axis=0)
    sin_t = jnp.concatenate([ident[1], sin_slot.reshape(LAT_TILES_PER_SEQ, TILE, LANES)], axis=0)
    return cos_t, sin_t


def _half_swap(block):
    dst = jnp.arange(LANES)
    src = jnp.where((dst % block) < block // 2, dst + block // 2, dst - block // 2)
    return (jnp.arange(LANES)[:, None] == src[None, :]).astype(BF16)


def _pad_last(a, width):
    return jnp.pad(a, [(0, 0)] * (a.ndim - 1) + [(0, width - a.shape[-1])])


def _prepare(p):
    L = DEPTH
    w = {}
    w_in = p["w_in"]
    kr_slot = jnp.pad(w_in[:, :, OFF_KVA + KV_LORA:OFF_LRU_X], ((0, 0), (0, 0), (MLA_NOPE, LANES - MLA_QK)))
    w["w_in"] = jnp.concatenate(
        [w_in[:, :, OFF_QA:OFF_KVA + KV_LORA], kr_slot, w_in[:, :, OFF_LRU_X:IN_COLS]], axis=-1).astype(BF16)
    w["wqb"] = _pad_last(p["w_q_b"].reshape(L, Q_LORA, MLA_HEADS, MLA_QK), LANES).reshape(
        L, Q_LORA, MLA_HEADS * LANES).astype(BF16)
    kvb = p["w_kv_b"].reshape(L, KV_LORA, MLA_HEADS, MLA_NOPE + MLA_V)
    w["wkvk"] = _pad_last(kvb[..., :MLA_NOPE], LANES).reshape(L, KV_LORA, MLA_HEADS * LANES).astype(BF16)
    w["wkvv"] = kvb[..., MLA_NOPE:].reshape(L, KV_LORA, MLA_HEADS * MLA_V).astype(BF16)
    row = lambda a: a[:, None, :]
    w["norm1"] = row(p["norm1_g"])
    w["norm2"] = row(p["norm2_g"])
    w["qa_norm"] = row(p["mla_qa_norm"])
    w["kva_norm"] = row(p["mla_kva_norm"])
    w["qn"] = row(_pad_last(p["mla_qn"], LANES))
    w["kn"] = row(_pad_last(p["mla_kn"], LANES))
    w["dqn"] = row(jnp.tile(p["diff_qn"], (1, 2)))
    w["dkn"] = row(jnp.tile(p["diff_kn"], (1, 2)))
    w["cos_m"], w["sin_m"] = _rope_slot_tables(MLA_ROPE, MLA_NOPE, 1)
    w["cos_d"], w["sin_d"] = _rope_slot_tables(DIFF_DH, 0, 2)
    w["swap_m"] = _half_swap(MLA_ROPE // 2)
    w["swap_d"] = _half_swap(DIFF_DH // 2)
    w["conv_w"] = p["lru_conv_w"]
    w["conv_b"] = row(p["lru_conv_b"])
    per = (D_RNN // 2) // LRU_BW
    gw = p["lru_gate_w"].reshape(L, 4, 2, per, LRU_BW, LRU_BW)
    eye = jnp.eye(per, dtype=F32)
    w["wbd"] = jnp.einsum("lkhacd,ab->lkhacbd", gw, eye).reshape(L, 4, 2, D_RNN // 2, D_RNN // 2).astype(BF16)
    w["gate_b"] = p["lru_gate_b"].reshape(L, 4, D_RNN)
    w["lam"] = p["lru_lambda"]
    w["w_o_mla"] = p["w_o_mla"].astype(BF16)
    w["w_o_lru"] = p["w_o_lru"].astype(BF16)
    w["w_o_diff"] = p["w_o_diff"].astype(BF16)
    w["w_merge"] = p["w_merge"].astype(BF16)
    w["b_merge"] = row(p["b_merge"])
    w["w_out"] = p["w_out"].astype(BF16)
    w["router_w"] = _pad_last(p["router_w"], LANES)
    w["router_b"] = row(_pad_last(p["router_b"], LANES))
    w["exp_w1"] = p["exp_w1"]
    w["exp_b1"] = p["exp_b1"][:, :, None, :]
    w["exp_w2"] = p["exp_w2"]
    w["exp_b2"] = p["exp_b2"][:, :, None, :]
    w["diff_lambda"] = p["diff_lambda"]
    w["diff_subln"] = row(p["diff_subln"])
    return w


def kernel(x_prompt, x_sample, cache_mla_ckv, cache_mla_krope, state_lru, cache_diff_k, cache_diff_v,
           c, c_ctx, ada_w, ada_b, norm1_g, norm2_g, w_in, mla_qa_norm, w_q_b, mla_kva_norm, w_kv_b,
           mla_qn, mla_kn, w_o_mla, lru_conv_w, lru_conv_b, lru_gate_w, lru_gate_b, lru_lambda, w_o_lru,
           diff_qn, diff_kn, diff_lambda, diff_subln, w_o_diff, w_merge, b_merge, w_out,
           router_w, router_b, exp_w1, exp_b1, exp_w2, exp_b2):
    params = dict(
        norm1_g=norm1_g, norm2_g=norm2_g, w_in=w_in, mla_qa_norm=mla_qa_norm, w_q_b=w_q_b,
        mla_kva_norm=mla_kva_norm, w_kv_b=w_kv_b, mla_qn=mla_qn, mla_kn=mla_kn, w_o_mla=w_o_mla,
        lru_conv_w=lru_conv_w, lru_conv_b=lru_conv_b, lru_gate_w=lru_gate_w, lru_gate_b=lru_gate_b,
        lru_lambda=lru_lambda, w_o_lru=w_o_lru, diff_qn=diff_qn, diff_kn=diff_kn, diff_lambda=diff_lambda,
        diff_subln=diff_subln, w_o_diff=w_o_diff, w_merge=w_merge, b_merge=b_merge, w_out=w_out,
        router_w=router_w, router_b=router_b, exp_w1=exp_w1, exp_b1=exp_b1, exp_w2=exp_w2, exp_b2=exp_b2)
    w = _prepare(params)

    cond = jnp.concatenate(
        [c, c_ctx[None, :], jnp.zeros((COND_ROWS - DEC_BATCH - 1, D_MODEL), F32)], axis=0)
    mod = _ada_call(cond, ada_w, ada_b[:, None, :])
    mod3 = mod.reshape(DEPTH * COND_ROWS, 1, 6 * D_MODEL)

    place = jnp.pad(jnp.eye(MLA_ROPE, dtype=F32), ((0, 0), (MLA_NOPE, LANES - MLA_QK))).astype(BF16)
    k_ctx, v_ctx = _kvx_call(cache_mla_ckv, cache_mla_krope, w["wkvk"], w["wkvv"], w["kn"], place)
    dk_ctx = cache_diff_k.reshape(DEC_BATCH, DEPTH, PAST_LEN, 512)
    dv_ctx = cache_diff_v.reshape(DEC_BATCH, DEPTH, PAST_LEN, 512)

    x = jnp.concatenate([x_prompt.reshape(N_CTX_TOK, D_MODEL), x_sample.reshape(-1, D_MODEL)], axis=0)
    h0_ctx = jnp.zeros((BATCH, 2, D_RNN), F32)
    lat_blk = N_CTX_TOK // DEC_SEQ
    new_ckv, new_krope, new_lru, new_dk, new_dv = [], [], [], [], []
    moe_out = None
    for l in range(DEPTH):
        lam_init = 0.8 - 0.6 * math.exp(-0.3 * l)
        x, (qm, km, vm, qd, kd, vd, lx, lg, ckv_o, kro_o, dk_o, dv_o) = _front_call(l, x, moe_out, mod3, w)

        u_ctx, hf_ctx = _lru_call(l, lx, lg, h0_ctx, w, BATCH, SEQ, 0)
        u_lat, _ = _lru_call(l, lx, lg, state_lru[:, l], w, DEC_BATCH, DEC_SEQ, lat_blk)

        om_ctx = _attn_call(_mla_attn_kernel, "mla_ctx", qm, [(km, vm, "tok")], [], [], 0, BATCH, 1, 512)
        om_lat = _attn_call(_mla_attn_kernel, "mla_lat", qm,
                            [(k_ctx, v_ctx, lambda b, t: (l, b, 0, 0)), (km, vm, "tok")], [], [],
                            lat_blk, DEC_BATCH, LAT_TILES_PER_SEQ, 512)

        dk_extra = [w["diff_lambda"], w["diff_subln"]]
        dk_specs = lambda: [pl.BlockSpec((None, 4, DIFF_DH), lambda b, t: (l, 0, 0)),
                            pl.BlockSpec((None, 1, LANES), lambda b, t: (l, 0, 0))]
        dkern = lambda n_seg, *refs: _diff_attn_kernel(n_seg, lam_init, *refs)
        od_ctx = _attn_call(dkern, "diff_ctx", qd, [(kd, vd, "tok")], dk_extra, dk_specs(),
                            0, BATCH, 1, 512)
        od_lat = _attn_call(dkern, "diff_lat", qd,
                            [(dk_ctx, dv_ctx, lambda b, t: (b, l, 0, 0)), (kd, vd, "tok")], dk_extra, dk_specs(),
                            lat_blk, DEC_BATCH, LAT_TILES_PER_SEQ, 512)

        x, h2t, ids, sw, seg_off, n_gran = _back_call(
            l, x, mod3, (om_ctx, om_lat), (u_ctx, u_lat), (od_ctx, od_lat), w)
        moe_out = _moe_call(l, h2t, ids, sw, seg_off, n_gran, w)

        new_ckv.append(ckv_o.reshape(BATCH, SEQ, KV_LORA))
        new_krope.append(kro_o.reshape(BATCH, SEQ, MLA_ROPE))
        new_lru.append(hf_ctx)
        new_dk.append(dk_o.reshape(BATCH, SEQ, DIFF_HEADS, 2, DIFF_DH))
        new_dv.append(dv_o.reshape(BATCH, SEQ, DIFF_HEADS, DIFF_DV))

    x_ctx, x_lat = _residual_call(DEPTH - 1, x, moe_out, mod3)
    xp = x_ctx.reshape(BATCH, SEQ, D_MODEL)
    xs = x_lat.reshape(DEC_BATCH, DEC_SEQ, D_MODEL)
    return (xp, xs, jnp.stack(new_ckv, axis=1), jnp.stack(new_krope, axis=1), jnp.stack(new_lru, axis=1),
            jnp.stack(new_dk, axis=1), jnp.stack(new_dv, axis=1))
```

```python
import functools
import math

import jax
import jax.numpy as jnp
from jax import lax
from jax.experimental import pallas as pl
from jax.experimental.pallas import tpu as pltpu

F32 = jnp.float32
BF16 = jnp.bfloat16

D_MODEL = 1024
BATCH = 16
SEQ = 256
DEPTH = 4
DEC_BATCH = 8
DEC_SEQ = 1024
PAST_LEN = 256
GRID_W = 64
ROPE_BASE = 10000.0
NORM_EPS = 1e-6
LOG2_E = math.log2(math.e)

MLA_HEADS = 8
MLA_NOPE = 64
MLA_ROPE = 32
MLA_QK = MLA_NOPE + MLA_ROPE
MLA_V = 64
Q_LORA = 384
KV_LORA = 256
D_RNN = 512
LRU_BLOCKS = 8
LRU_BW = D_RNN // LRU_BLOCKS
CONV_W = 4
LRU_C = 8.0
DIFF_HEADS = 4
DIFF_DH = 64
DIFF_DV = 2 * DIFF_DH
N_EXPERTS = 32
TOP_K = 4
D_EXPERT = 512
SWIGLU_LIMIT = 7.0
SWIGLU_ALPHA = 1.702

OFF_QA = 0
OFF_KVA = OFF_QA + Q_LORA
OFF_LRU_X = OFF_KVA + KV_LORA + MLA_ROPE
OFF_LRU_G = OFF_LRU_X + D_RNN
OFF_DQ = OFF_LRU_G + D_RNN
OFF_DK = OFF_DQ + DIFF_HEADS * 2 * DIFF_DH
OFF_DV = OFF_DK + DIFF_HEADS * 2 * DIFF_DH
IN_COLS = OFF_DV + DIFF_HEADS * DIFF_DV

LANES = 128
TILE = 256
N_CTX_TILES = BATCH * SEQ // TILE
LAT_TILES_PER_SEQ = DEC_SEQ // TILE
N_LAT_TILES = DEC_BATCH * LAT_TILES_PER_SEQ
N_TILES = N_CTX_TILES + N_LAT_TILES
N_CTX_TOK = BATCH * SEQ
N_TOK = N_TILES * TILE
COND_ROWS = 16
CTX_COND_ROW = DEC_BATCH

P_QA = 0
P_CKV = P_QA + Q_LORA
P_KR = P_CKV + KV_LORA
P_LX = P_KR + LANES
P_LG = P_LX + D_RNN
P_DQ = P_LG + D_RNN
P_DK = P_DQ + 512
P_DV = P_DK + 512
P_COLS = P_DV + 512

SUB = 8
MOE_TILES = 16
MOE_TOK = MOE_TILES * TILE
N_MOE_BLOCKS = N_TOK // MOE_TOK
GRAN = 8
SLOTS = TILE * TOP_K + N_EXPERTS * GRAN
MOE_CHUNK = 256
CHUNK_GRAN = MOE_CHUNK // GRAN
CHUNK_SHIFT = CHUNK_GRAN.bit_length() - 1
MAX_GRAN = MOE_TILES * SLOTS // GRAN
VMEM_LIMIT = 56 * 1024 * 1024


def _cond_row(i):
    return jnp.where(i < N_CTX_TILES, CTX_COND_ROW, (i - N_CTX_TILES) // LAT_TILES_PER_SEQ)


def _rope_idx(i):
    return jnp.where(i < N_CTX_TILES, 0, 1 + (i - N_CTX_TILES) % LAT_TILES_PER_SEQ)


def _ctx_out_idx(i):
    return jnp.minimum(i, N_CTX_TILES - 1)


def _const_spec(shape, index):
    return pl.BlockSpec(shape, lambda *_: index, pipeline_mode=pl.Buffered(1))


def _dot(a, b):
    return jnp.dot(a, b, preferred_element_type=F32)


def _dot_nt(a, b):
    return lax.dot_general(a, b, (((1,), (1,)), ((), ())), preferred_element_type=F32)


def _rms(x, n=None):
    n = x.shape[-1] if n is None else n
    ss = jnp.sum(x * x, axis=-1, keepdims=True)
    return x * lax.rsqrt(ss * (1.0 / n) + NORM_EPS)


def _rope(x, cos, sin_signed, swap):
    return x * cos + _dot(x.astype(BF16), swap) * sin_signed


def _sigmoid(x):
    return 0.5 * jnp.tanh(0.5 * x) + 0.5


def _lane_iota(shape):
    return lax.broadcasted_iota(jnp.int32, shape, len(shape) - 1)


def _ada_kernel(cond_ref, w_ref, b_ref, o_ref):
    c = cond_ref[...]
    s = c * jax.nn.sigmoid(c)
    o_ref[...] = _dot(s.astype(BF16), w_ref[...].astype(BF16)) + b_ref[...]


def _ada_call(cond, ada_w, ada_b3):
    cb = 1024
    return pl.pallas_call(
        _ada_kernel,
        grid=(DEPTH, 6 * D_MODEL // cb),
        in_specs=[
            pl.BlockSpec((COND_ROWS, D_MODEL), lambda l, j: (0, 0)),
            pl.BlockSpec((None, D_MODEL, cb), lambda l, j: (l, 0, j)),
            pl.BlockSpec((None, 1, cb), lambda l, j: (l, 0, j)),
        ],
        out_specs=pl.BlockSpec((None, COND_ROWS, cb), lambda l, j: (l, 0, j)),
        out_shape=jax.ShapeDtypeStruct((DEPTH, COND_ROWS, 6 * D_MODEL), F32),
        compiler_params=pltpu.CompilerParams(
            dimension_semantics=("arbitrary", "arbitrary"), vmem_limit_bytes=VMEM_LIMIT),
        name="ada_mod",
    )(cond, ada_w, ada_b3)


def _mla_k_heads(kn, krs, kn_gain, cos, sin, swap, store):
    for hd in range(MLA_HEADS):
        kh = kn[:, hd * LANES:(hd + 1) * LANES] + krs
        kh = _rms(kh, MLA_QK) * kn_gain
        if cos is not None:
            kh = _rope(kh, cos, sin, swap)
        store(hd, kh)


def _kvx_kernel(ckv_ref, kr_ref, wk_ref, wv_ref, kn_ref, place_ref, k_ref, v_ref):
    c = ckv_ref[...].astype(BF16)
    kn = _dot(c, wk_ref[...])
    v_ref[...] = _dot(c, wv_ref[...]).astype(BF16)
    krs = _dot(kr_ref[...].astype(BF16), place_ref[...])

    def store(hd, kh):
        k_ref[:, hd * LANES:(hd + 1) * LANES] = kh.astype(BF16)

    _mla_k_heads(kn, krs, kn_ref[...], None, None, None, store)


def _kvx_call(cache_ckv, cache_krope, wkvk, wkvv, kn_pad, place):
    return pl.pallas_call(
        _kvx_kernel,
        grid=(DEPTH, DEC_BATCH),
        in_specs=[
            pl.BlockSpec((None, None, PAST_LEN, KV_LORA), lambda l, b: (b, l, 0, 0)),
            pl.BlockSpec((None, None, PAST_LEN, MLA_ROPE), lambda l, b: (b, l, 0, 0)),
            pl.BlockSpec((None, KV_LORA, MLA_HEADS * LANES), lambda l, b: (l, 0, 0)),
            pl.BlockSpec((None, KV_LORA, MLA_HEADS * MLA_V), lambda l, b: (l, 0, 0)),
            pl.BlockSpec((None, 1, LANES), lambda l, b: (l, 0, 0)),
            pl.BlockSpec((MLA_ROPE, LANES), lambda l, b: (0, 0)),
        ],
        out_specs=[
            pl.BlockSpec((None, None, PAST_LEN, MLA_HEADS * LANES), lambda l, b: (l, b, 0, 0)),
            pl.BlockSpec((None, None, PAST_LEN, MLA_HEADS * MLA_V), lambda l, b: (l, b, 0, 0)),
        ],
        out_shape=[
            jax.ShapeDtypeStruct((DEPTH, DEC_BATCH, PAST_LEN, MLA_HEADS * LANES), BF16),
            jax.ShapeDtypeStruct((DEPTH, DEC_BATCH, PAST_LEN, MLA_HEADS * MLA_V), BF16),
        ],
        compiler_params=pltpu.CompilerParams(
            dimension_semantics=("arbitrary", "arbitrary"), vmem_limit_bytes=VMEM_LIMIT),
        name="ctx_kv_expand",
    )(cache_ckv, cache_krope, wkvk, wkvv, kn_pad, place)


def _modulated(x, gain, shift, scale):
    return _rms(x) * gain * (1.0 + scale) + shift


def _untile(t_ref):
    rows = t_ref.shape[0] // SUB
    return jnp.concatenate([t_ref[pl.ds(s, rows, stride=SUB), :] for s in range(SUB)], axis=-1)


def _moe_residual(x1_ref, yt_ref, modp_ref):
    return x1_ref[...] + modp_ref[:, 5 * D_MODEL:6 * D_MODEL] * _untile(yt_ref)


def _front_kernel(has_moe, *refs):
    if has_moe:
        x1_ref, yt_ref, modp_ref = refs[:3]
        refs = refs[3:]
    else:
        x_ref = refs[0]
        refs = refs[1:]
    (mod_ref, n1_ref, win_ref, qan_ref, wqb_ref, kvan_ref, wkvk_ref, wkvv_ref,
     qn_ref, kn_ref, dqn_ref, dkn_ref, cosm_ref, sinm_ref, cosd_ref, sind_ref, swapm_ref, swapd_ref) = refs[:18]
    refs = refs[18:]
    if has_moe:
        xo_ref = refs[0]
        refs = refs[1:]
    (qm_ref, km_ref, vm_ref, qd_ref, kd_ref, vd_ref, lx_ref, lg_ref,
     ckv_ref, kro_ref, dko_ref, dvo_ref) = refs
    if has_moe:
        x = _moe_residual(x1_ref, yt_ref, modp_ref)
        xo_ref[...] = x
    else:
        x = x_ref[...]
    mod = mod_ref[...]
    h = _modulated(x, n1_ref[...], mod[:, 0:D_MODEL], mod[:, D_MODEL:2 * D_MODEL])
    hb = h.astype(BF16)

    lane = _lane_iota((1, LANES))
    cosm, sinm, swap_m = cosm_ref[...], sinm_ref[...], swapm_ref[...]
    cosd, sind, swap_d = cosd_ref[...], sind_ref[...], swapd_ref[...]

    qa = _rms(_dot(hb, win_ref[:, P_QA:P_QA + Q_LORA])) * qan_ref[...]
    q = _dot(qa.astype(BF16), wqb_ref[...])
    q_scale = MLA_QK ** -0.5 * LOG2_E
    for hd in range(MLA_HEADS):
        qh = _rms(q[:, hd * LANES:(hd + 1) * LANES], MLA_QK) * qn_ref[...]
        qh = _rope(qh, cosm, sinm, swap_m)
        qm_ref[:, hd * LANES:(hd + 1) * LANES] = (qh * q_scale).astype(BF16)

    ckv = _rms(_dot(hb, win_ref[:, P_CKV:P_CKV + KV_LORA])) * kvan_ref[...]
    krs = _dot(hb, win_ref[:, P_KR:P_KR + LANES])
    cb = ckv.astype(BF16)
    vm_ref[...] = _dot(cb, wkvv_ref[...]).astype(BF16)
    kn = _dot(cb, wkvk_ref[...])

    def store_k(hd, kh):
        km_ref[:, hd * LANES:(hd + 1) * LANES] = kh.astype(BF16)

    _mla_k_heads(kn, krs, kn_ref[...], cosm, sinm, swap_m, store_k)

    lx_ref[...] = _dot(hb, win_ref[:, P_LX:P_LX + D_RNN])
    lg_ref[...] = _dot(hb, win_ref[:, P_LG:P_LG + D_RNN])

    low = lane < DIFF_DH

    def pair_norm(t):
        sq = t * t
        s_all = jnp.sum(sq, axis=-1, keepdims=True)
        s_lo = jnp.sum(jnp.where(low, sq, 0.0), axis=-1, keepdims=True)
        r_lo = lax.rsqrt(s_lo * (1.0 / DIFF_DH) + NORM_EPS)
        r_hi = lax.rsqrt((s_all - s_lo) * (1.0 / DIFF_DH) + NORM_EPS)
        return t * jnp.where(low, r_lo, r_hi)

    dq = _dot(hb, win_ref[:, P_DQ:P_DQ + 512])
    dk = _dot(hb, win_ref[:, P_DK:P_DK + 512])
    d_scale = DIFF_DH ** -0.5 * LOG2_E
    k_heads = []
    for hd in range(DIFF_HEADS):
        sl = slice(hd * LANES, (hd + 1) * LANES)
        qh = _rope(pair_norm(dq[:, sl]) * dqn_ref[...], cosd, sind, swap_d)
        qd_ref[:, sl] = (qh * d_scale).astype(BF16)
        kh = _rope(pair_norm(dk[:, sl]) * dkn_ref[...], cosd, sind, swap_d)
        kd_ref[:, sl] = kh.astype(BF16)
        k_heads.append(kh)
    dv = _dot(hb, win_ref[:, P_DV:P_DV + 512])
    vd_ref[...] = dv.astype(BF16)

    @pl.when(pl.program_id(0) < N_CTX_TILES)
    def _():
        ckv_ref[...] = ckv
        kro_ref[...] = krs[:, MLA_NOPE:MLA_NOPE + MLA_ROPE]
        for hd in range(DIFF_HEADS):
            dko_ref[:, hd * LANES:(hd + 1) * LANES] = k_heads[hd]
        dvo_ref[...] = dv


def _mod_spec(l):
    return pl.BlockSpec((None, 1, 6 * D_MODEL), lambda i: (l * COND_ROWS + _cond_row(i), 0, 0))


def _tile_spec():
    return pl.BlockSpec((TILE * SUB, LANES), lambda i: (i, 0))


def _front_call(l, x, moe_out, mod3, w):
    tok = lambda width: pl.BlockSpec((TILE, width), lambda i: (i, 0))
    ctx = lambda width: pl.BlockSpec((TILE, width), lambda i: (_ctx_out_idx(i), 0))
    lay = lambda *shape: _const_spec((None,) + shape, (l,) + (0,) * len(shape))
    rope = pl.BlockSpec((None, TILE, LANES), lambda i: (_rope_idx(i), 0, 0))
    swap = pl.BlockSpec((LANES, LANES), lambda i: (0, 0))
    n_ctx_rows = N_CTX_TOK
    has_moe = moe_out is not None
    if has_moe:
        lead_specs = [tok(D_MODEL), _tile_spec(), _mod_spec(l - 1)]
        lead_args = [x, moe_out, mod3]
        x_out_specs = [tok(D_MODEL)]
        x_out_shape = [jax.ShapeDtypeStruct((N_TOK, D_MODEL), F32)]
    else:
        lead_specs, lead_args, x_out_specs, x_out_shape = [tok(D_MODEL)], [x], [], []
    outs = pl.pallas_call(
        functools.partial(_front_kernel, has_moe),
        grid=(N_TILES,),
        in_specs=lead_specs + [
            _mod_spec(l),
            lay(1, D_MODEL), lay(D_MODEL, P_COLS), lay(1, Q_LORA), lay(Q_LORA, MLA_HEADS * LANES),
            lay(1, KV_LORA), lay(KV_LORA, MLA_HEADS * LANES), lay(KV_LORA, MLA_HEADS * MLA_V),
            lay(1, LANES), lay(1, LANES), lay(1, LANES), lay(1, LANES),
            rope, rope, rope, rope, swap, swap,
        ],
        out_specs=x_out_specs + [
            tok(1024), tok(1024), tok(512), tok(512), tok(512), tok(512), tok(512), tok(512),
            ctx(KV_LORA), ctx(MLA_ROPE), ctx(512), ctx(512),
        ],
        out_shape=x_out_shape + [
            jax.ShapeDtypeStruct((N_TOK, 1024), BF16), jax.ShapeDtypeStruct((N_TOK, 1024), BF16),
            jax.ShapeDtypeStruct((N_TOK, 512), BF16), jax.ShapeDtypeStruct((N_TOK, 512), BF16),
            jax.ShapeDtypeStruct((N_TOK, 512), BF16), jax.ShapeDtypeStruct((N_TOK, 512), BF16),
            jax.ShapeDtypeStruct((N_TOK, 512), F32), jax.ShapeDtypeStruct((N_TOK, 512), F32),
            jax.ShapeDtypeStruct((n_ctx_rows, KV_LORA), F32),
            jax.ShapeDtypeStruct((n_ctx_rows, MLA_ROPE), F32),
            jax.ShapeDtypeStruct((n_ctx_rows, 512), F32),
            jax.ShapeDtypeStruct((n_ctx_rows, 512), F32),
        ],
        compiler_params=pltpu.CompilerParams(
            dimension_semantics=("arbitrary",), vmem_limit_bytes=VMEM_LIMIT),
        name="front",
    )(*lead_args, mod3, w["norm1"], w["w_in"], w["qa_norm"], w["wqb"], w["kva_norm"], w["wkvk"], w["wkvv"],
      w["qn"], w["kn"], w["dqn"], w["dkn"], w["cos_m"], w["sin_m"], w["cos_d"], w["sin_d"],
      w["swap_m"], w["swap_d"])
    if has_moe:
        return outs[0], outs[1:]
    return x, outs


def _residual_kernel(x1_ref, yt_ref, modp_ref, oc_ref, ol_ref):
    x = _moe_residual(x1_ref, yt_ref, modp_ref)
    is_ctx = pl.program_id(0) < N_CTX_TILES

    @pl.when(is_ctx)
    def _():
        oc_ref[...] = x

    @pl.when(jnp.logical_not(is_ctx))
    def _():
        ol_ref[...] = x


def _residual_call(l, x1, moe_out, mod3):
    tok = pl.BlockSpec((TILE, D_MODEL), lambda i: (i, 0))
    return pl.pallas_call(
        _residual_kernel,
        grid=(N_TILES,),
        in_specs=[tok, _tile_spec(), _mod_spec(l)],
        out_specs=[
            pl.BlockSpec((TILE, D_MODEL), lambda i: (jnp.minimum(i, N_CTX_TILES - 1), 0)),
            pl.BlockSpec((TILE, D_MODEL), lambda i: (jnp.maximum(i - N_CTX_TILES, 0), 0)),
        ],
        out_shape=[
            jax.ShapeDtypeStruct((N_CTX_TOK, D_MODEL), F32),
            jax.ShapeDtypeStruct((N_TOK - N_CTX_TOK, D_MODEL), F32),
        ],
        compiler_params=pltpu.CompilerParams(
            dimension_semantics=("arbitrary",), vmem_limit_bytes=VMEM_LIMIT),
        name="moe_residual",
    )(x1, moe_out, mod3)


def _scan(a, b, h0, reverse):
    n = a.shape[0]
    row = lax.broadcasted_iota(jnp.int32, (n, 1), 0)
    d = 1
    while d < n:
        if reverse:
            valid = row < n - d
            shift = n - d
        else:
            valid = row >= d
            shift = d
        a_s = jnp.where(valid, pltpu.roll(a, shift, 0), 1.0)
        b_s = jnp.where(valid, pltpu.roll(b, shift, 0), 0.0)
        b = a * b_s + b
        a = a * a_s
        d *= 2
    return a * h0 + b


def _gelu_tanh(x):
    return 0.5 * x * (1.0 + jnp.tanh(math.sqrt(2.0 / math.pi) * (x + 0.044715 * x * x * x)))


N_LANE_TILES = D_RNN // LANES


def _strided_scan(a, b, h0, reverse, first, sa_ref, sb_ref, tot_ref):
    n = a.shape[0]
    g = n // SUB
    for c in range(N_LANE_TILES):
        sa_ref[pl.ds(c * n, n), :] = a[:, c * LANES:(c + 1) * LANES]
        sb_ref[pl.ds(c * n, n), :] = b[:, c * LANES:(c + 1) * LANES]
    order = range(SUB - 1, -1, -1) if reverse else range(SUB)
    row = lax.broadcasted_iota(jnp.int32, (g, 1), 0)
    finals = []
    for c in range(N_LANE_TILES):
        comps = []
        acc_a = acc_b = None
        for r in order:
            cls = pl.ds(c * n + r, g, stride=SUB)
            a_r, b_r = sa_ref[cls, :], sb_ref[cls, :]
            if acc_a is None:
                acc_a, acc_b = a_r, b_r
            else:
                acc_b = a_r * acc_b + b_r
                acc_a = a_r * acc_a
            comps.append((cls, acc_a, acc_b))
        h0c = h0[:, c * LANES:(c + 1) * LANES]
        hg = _scan(acc_a, acc_b, h0c, reverse)
        if reverse:
            carry = jnp.where(row == g - 1, h0c, pltpu.roll(hg, g - 1, 0))
            finals.append(hg[0:1, :])
        else:
            carry = jnp.where(row == 0, h0c, pltpu.roll(hg, 1, 0))
            finals.append(hg[g - 1:g, :])
        for cls, comp_a, comp_b in comps:
            h = comp_a * carry + comp_b
            tot_ref[cls, :] = h if first else tot_ref[cls, :] + h
    return jnp.concatenate(finals, axis=-1)


def _sqrt_pos(t):
    return jnp.where(t > 0.0, t * lax.rsqrt(t), 0.0)


def _lru_kernel(lx_ref, lg_ref, cw_ref, cb_ref, wbd_ref, gb_ref, lam_ref, h0_ref, u_ref, hf_ref,
                sa_ref, sb_ref, tot_ref):
    x = lx_ref[...]
    n = x.shape[0]
    row = lax.broadcasted_iota(jnp.int32, (n, 1), 0)
    cw = cw_ref[...]
    xr = cb_ref[...] + cw[2:3] * x
    xr = xr + cw[0:1] * jnp.where(row >= 2, pltpu.roll(x, 2, 0), 0.0)
    xr = xr + cw[1:2] * jnp.where(row >= 1, pltpu.roll(x, 1, 0), 0.0)
    xr = xr + cw[3:4] * jnp.where(row < n - 1, pltpu.roll(x, n - 1, 0), 0.0)
    xb = xr.astype(BF16)
    half = D_RNN // 2
    lam = lam_ref[...]
    h0 = h0_ref[...]
    gb = gb_ref[...]
    for d in range(2):
        pre = []
        for g in range(2):
            k = d * 2 + g
            p = jnp.concatenate([_dot(xb[:, :half], wbd_ref[k, 0]), _dot(xb[:, half:], wbd_ref[k, 1])],
                                axis=-1)
            pre.append(p + gb[k:k + 1])
        r = _sigmoid(pre[0])
        i = _sigmoid(pre[1])
        z = -lam[d:d + 1]
        softplus = jnp.maximum(z, 0.0) + jnp.log(1.0 + jnp.exp(-jnp.abs(z)))
        a = jnp.exp(-LRU_C * r * softplus)
        bx = _sqrt_pos(1.0 - a * a) * (i * xr)
        hf_ref[d:d + 1, :] = _strided_scan(a, bx, h0[d:d + 1], d == 1, d == 0, sa_ref, sb_ref, tot_ref)
    total = jnp.concatenate([tot_ref[pl.ds(c * n, n), :] for c in range(N_LANE_TILES)], axis=-1)
    u_ref[...] = (total * _gelu_tanh(lg_ref[...])).astype(BF16)


def _lru_call(l, lx, lg, h0, w, n_seq, seq_len, row_block0):
    seq = lambda: pl.BlockSpec((seq_len, D_RNN), lambda s: (row_block0 + s, 0))
    lay = lambda *shape: _const_spec((None,) + shape, (l,) + (0,) * len(shape))
    return pl.pallas_call(
        _lru_kernel,
        grid=(n_seq,),
        in_specs=[
            seq(), seq(), lay(CONV_W, D_RNN), lay(1, D_RNN), lay(4, 2, D_RNN // 2, D_RNN // 2),
            lay(4, D_RNN), lay(2, D_RNN),
            pl.BlockSpec((None, 2, D_RNN), lambda s: (s, 0, 0)),
        ],
        out_specs=[
            pl.BlockSpec((seq_len, D_RNN), lambda s: (s, 0)),
            pl.BlockSpec((None, 2, D_RNN), lambda s: (s, 0, 0)),
        ],
        out_shape=[
            jax.ShapeDtypeStruct((n_seq * seq_len, D_RNN), BF16),
            jax.ShapeDtypeStruct((n_seq, 2, D_RNN), F32),
        ],
        scratch_shapes=[pltpu.VMEM((N_LANE_TILES * seq_len, LANES), F32)] * 3,
        compiler_params=pltpu.CompilerParams(
            dimension_semantics=("arbitrary",), vmem_limit_bytes=VMEM_LIMIT),
        name="rglru_%d" % seq_len,
    )(lx, lg, w["conv_w"], w["conv_b"], w["wbd"], w["gate_b"], w["lam"], h0)


def _softmax_parts(scores):
    m = scores[0].max(axis=-1, keepdims=True)
    for s in scores[1:]:
        m = jnp.maximum(m, s.max(axis=-1, keepdims=True))
    es = [jnp.exp2(s - m) for s in scores]
    tot = es[0].sum(axis=-1, keepdims=True)
    for e in es[1:]:
        tot = tot + e.sum(axis=-1, keepdims=True)
    return es, tot


def _mla_attn_kernel(n_seg, q_ref, *refs):
    k_refs = refs[0:2 * n_seg:2]
    v_refs = refs[1:2 * n_seg:2]
    o_ref = refs[2 * n_seg]
    low = _lane_iota((1, LANES)) < MLA_V
    for pair in range(MLA_HEADS // 2):
        vs = slice(pair * LANES, (pair + 1) * LANES)
        out = None
        for hh in range(2):
            hs = slice((2 * pair + hh) * LANES, (2 * pair + hh + 1) * LANES)
            qh = q_ref[:, hs]
            es, tot = _softmax_parts([_dot_nt(qh, k[:, hs]) for k in k_refs])
            pv = None
            for e, v in zip(es, v_refs):
                t = _dot(e.astype(BF16), v[:, vs])
                pv = t if pv is None else pv + t
            oh = pv / tot
            out = oh if hh == 0 else jnp.where(low, out, oh)
        o_ref[:, vs] = out.astype(BF16)


def _diff_attn_kernel(n_seg, lam_init, q_ref, *refs):
    k_refs = refs[0:2 * n_seg:2]
    v_refs = refs[1:2 * n_seg:2]
    lam_ref, sub_ref, o_ref = refs[2 * n_seg:2 * n_seg + 3]
    lp = lam_ref[...]
    lam = (jnp.exp(jnp.sum(lp[0:1] * lp[1:2], axis=-1, keepdims=True))
           - jnp.exp(jnp.sum(lp[2:3] * lp[3:4], axis=-1, keepdims=True)) + lam_init)
    low = _lane_iota((1, LANES)) < DIFF_DH
    zero = jnp.zeros((), BF16)
    for hd in range(DIFF_HEADS):
        hs = slice(hd * LANES, (hd + 1) * LANES)
        qh = q_ref[:, hs]
        ks = [k[:, hs].astype(BF16) for k in k_refs]
        e0, t0 = _softmax_parts([_dot_nt(jnp.where(low, qh, zero), k) for k in ks])
        e1, t1 = _softmax_parts([_dot_nt(jnp.where(low, zero, qh), k) for k in ks])
        w0 = 1.0 / t0
        w1 = lam / t1
        o = None
        for a, b, v in zip(e0, e1, v_refs):
            t = _dot((a * w0 - b * w1).astype(BF16), v[:, hs].astype(BF16))
            o = t if o is None else o + t
        o = _rms(o) * sub_ref[...] * (1.0 - lam_init)
        o_ref[:, hs] = o.astype(BF16)


def _over_q_tiles(kernel, n_seg, tiles, q_ref, *refs):
    if tiles == 1:
        kernel(n_seg, q_ref, *refs)
        return

    def tile(t, carry):
        rows = pl.ds(pl.multiple_of(t * TILE, TILE), TILE)
        kernel(n_seg, q_ref.at[rows], *refs[:-1], refs[-1].at[rows])
        return carry

    lax.fori_loop(0, tiles, tile, 0)


def _attn_call(kernel, name, q, segs, extra, extra_specs, q_block0, n_seq, tiles_per_seq, out_width):
    seq_len = tiles_per_seq * TILE
    in_specs = [pl.BlockSpec((seq_len, q.shape[1]), lambda b: (q_block0 + b, 0))]
    args = [q]
    for k, v, kind in segs:
        for arr in (k, v):
            if kind == "tok":
                in_specs.append(pl.BlockSpec((seq_len, arr.shape[1]), lambda b: (q_block0 + b, 0)))
            else:
                in_specs.append(pl.BlockSpec((None, None) + arr.shape[2:], kind))
            args.append(arr)
    in_specs += extra_specs
    args += extra
    return pl.pallas_call(
        functools.partial(_over_q_tiles, kernel, len(segs), tiles_per_seq),
        grid=(n_seq,),
        in_specs=in_specs,
        out_specs=pl.BlockSpec((seq_len, out_width), lambda b: (b, 0)),
        out_shape=jax.ShapeDtypeStruct((n_seq * seq_len, out_width), BF16),
        compiler_params=pltpu.CompilerParams(
            dimension_semantics=("arbitrary",), vmem_limit_bytes=VMEM_LIMIT),
        name=name,
    )(*args)


def _back_kernel(x_ref, mod_ref, n1_ref, n2_ref, omc_ref, oml_ref, ulc_ref, ull_ref, odc_ref, odl_ref,
                 wom_ref, wol_ref, wod_ref, wmg_ref, bmg_ref, wout_ref, rw_ref, rb_ref,
                 x1_ref, h2t_ref, ids_ref, sw_ref, off_ref, ng_ref):
    i = pl.program_id(0)
    is_ctx = i < N_CTX_TILES
    x = x_ref[...]
    mod = mod_ref[...]
    seg = lambda j: mod[:, j * D_MODEL:(j + 1) * D_MODEL]
    hb = _modulated(x, n1_ref[...], seg(0), seg(1)).astype(BF16)
    pick = lambda c_ref, l_ref: jnp.where(is_ctx, c_ref[...], l_ref[...])
    branches = (_dot(pick(omc_ref, oml_ref), wom_ref[...]), _dot(pick(ulc_ref, ull_ref), wol_ref[...]),
                _dot(pick(odc_ref, odl_ref), wod_ref[...]))
    merged = None
    for j, o in enumerate(branches):
        cs = slice(j * D_MODEL, (j + 1) * D_MODEL)
        g = _sigmoid(_dot(hb, wmg_ref[:, cs]) + bmg_ref[:, cs])
        merged = g * o if merged is None else merged + g * o
    x1 = x + seg(2) * _dot(merged.astype(BF16), wout_ref[...])
    x1_ref[...] = x1
    h2 = _modulated(x1, n2_ref[...], seg(3), seg(4))
    for s in range(SUB):
        h2t_ref[pl.ds(s, TILE, stride=SUB), :] = h2[:, s * LANES:(s + 1) * LANES]

    h_hi = h2.astype(BF16)
    h_lo = (h2 - h_hi.astype(F32)).astype(BF16)
    rw = rw_ref[...]
    w_hi = rw.astype(BF16)
    w_lo = (rw - w_hi.astype(F32)).astype(BF16)
    logits = _dot(h_hi, w_hi) + (_dot(h_lo, w_hi) + _dot(h_hi, w_lo)) + rb_ref[...]
    lane = _lane_iota(logits.shape).astype(F32)
    neg = jnp.float32(-jnp.inf)
    work = jnp.where(lane < N_EXPERTS, logits, neg)
    sels, probs = [], []
    top = None
    for _ in range(TOP_K):
        m = work.max(axis=-1, keepdims=True)
        idx = jnp.min(jnp.where(work == m, lane, float(LANES)), axis=-1, keepdims=True)
        sel = lane == idx
        top = m if top is None else top
        sels.append(sel)
        probs.append(jnp.exp(m - top))
        work = jnp.where(sel, neg, work)
    denom = probs[0] + probs[1] + probs[2] + probs[3]

    onehot = jnp.zeros_like(logits)
    for sel in sels:
        onehot = onehot + jnp.where(sel, 1.0, 0.0)
    r_i = lax.broadcasted_iota(jnp.int32, (TILE, TILE), 0)
    c_i = lax.broadcasted_iota(jnp.int32, (TILE, TILE), 1)
    earlier = jnp.where(c_i < r_i, 1.0, 0.0).astype(BF16)
    rank = _dot(earlier, onehot.astype(BF16))
    count = jnp.sum(onehot, axis=0, keepdims=True)
    n_gran = jnp.floor((count + (GRAN - 1.0)) * (1.0 / GRAN))
    r_l = lax.broadcasted_iota(jnp.int32, (LANES, LANES), 0)
    c_l = lax.broadcasted_iota(jnp.int32, (LANES, LANES), 1)
    before = jnp.where(r_l < c_l, 1.0, 0.0).astype(BF16)
    seg_off = _dot(jnp.broadcast_to(n_gran * GRAN, (SUB, LANES)).astype(BF16), before)[0:1]
    slot_base = seg_off + rank
    slot_lane = lax.broadcasted_iota(jnp.int32, (1, SLOTS), 1).astype(F32)
    tok_col = (lax.broadcasted_iota(jnp.int32, (TILE, 1), 0) + (i % MOE_TILES) * TILE).astype(F32)
    id_acc = jnp.zeros((TILE, SLOTS), F32)
    w_acc = jnp.zeros((TILE, SLOTS), F32)
    for sel, p in zip(sels, probs):
        slot = jnp.sum(jnp.where(sel, slot_base, 0.0), axis=-1, keepdims=True)
        hit = slot_lane == slot
        id_acc = jnp.where(hit, tok_col, id_acc)
        w_acc = jnp.where(hit, p / denom, w_acc)
    ids_ref[...] = jnp.sum(id_acc, axis=0, keepdims=True).astype(jnp.int32)
    sw_ref[...] = jnp.sum(w_acc, axis=0, keepdims=True)
    off_ref[...] = seg_off.astype(jnp.int32)
    ng_ref[...] = n_gran.astype(jnp.int32)


def _back_call(l, x, mod3, om, ul, od, w):
    tok = lambda width: pl.BlockSpec((TILE, width), lambda i: (i, 0))
    ctx_in = pl.BlockSpec((TILE, 512), lambda i: (jnp.minimum(i, N_CTX_TILES - 1), 0))
    lat_in = pl.BlockSpec((TILE, 512), lambda i: (jnp.maximum(i - N_CTX_TILES, 0), 0))
    lay = lambda *shape: _const_spec((None,) + shape, (l,) + (0,) * len(shape))
    row = lambda width: pl.BlockSpec((None, 1, width), lambda i: (i, 0, 0))
    return pl.pallas_call(
        _back_kernel,
        grid=(N_TILES,),
        in_specs=[
            tok(D_MODEL), _mod_spec(l),
            lay(1, D_MODEL), lay(1, D_MODEL), ctx_in, lat_in, ctx_in, lat_in, ctx_in, lat_in,
            lay(512, D_MODEL), lay(512, D_MODEL), lay(512, D_MODEL),
            lay(D_MODEL, 3 * D_MODEL), lay(1, 3 * D_MODEL), lay(D_MODEL, D_MODEL),
            lay(D_MODEL, LANES), lay(1, LANES),
        ],
        out_specs=[tok(D_MODEL), _tile_spec(), row(SLOTS), row(SLOTS), row(LANES), row(LANES)],
        out_shape=[
            jax.ShapeDtypeStruct((N_TOK, D_MODEL), F32),
            jax.ShapeDtypeStruct((N_TOK * SUB, LANES), F32),
            jax.ShapeDtypeStruct((N_TILES, 1, SLOTS), jnp.int32),
            jax.ShapeDtypeStruct((N_TILES, 1, SLOTS), F32),
            jax.ShapeDtypeStruct((N_TILES, 1, LANES), jnp.int32),
            jax.ShapeDtypeStruct((N_TILES, 1, LANES), jnp.int32),
        ],
        compiler_params=pltpu.CompilerParams(
            dimension_semantics=("arbitrary",), vmem_limit_bytes=VMEM_LIMIT),
        name="back",
    )(x, mod3, w["norm1"], w["norm2"], om[0], om[1], ul[0], ul[1], od[0], od[1],
      w["w_o_mla"], w["w_o_lru"], w["w_o_diff"],
      w["w_merge"], w["b_merge"], w["w_out"], w["router_w"], w["router_b"])


def _moe_kernel(ids_ref, sw_ref, off_ref, ng_ref, h2t_ref, w1_ref, b1_ref, w2_ref, b2_ref,
                y_ref, xga_ref, xgb_ref, za_ref, zb_ref, gb_ref):
    e = pl.program_id(1)

    @pl.when(e == 0)
    def _():
        y_ref[...] = jnp.zeros_like(y_ref)
        for ref in (xga_ref, xgb_ref, za_ref, zb_ref):
            ref[...] = jnp.zeros_like(ref)

    n_total = 0
    for t in range(MOE_TILES):
        first = t * SLOTS + off_ref[t * LANES + e]

        def add_gran(g, pos, first=first):
            gb_ref[pos] = first + g * GRAN
            return pos + 1

        n_total = lax.fori_loop(0, ng_ref[t * LANES + e], add_gran, n_total)
    for i in range(2 * CHUNK_GRAN):
        gb_ref[n_total + i] = SLOTS - GRAN
    n_chunks = lax.shift_right_logical(n_total + (CHUNK_GRAN - 1), CHUNK_SHIFT)

    def slot_rows(j):
        return pl.ds(j * SUB, SUB)

    def gather(t, xg_ref):
        for g in range(CHUNK_GRAN):
            base = gb_ref[t * CHUNK_GRAN + g]
            for u in range(GRAN):
                xg_ref[slot_rows(g * GRAN + u), :] = h2t_ref[ids_ref[base + u]]

    def expert(xg_ref, z_ref):
        xb = jnp.concatenate([xg_ref[pl.ds(s, MOE_CHUNK, stride=SUB), :] for s in range(SUB)],
                             axis=-1).astype(BF16)
        gu = _dot(xb, w1_ref[...].astype(BF16)) + b1_ref[...]
        gate = jnp.minimum(gu[:, :D_EXPERT], SWIGLU_LIMIT)
        up = jnp.clip(gu[:, D_EXPERT:], -SWIGLU_LIMIT, SWIGLU_LIMIT)
        act = (up + 1.0) * (gate * _sigmoid(SWIGLU_ALPHA * gate))
        y = _dot(act.astype(BF16), w2_ref[...].astype(BF16)) + b2_ref[...]
        for s in range(SUB):
            z_ref[pl.ds(s, MOE_CHUNK, stride=SUB), :] = y[:, s * LANES:(s + 1) * LANES]

    def combine(t, z_ref):
        for g in range(CHUNK_GRAN):
            base = gb_ref[t * CHUNK_GRAN + g]
            toks = [ids_ref[base + u] for u in range(GRAN)]
            new = [y_ref[toks[u]] + sw_ref[base + u] * z_ref[slot_rows(g * GRAN + u), :] for u in range(GRAN)]
            for u in reversed(range(GRAN)):
                y_ref[toks[u]] = new[u]

    gather(0, xga_ref)

    @pl.when(n_chunks > 0)
    def _():
        expert(xga_ref, za_ref)
        gather(1, xgb_ref)

    def tick_pair(p, carry):
        t = 2 * p + 1
        expert(xgb_ref, zb_ref)
        gather(t + 1, xga_ref)
        combine(t - 1, za_ref)

        @pl.when(t + 1 < n_chunks)
        def _():
            expert(xga_ref, za_ref)
            gather(t + 2, xgb_ref)
            combine(t, zb_ref)

        return carry

    lax.fori_loop(0, lax.shift_right_logical(n_chunks, 1), tick_pair, 0)

    @pl.when((n_chunks & 1) == 1)
    def _():
        combine(n_chunks - 1, za_ref)

    @pl.when(jnp.logical_and(n_chunks > 0, (n_chunks & 1) == 0))
    def _():
        combine(n_chunks - 1, zb_ref)


def _moe_call(l, h2t, ids, sw, seg_off, n_gran, w):
    smem = lambda n: pl.BlockSpec((MOE_TILES * n,), lambda j, e: (j,), memory_space=pltpu.SMEM,
                                  pipeline_mode=pl.Buffered(1))
    blk = pl.BlockSpec((MOE_TOK, SUB, LANES), lambda j, e: (j, 0, 0), pipeline_mode=pl.Buffered(1))
    return pl.pallas_call(
        _moe_kernel,
        grid=(N_MOE_BLOCKS, N_EXPERTS),
        in_specs=[
            smem(SLOTS), smem(SLOTS), smem(LANES), smem(LANES),
            blk,
            pl.BlockSpec((None, None, D_MODEL, 2 * D_EXPERT), lambda j, e: (l, e, 0, 0)),
            pl.BlockSpec((None, None, 1, 2 * D_EXPERT), lambda j, e: (l, e, 0, 0)),
            pl.BlockSpec((None, None, D_EXPERT, D_MODEL), lambda j, e: (l, e, 0, 0)),
            pl.BlockSpec((None, None, 1, D_MODEL), lambda j, e: (l, e, 0, 0)),
        ],
        out_specs=blk,
        out_shape=jax.ShapeDtypeStruct((N_TOK, SUB, LANES), F32),
        scratch_shapes=[pltpu.VMEM((MOE_CHUNK * SUB, LANES), F32)] * 4
        + [pltpu.SMEM((MAX_GRAN + 2 * CHUNK_GRAN,), jnp.int32)],
        compiler_params=pltpu.CompilerParams(
            dimension_semantics=("arbitrary", "arbitrary"), vmem_limit_bytes=VMEM_LIMIT),
        name="moe",
    )(ids.reshape(-1), sw.reshape(-1), seg_off.reshape(-1), n_gran.reshape(-1),
      h2t.reshape(N_TOK, SUB, LANES), w["exp_w1"], w["exp_b1"], w["exp_w2"], w["exp_b2"]
      ).reshape(N_TOK * SUB, LANES)


def _axial_tables(n_tokens, dim):
    rows = n_tokens // GRID_W
    row = jnp.repeat(jnp.arange(rows), GRID_W)
    col = jnp.tile(jnp.arange(GRID_W), rows)
    half = dim // 2
    inv = 1.0 / (ROPE_BASE ** (jnp.arange(0, half, 2, dtype=F32) / half))

    def axis_angles(pos):
        ang = pos.astype(F32)[:, None] * inv[None, :]
        return jnp.concatenate([ang, ang], axis=-1)

    ang = jnp.concatenate([axis_angles(row), axis_angles(col)], axis=-1)
    return jnp.cos(ang), jnp.sin(ang)


def _rope_slot_tables(dim, lane0, copies):
    cos, sin = _axial_tables(DEC_SEQ, dim)
    quarter = dim // 4
    sign = jnp.where((jnp.arange(dim) % (dim // 2)) < quarter, -1.0, 1.0)
    sin = sin * sign
    cos_slot = jnp.ones((DEC_SEQ, LANES), F32)
    sin_slot = jnp.zeros((DEC_SEQ, LANES), F32)
    for c in range(copies):
        cos_slot = cos_slot.at[:, lane0 + c * dim:lane0 + (c + 1) * dim].set(cos)
        sin_slot = sin_slot.at[:, lane0 + c * dim:lane0 + (c + 1) * dim].set(sin)
    ident = (jnp.ones((1, TILE, LANES), F32), jnp.zeros((1, TILE, LANES), F32))
    cos_t = jnp.concatenate([ident[0], cos_slot.reshape(LAT_TILES_PER_SEQ, TILE, LANES)], axis=0)
    sin_t = jnp.concatenate([ident[1], sin_slot.reshape(LAT_TILES_PER_SEQ, TILE, LANES)], axis=0)
    return cos_t, sin_t


def _half_swap(block):
    dst = jnp.arange(LANES)
    src = jnp.where((dst % block) < block // 2, dst + block // 2, dst - block // 2)
    return (jnp.arange(LANES)[:, None] == src[None, :]).astype(BF16)


def _pad_last(a, width):
    return jnp.pad(a, [(0, 0)] * (a.ndim - 1) + [(0, width - a.shape[-1])])


def _prepare(p):
    L = DEPTH
    w = {}
    w_in = p["w_in"]
    kr_slot = jnp.pad(w_in[:, :, OFF_KVA + KV_LORA:OFF_LRU_X], ((0, 0), (0, 0), (MLA_NOPE, LANES - MLA_QK)))
    w["w_in"] = jnp.concatenate(
        [w_in[:, :, OFF_QA:OFF_KVA + KV_LORA], kr_slot, w_in[:, :, OFF_LRU_X:IN_COLS]], axis=-1).astype(BF16)
    w["wqb"] = _pad_last(p["w_q_b"].reshape(L, Q_LORA, MLA_HEADS, MLA_QK), LANES).reshape(
        L, Q_LORA, MLA_HEADS * LANES).astype(BF16)
    kvb = p["w_kv_b"].reshape(L, KV_LORA, MLA_HEADS, MLA_NOPE + MLA_V)
    w["wkvk"] = _pad_last(kvb[..., :MLA_NOPE], LANES).reshape(L, KV_LORA, MLA_HEADS * LANES).astype(BF16)
    w["wkvv"] = kvb[..., MLA_NOPE:].reshape(L, KV_LORA, MLA_HEADS * MLA_V).astype(BF16)
    row = lambda a: a[:, None, :]
    w["norm1"] = row(p["norm1_g"])
    w["norm2"] = row(p["norm2_g"])
    w["qa_norm"] = row(p["mla_qa_norm"])
    w["kva_norm"] = row(p["mla_kva_norm"])
    w["qn"] = row(_pad_last(p["mla_qn"], LANES))
    w["kn"] = row(_pad_last(p["mla_kn"], LANES))
    w["dqn"] = row(jnp.tile(p["diff_qn"], (1, 2)))
    w["dkn"] = row(jnp.tile(p["diff_kn"], (1, 2)))
    w["cos_m"], w["sin_m"] = _rope_slot_tables(MLA_ROPE, MLA_NOPE, 1)
    w["cos_d"], w["sin_d"] = _rope_slot_tables(DIFF_DH, 0, 2)
    w["swap_m"] = _half_swap(MLA_ROPE // 2)
    w["swap_d"] = _half_swap(DIFF_DH // 2)
    w["conv_w"] = p["lru_conv_w"]
    w["conv_b"] = row(p["lru_conv_b"])
    per = (D_RNN // 2) // LRU_BW
    gw = p["lru_gate_w"].reshape(L, 4, 2, per, LRU_BW, LRU_BW)
    eye = jnp.eye(per, dtype=F32)
    w["wbd"] = jnp.einsum("lkhacd,ab->lkhacbd", gw, eye).reshape(L, 4, 2, D_RNN // 2, D_RNN // 2).astype(BF16)
    w["gate_b"] = p["lru_gate_b"].reshape(L, 4, D_RNN)
    w["lam"] = p["lru_lambda"]
    w["w_o_mla"] = p["w_o_mla"].astype(BF16)
    w["w_o_lru"] = p["w_o_lru"].astype(BF16)
    w["w_o_diff"] = p["w_o_diff"].astype(BF16)
    w["w_merge"] = p["w_merge"].astype(BF16)
    w["b_merge"] = row(p["b_merge"])
    w["w_out"] = p["w_out"].astype(BF16)
    w["router_w"] = _pad_last(p["router_w"], LANES)
    w["router_b"] = row(_pad_last(p["router_b"], LANES))
    w["exp_w1"] = p["exp_w1"]
    w["exp_b1"] = p["exp_b1"][:, :, None, :]
    w["exp_w2"] = p["exp_w2"]
    w["exp_b2"] = p["exp_b2"][:, :, None, :]
    w["diff_lambda"] = p["diff_lambda"]
    w["diff_subln"] = row(p["diff_subln"])
    return w


def kernel(x_prompt, x_sample, cache_mla_ckv, cache_mla_krope, state_lru, cache_diff_k, cache_diff_v,
           c, c_ctx, ada_w, ada_b, norm1_g, norm2_g, w_in, mla_qa_norm, w_q_b, mla_kva_norm, w_kv_b,
           mla_qn, mla_kn, w_o_mla, lru_conv_w, lru_conv_b, lru_gate_w, lru_gate_b, lru_lambda, w_o_lru,
           diff_qn, diff_kn, diff_lambda, diff_subln, w_o_diff, w_merge, b_merge, w_out,
           router_w, router_b, exp_w1, exp_b1, exp_w2, exp_b2):
    params = dict(
        norm1_g=norm1_g, norm2_g=norm2_g, w_in=w_in, mla_qa_norm=mla_qa_norm, w_q_b=w_q_b,
        mla_kva_norm=mla_kva_norm, w_kv_b=w_kv_b, mla_qn=mla_qn, mla_kn=mla_kn, w_o_mla=w_o_mla,
        lru_conv_w=lru_conv_w, lru_conv_b=lru_conv_b, lru_gate_w=lru_gate_w, lru_gate_b=lru_gate_b,
        lru_lambda=lru_lambda, w_o_lru=w_o_lru, diff_qn=diff_qn, diff_kn=diff_kn, diff_lambda=diff_lambda,
        diff_subln=diff_subln, w_o_diff=w_o_diff, w_merge=w_merge, b_merge=b_merge, w_out=w_out,
        router_w=router_w, router_b=router_b, exp_w1=exp_w1, exp_b1=exp_b1, exp_w2=exp_w2, exp_b2=exp_b2)
    w = _prepare(params)

    cond = jnp.concatenate(
        [c, c_ctx[None, :], jnp.zeros((COND_ROWS - DEC_BATCH - 1, D_MODEL), F32)], axis=0)
    mod = _ada_call(cond, ada_w, ada_b[:, None, :])
    mod3 = mod.reshape(DEPTH * COND_ROWS, 1, 6 * D_MODEL)

    place = jnp.pad(jnp.eye(MLA_ROPE, dtype=F32), ((0, 0), (MLA_NOPE, LANES - MLA_QK))).astype(BF16)
    k_ctx, v_ctx = _kvx_call(cache_mla_ckv, cache_mla_krope, w["wkvk"], w["wkvv"], w["kn"], place)
    dk_ctx = cache_diff_k.reshape(DEC_BATCH, DEPTH, PAST_LEN, 512)
    dv_ctx = cache_diff_v.reshape(DEC_BATCH, DEPTH, PAST_LEN, 512)

    x = jnp.concatenate([x_prompt.reshape(N_CTX_TOK, D_MODEL), x_sample.reshape(-1, D_MODEL)], axis=0)
    h0_ctx = jnp.zeros((BATCH, 2, D_RNN), F32)
    lat_blk = N_CTX_TOK // DEC_SEQ
    new_ckv, new_krope, new_lru, new_dk, new_dv = [], [], [], [], []
    moe_out = None
    for l in range(DEPTH):
        lam_init = 0.8 - 0.6 * math.exp(-0.3 * l)
        x, (qm, km, vm, qd, kd, vd, lx, lg, ckv_o, kro_o, dk_o, dv_o) = _front_call(l, x, moe_out, mod3, w)

        u_ctx, hf_ctx = _lru_call(l, lx, lg, h0_ctx, w, BATCH, SEQ, 0)
        u_lat, _ = _lru_call(l, lx, lg, state_lru[:, l], w, DEC_BATCH, DEC_SEQ, lat_blk)

        om_ctx = _attn_call(_mla_attn_kernel, "mla_ctx", qm, [(km, vm, "tok")], [], [], 0, BATCH, 1, 512)
        om_lat = _attn_call(_mla_attn_kernel, "mla_lat", qm,
                            [(k_ctx, v_ctx, lambda b: (l, b, 0, 0)), (km, vm, "tok")], [], [],
                            lat_blk, DEC_BATCH, LAT_TILES_PER_SEQ, 512)

        dk_extra = [w["diff_lambda"], w["diff_subln"]]
        dk_specs = lambda: [pl.BlockSpec((None, 4, DIFF_DH), lambda b: (l, 0, 0)),
                            pl.BlockSpec((None, 1, LANES), lambda b: (l, 0, 0))]
        dkern = lambda n_seg, *refs: _diff_attn_kernel(n_seg, lam_init, *refs)
        od_ctx = _attn_call(dkern, "diff_ctx", qd, [(kd, vd, "tok")], dk_extra, dk_specs(),
                            0, BATCH, 1, 512)
        od_lat = _attn_call(dkern, "diff_lat", qd,
                            [(dk_ctx, dv_ctx, lambda b: (b, l, 0, 0)), (kd, vd, "tok")], dk_extra, dk_specs(),
                            lat_blk, DEC_BATCH, LAT_TILES_PER_SEQ, 512)

        x, h2t, ids, sw, seg_off, n_gran = _back_call(
            l, x, mod3, (om_ctx, om_lat), (u_ctx, u_lat), (od_ctx, od_lat), w)
        moe_out = _moe_call(l, h2t, ids, sw, seg_off, n_gran, w)

        new_ckv.append(ckv_o.reshape(BATCH, SEQ, KV_LORA))
        new_krope.append(kro_o.reshape(BATCH, SEQ, MLA_ROPE))
        new_lru.append(hf_ctx)
        new_dk.append(dk_o.reshape(BATCH, SEQ, DIFF_HEADS, 2, DIFF_DH))
        new_dv.append(dv_o.reshape(BATCH, SEQ, DIFF_HEADS, DIFF_DV))

    x_ctx, x_lat = _residual_call(DEPTH - 1, x, moe_out, mod3)
    xp = x_ctx.reshape(BATCH, SEQ, D_MODEL)
    xs = x_lat.reshape(DEC_BATCH, DEC_SEQ, D_MODEL)
    return (xp, xs, jnp.stack(new_ckv, axis=1), jnp.stack(new_krope, axis=1), jnp.stack(new_lru, axis=1),
            jnp.stack(new_dk, axis=1), jnp.stack(new_dv, axis=1))
```

```python
import functools
import math

import jax
import jax.numpy as jnp
from jax import lax
from jax.experimental import pallas as pl
from jax.experimental.pallas import tpu as pltpu

F32 = jnp.float32
BF16 = jnp.bfloat16

D_MODEL = 1024
BATCH = 16
SEQ = 256
DEPTH = 4
DEC_BATCH = 8
DEC_SEQ = 1024
PAST_LEN = 256
GRID_W = 64
ROPE_BASE = 10000.0
NORM_EPS = 1e-6
LOG2_E = math.log2(math.e)

MLA_HEADS = 8
MLA_NOPE = 64
MLA_ROPE = 32
MLA_QK = MLA_NOPE + MLA_ROPE
MLA_V = 64
Q_LORA = 384
KV_LORA = 256
D_RNN = 512
LRU_BLOCKS = 8
LRU_BW = D_RNN // LRU_BLOCKS
CONV_W = 4
LRU_C = 8.0
DIFF_HEADS = 4
DIFF_DH = 64
DIFF_DV = 2 * DIFF_DH
N_EXPERTS = 32
TOP_K = 4
D_EXPERT = 512
SWIGLU_LIMIT = 7.0
SWIGLU_ALPHA = 1.702

OFF_QA = 0
OFF_KVA = OFF_QA + Q_LORA
OFF_LRU_X = OFF_KVA + KV_LORA + MLA_ROPE
OFF_LRU_G = OFF_LRU_X + D_RNN
OFF_DQ = OFF_LRU_G + D_RNN
OFF_DK = OFF_DQ + DIFF_HEADS * 2 * DIFF_DH
OFF_DV = OFF_DK + DIFF_HEADS * 2 * DIFF_DH
IN_COLS = OFF_DV + DIFF_HEADS * DIFF_DV

LANES = 128
TILE = 256
N_CTX_TILES = BATCH * SEQ // TILE
LAT_TILES_PER_SEQ = DEC_SEQ // TILE
N_LAT_TILES = DEC_BATCH * LAT_TILES_PER_SEQ
N_TILES = N_CTX_TILES + N_LAT_TILES
N_CTX_TOK = BATCH * SEQ
N_TOK = N_TILES * TILE
COND_ROWS = 16
CTX_COND_ROW = DEC_BATCH

P_QA = 0
P_CKV = P_QA + Q_LORA
P_KR = P_CKV + KV_LORA
P_LX = P_KR + LANES
P_LG = P_LX + D_RNN
P_DQ = P_LG + D_RNN
P_DK = P_DQ + 512
P_DV = P_DK + 512
P_COLS = P_DV + 512

SUB = 8
MOE_TILES = 16
MOE_TOK = MOE_TILES * TILE
N_MOE_BLOCKS = N_TOK // MOE_TOK
GRAN = 8
SLOTS = TILE * TOP_K + N_EXPERTS * GRAN
MOE_CHUNK = 256
CHUNK_GRAN = MOE_CHUNK // GRAN
CHUNK_SHIFT = CHUNK_GRAN.bit_length() - 1
MAX_GRAN = MOE_TILES * SLOTS // GRAN
VMEM_LIMIT = 56 * 1024 * 1024


def _cond_row(i):
    return jnp.where(i < N_CTX_TILES, CTX_COND_ROW, (i - N_CTX_TILES) // LAT_TILES_PER_SEQ)


def _rope_idx(i):
    return jnp.where(i < N_CTX_TILES, 0, 1 + (i - N_CTX_TILES) % LAT_TILES_PER_SEQ)


def _ctx_out_idx(i):
    return jnp.minimum(i, N_CTX_TILES - 1)


def _const_spec(shape, index):
    return pl.BlockSpec(shape, lambda *_: index, pipeline_mode=pl.Buffered(1))


def _dot(a, b):
    return jnp.dot(a, b, preferred_element_type=F32)


def _dot_nt(a, b):
    return lax.dot_general(a, b, (((1,), (1,)), ((), ())), preferred_element_type=F32)


def _rms(x, n=None):
    n = x.shape[-1] if n is None else n
    ss = jnp.sum(x * x, axis=-1, keepdims=True)
    return x * lax.rsqrt(ss * (1.0 / n) + NORM_EPS)


def _rope_pair(ya, yb, cos, sin_signed, swap2):
    rot = _dot(jnp.concatenate([ya, yb], axis=-1).astype(BF16), swap2)
    return ya * cos + rot[:, :LANES] * sin_signed, yb * cos + rot[:, LANES:] * sin_signed


def _rms_factor(x, n):
    return lax.rsqrt(jnp.sum(x * x, axis=-1, keepdims=True) * (1.0 / n) + NORM_EPS)


def _sigmoid(x):
    return 0.5 * jnp.tanh(0.5 * x) + 0.5


def _lane_iota(shape):
    return lax.broadcasted_iota(jnp.int32, shape, len(shape) - 1)


def _ada_kernel(cond_ref, w_ref, b_ref, o_ref):
    c = cond_ref[...]
    s = c * jax.nn.sigmoid(c)
    o_ref[...] = _dot(s.astype(BF16), w_ref[...].astype(BF16)) + b_ref[...]


def _ada_call(cond, ada_w, ada_b3):
    cb = 1024
    return pl.pallas_call(
        _ada_kernel,
        grid=(DEPTH, 6 * D_MODEL // cb),
        in_specs=[
            pl.BlockSpec((COND_ROWS, D_MODEL), lambda l, j: (0, 0)),
            pl.BlockSpec((None, D_MODEL, cb), lambda l, j: (l, 0, j)),
            pl.BlockSpec((None, 1, cb), lambda l, j: (l, 0, j)),
        ],
        out_specs=pl.BlockSpec((None, COND_ROWS, cb), lambda l, j: (l, 0, j)),
        out_shape=jax.ShapeDtypeStruct((DEPTH, COND_ROWS, 6 * D_MODEL), F32),
        compiler_params=pltpu.CompilerParams(
            dimension_semantics=("arbitrary", "arbitrary"), vmem_limit_bytes=VMEM_LIMIT),
        name="ada_mod",
    )(cond, ada_w, ada_b3)


def _mla_k_heads(kn, krs, kn_gain, cos, sin, swap, store):
    for hp in range(MLA_HEADS // 2):
        heads = (2 * hp, 2 * hp + 1)
        raw = [kn[:, hd * LANES:(hd + 1) * LANES] + krs for hd in heads]
        ys = [kh * kn_gain for kh in raw]
        if cos is not None:
            ys = _rope_pair(ys[0], ys[1], cos, sin, swap)
        for hd, kh, y in zip(heads, raw, ys):
            store(hd, y * _rms_factor(kh, MLA_QK))


def _kvx_kernel(ckv_ref, kr_ref, wk_ref, wv_ref, kn_ref, place_ref, k_ref, v_ref):
    c = ckv_ref[...].astype(BF16)
    kn = _dot(c, wk_ref[...])
    v_ref[...] = _dot(c, wv_ref[...]).astype(BF16)
    krs = _dot(kr_ref[...].astype(BF16), place_ref[...])

    def store(hd, kh):
        k_ref[:, hd * LANES:(hd + 1) * LANES] = kh.astype(BF16)

    _mla_k_heads(kn, krs, kn_ref[...], None, None, None, store)


def _kvx_call(cache_ckv, cache_krope, wkvk, wkvv, kn_pad, place):
    return pl.pallas_call(
        _kvx_kernel,
        grid=(DEPTH, DEC_BATCH),
        in_specs=[
            pl.BlockSpec((None, None, PAST_LEN, KV_LORA), lambda l, b: (b, l, 0, 0)),
            pl.BlockSpec((None, None, PAST_LEN, MLA_ROPE), lambda l, b: (b, l, 0, 0)),
            pl.BlockSpec((None, KV_LORA, MLA_HEADS * LANES), lambda l, b: (l, 0, 0)),
            pl.BlockSpec((None, KV_LORA, MLA_HEADS * MLA_V), lambda l, b: (l, 0, 0)),
            pl.BlockSpec((None, 1, LANES), lambda l, b: (l, 0, 0)),
            pl.BlockSpec((MLA_ROPE, LANES), lambda l, b: (0, 0)),
        ],
        out_specs=[
            pl.BlockSpec((None, None, PAST_LEN, MLA_HEADS * LANES), lambda l, b: (l, b, 0, 0)),
            pl.BlockSpec((None, None, PAST_LEN, MLA_HEADS * MLA_V), lambda l, b: (l, b, 0, 0)),
        ],
        out_shape=[
            jax.ShapeDtypeStruct((DEPTH, DEC_BATCH, PAST_LEN, MLA_HEADS * LANES), BF16),
            jax.ShapeDtypeStruct((DEPTH, DEC_BATCH, PAST_LEN, MLA_HEADS * MLA_V), BF16),
        ],
        compiler_params=pltpu.CompilerParams(
            dimension_semantics=("arbitrary", "arbitrary"), vmem_limit_bytes=VMEM_LIMIT),
        name="ctx_kv_expand",
    )(cache_ckv, cache_krope, wkvk, wkvv, kn_pad, place)


def _modulated(x, gain, shift, scale):
    return _rms(x) * gain * (1.0 + scale) + shift


def _untile(t_ref):
    rows = t_ref.shape[0] // SUB
    return jnp.concatenate([t_ref[pl.ds(s, rows, stride=SUB), :] for s in range(SUB)], axis=-1)


def _moe_residual(x1_ref, yt_ref, modp_ref):
    return x1_ref[...] + modp_ref[:, 5 * D_MODEL:6 * D_MODEL] * _untile(yt_ref)


def _front_kernel(has_moe, *refs):
    if has_moe:
        x1_ref, yt_ref, modp_ref = refs[:3]
        refs = refs[3:]
    else:
        x_ref = refs[0]
        refs = refs[1:]
    (mod_ref, n1_ref, win_ref, qan_ref, wqb_ref, kvan_ref, wkvk_ref, wkvv_ref,
     qn_ref, kn_ref, dqn_ref, dkn_ref, cosm_ref, sinm_ref, cosd_ref, sind_ref, swapm_ref, swapd_ref) = refs[:18]
    refs = refs[18:]
    if has_moe:
        xo_ref = refs[0]
        refs = refs[1:]
    (qm_ref, km_ref, vm_ref, qd_ref, kd_ref, vd_ref, lx_ref, lg_ref,
     ckv_ref, kro_ref, dko_ref, dvo_ref) = refs
    if has_moe:
        x = _moe_residual(x1_ref, yt_ref, modp_ref)
        xo_ref[...] = x
    else:
        x = x_ref[...]
    mod = mod_ref[...]
    h = _modulated(x, n1_ref[...], mod[:, 0:D_MODEL], mod[:, D_MODEL:2 * D_MODEL])
    hb = h.astype(BF16)

    lane = _lane_iota((1, LANES))
    cosm, sinm, swap_m = cosm_ref[...], sinm_ref[...], swapm_ref[...]
    cosd, sind, swap_d = cosd_ref[...], sind_ref[...], swapd_ref[...]

    qa = _rms(_dot(hb, win_ref[:, P_QA:P_QA + Q_LORA])) * qan_ref[...]
    q = _dot(qa.astype(BF16), wqb_ref[...])
    q_scale = MLA_QK ** -0.5 * LOG2_E
    for hp in range(MLA_HEADS // 2):
        heads = (2 * hp, 2 * hp + 1)
        raw = [q[:, hd * LANES:(hd + 1) * LANES] for hd in heads]
        ys = _rope_pair(raw[0] * qn_ref[...], raw[1] * qn_ref[...], cosm, sinm, swap_m)
        for hd, qh, y in zip(heads, raw, ys):
            qm_ref[:, hd * LANES:(hd + 1) * LANES] = (y * (_rms_factor(qh, MLA_QK) * q_scale)).astype(BF16)

    ckv = _rms(_dot(hb, win_ref[:, P_CKV:P_CKV + KV_LORA])) * kvan_ref[...]
    krs = _dot(hb, win_ref[:, P_KR:P_KR + LANES])
    cb = ckv.astype(BF16)
    vm_ref[...] = _dot(cb, wkvv_ref[...]).astype(BF16)
    kn = _dot(cb, wkvk_ref[...])

    def store_k(hd, kh):
        km_ref[:, hd * LANES:(hd + 1) * LANES] = kh.astype(BF16)

    _mla_k_heads(kn, krs, kn_ref[...], cosm, sinm, swap_m, store_k)

    lx_ref[...] = _dot(hb, win_ref[:, P_LX:P_LX + D_RNN])
    lg_ref[...] = _dot(hb, win_ref[:, P_LG:P_LG + D_RNN])

    low = lane < DIFF_DH

    def pair_factor(t):
        sq = t * t
        s_all = jnp.sum(sq, axis=-1, keepdims=True)
        s_lo = jnp.sum(jnp.where(low, sq, 0.0), axis=-1, keepdims=True)
        r_lo = lax.rsqrt(s_lo * (1.0 / DIFF_DH) + NORM_EPS)
        r_hi = lax.rsqrt((s_all - s_lo) * (1.0 / DIFF_DH) + NORM_EPS)
        return jnp.where(low, r_lo, r_hi)

    dq = _dot(hb, win_ref[:, P_DQ:P_DQ + 512])
    dk = _dot(hb, win_ref[:, P_DK:P_DK + 512])
    d_scale = DIFF_DH ** -0.5 * LOG2_E
    k_heads = []
    for hp in range(DIFF_HEADS // 2):
        sls = [slice(hd * LANES, (hd + 1) * LANES) for hd in (2 * hp, 2 * hp + 1)]
        qs = _rope_pair(dq[:, sls[0]] * dqn_ref[...], dq[:, sls[1]] * dqn_ref[...], cosd, sind, swap_d)
        ks = _rope_pair(dk[:, sls[0]] * dkn_ref[...], dk[:, sls[1]] * dkn_ref[...], cosd, sind, swap_d)
        for sl, qh, kh in zip(sls, qs, ks):
            qd_ref[:, sl] = (qh * (pair_factor(dq[:, sl]) * d_scale)).astype(BF16)
            kh = kh * pair_factor(dk[:, sl])
            kd_ref[:, sl] = kh.astype(BF16)
            k_heads.append(kh)
    dv = _dot(hb, win_ref[:, P_DV:P_DV + 512])
    vd_ref[...] = dv.astype(BF16)

    @pl.when(pl.program_id(0) < N_CTX_TILES)
    def _():
        ckv_ref[...] = ckv
        kro_ref[...] = krs[:, MLA_NOPE:MLA_NOPE + MLA_ROPE]
        for hd in range(DIFF_HEADS):
            dko_ref[:, hd * LANES:(hd + 1) * LANES] = k_heads[hd]
        dvo_ref[...] = dv


def _mod_spec(l):
    return pl.BlockSpec((None, 1, 6 * D_MODEL), lambda i: (l * COND_ROWS + _cond_row(i), 0, 0))


def _tile_spec():
    return pl.BlockSpec((TILE * SUB, LANES), lambda i: (i, 0))


def _front_call(l, x, moe_out, mod3, w):
    tok = lambda width: pl.BlockSpec((TILE, width), lambda i: (i, 0))
    ctx = lambda width: pl.BlockSpec((TILE, width), lambda i: (_ctx_out_idx(i), 0))
    lay = lambda *shape: _const_spec((None,) + shape, (l,) + (0,) * len(shape))
    rope = pl.BlockSpec((None, TILE, LANES), lambda i: (_rope_idx(i), 0, 0))
    swap = pl.BlockSpec((2 * LANES, 2 * LANES), lambda i: (0, 0))
    n_ctx_rows = N_CTX_TOK
    has_moe = moe_out is not None
    if has_moe:
        lead_specs = [tok(D_MODEL), _tile_spec(), _mod_spec(l - 1)]
        lead_args = [x, moe_out, mod3]
        x_out_specs = [tok(D_MODEL)]
        x_out_shape = [jax.ShapeDtypeStruct((N_TOK, D_MODEL), F32)]
    else:
        lead_specs, lead_args, x_out_specs, x_out_shape = [tok(D_MODEL)], [x], [], []
    outs = pl.pallas_call(
        functools.partial(_front_kernel, has_moe),
        grid=(N_TILES,),
        in_specs=lead_specs + [
            _mod_spec(l),
            lay(1, D_MODEL), lay(D_MODEL, P_COLS), lay(1, Q_LORA), lay(Q_LORA, MLA_HEADS * LANES),
            lay(1, KV_LORA), lay(KV_LORA, MLA_HEADS * LANES), lay(KV_LORA, MLA_HEADS * MLA_V),
            lay(1, LANES), lay(1, LANES), lay(1, LANES), lay(1, LANES),
            rope, rope, rope, rope, swap, swap,
        ],
        out_specs=x_out_specs + [
            tok(1024), tok(1024), tok(512), tok(512), tok(512), tok(512), tok(512), tok(512),
            ctx(KV_LORA), ctx(MLA_ROPE), ctx(512), ctx(512),
        ],
        out_shape=x_out_shape + [
            jax.ShapeDtypeStruct((N_TOK, 1024), BF16), jax.ShapeDtypeStruct((N_TOK, 1024), BF16),
            jax.ShapeDtypeStruct((N_TOK, 512), BF16), jax.ShapeDtypeStruct((N_TOK, 512), BF16),
            jax.ShapeDtypeStruct((N_TOK, 512), BF16), jax.ShapeDtypeStruct((N_TOK, 512), BF16),
            jax.ShapeDtypeStruct((N_TOK, 512), F32), jax.ShapeDtypeStruct((N_TOK, 512), F32),
            jax.ShapeDtypeStruct((n_ctx_rows, KV_LORA), F32),
            jax.ShapeDtypeStruct((n_ctx_rows, MLA_ROPE), F32),
            jax.ShapeDtypeStruct((n_ctx_rows, 512), F32),
            jax.ShapeDtypeStruct((n_ctx_rows, 512), F32),
        ],
        compiler_params=pltpu.CompilerParams(
            dimension_semantics=("arbitrary",), vmem_limit_bytes=VMEM_LIMIT),
        name="front",
    )(*lead_args, mod3, w["norm1"], w["w_in"], w["qa_norm"], w["wqb"], w["kva_norm"], w["wkvk"], w["wkvv"],
      w["qn"], w["kn"], w["dqn"], w["dkn"], w["cos_m"], w["sin_m"], w["cos_d"], w["sin_d"],
      w["swap_m"], w["swap_d"])
    if has_moe:
        return outs[0], outs[1:]
    return x, outs


def _residual_kernel(x1_ref, yt_ref, modp_ref, oc_ref, ol_ref):
    x = _moe_residual(x1_ref, yt_ref, modp_ref)
    is_ctx = pl.program_id(0) < N_CTX_TILES

    @pl.when(is_ctx)
    def _():
        oc_ref[...] = x

    @pl.when(jnp.logical_not(is_ctx))
    def _():
        ol_ref[...] = x


def _residual_call(l, x1, moe_out, mod3):
    tok = pl.BlockSpec((TILE, D_MODEL), lambda i: (i, 0))
    return pl.pallas_call(
        _residual_kernel,
        grid=(N_TILES,),
        in_specs=[tok, _tile_spec(), _mod_spec(l)],
        out_specs=[
            pl.BlockSpec((TILE, D_MODEL), lambda i: (jnp.minimum(i, N_CTX_TILES - 1), 0)),
            pl.BlockSpec((TILE, D_MODEL), lambda i: (jnp.maximum(i - N_CTX_TILES, 0), 0)),
        ],
        out_shape=[
            jax.ShapeDtypeStruct((N_CTX_TOK, D_MODEL), F32),
            jax.ShapeDtypeStruct((N_TOK - N_CTX_TOK, D_MODEL), F32),
        ],
        compiler_params=pltpu.CompilerParams(
            dimension_semantics=("arbitrary",), vmem_limit_bytes=VMEM_LIMIT),
        name="moe_residual",
    )(x1, moe_out, mod3)


def _scan(a, b, h0, reverse):
    n = a.shape[0]
    row = lax.broadcasted_iota(jnp.int32, (n, 1), 0)
    d = 1
    while d < n:
        if reverse:
            valid = row < n - d
            shift = n - d
        else:
            valid = row >= d
            shift = d
        a_s = jnp.where(valid, pltpu.roll(a, shift, 0), 1.0)
        b_s = jnp.where(valid, pltpu.roll(b, shift, 0), 0.0)
        b = a * b_s + b
        a = a * a_s
        d *= 2
    return a * h0 + b


def _gelu_tanh(x):
    return 0.5 * x * (1.0 + jnp.tanh(math.sqrt(2.0 / math.pi) * (x + 0.044715 * x * x * x)))


N_LANE_TILES = D_RNN // LANES


def _strided_scan(a, b, h0, reverse, first, sa_ref, sb_ref, tot_ref):
    n = a.shape[0]
    g = n // SUB
    for c in range(N_LANE_TILES):
        sa_ref[pl.ds(c * n, n), :] = a[:, c * LANES:(c + 1) * LANES]
        sb_ref[pl.ds(c * n, n), :] = b[:, c * LANES:(c + 1) * LANES]
    order = range(SUB - 1, -1, -1) if reverse else range(SUB)
    row = lax.broadcasted_iota(jnp.int32, (g, 1), 0)
    finals = []
    for c in range(N_LANE_TILES):
        comps = []
        acc_a = acc_b = None
        for r in order:
            cls = pl.ds(c * n + r, g, stride=SUB)
            a_r, b_r = sa_ref[cls, :], sb_ref[cls, :]
            if acc_a is None:
                acc_a, acc_b = a_r, b_r
            else:
                acc_b = a_r * acc_b + b_r
                acc_a = a_r * acc_a
            comps.append((cls, acc_a, acc_b))
        h0c = h0[:, c * LANES:(c + 1) * LANES]
        hg = _scan(acc_a, acc_b, h0c, reverse)
        if reverse:
            carry = jnp.where(row == g - 1, h0c, pltpu.roll(hg, g - 1, 0))
            finals.append(hg[0:1, :])
        else:
            carry = jnp.where(row == 0, h0c, pltpu.roll(hg, 1, 0))
            finals.append(hg[g - 1:g, :])
        for cls, comp_a, comp_b in comps:
            h = comp_a * carry + comp_b
            tot_ref[cls, :] = h if first else tot_ref[cls, :] + h
    return jnp.concatenate(finals, axis=-1)


def _sqrt_pos(t):
    return jnp.where(t > 0.0, t * lax.rsqrt(t), 0.0)


def _lru_kernel(lx_ref, lg_ref, cw_ref, cb_ref, wbd_ref, gb_ref, lam_ref, h0_ref, u_ref, hf_ref,
                sa_ref, sb_ref, tot_ref):
    x = lx_ref[...]
    n = x.shape[0]
    row = lax.broadcasted_iota(jnp.int32, (n, 1), 0)
    cw = cw_ref[...]
    xr = cb_ref[...] + cw[2:3] * x
    xr = xr + cw[0:1] * jnp.where(row >= 2, pltpu.roll(x, 2, 0), 0.0)
    xr = xr + cw[1:2] * jnp.where(row >= 1, pltpu.roll(x, 1, 0), 0.0)
    xr = xr + cw[3:4] * jnp.where(row < n - 1, pltpu.roll(x, n - 1, 0), 0.0)
    xb = xr.astype(BF16)
    half = D_RNN // 2
    lam = lam_ref[...]
    h0 = h0_ref[...]
    gb = gb_ref[...]
    for d in range(2):
        pre = []
        for g in range(2):
            k = d * 2 + g
            p = jnp.concatenate([_dot(xb[:, :half], wbd_ref[k, 0]), _dot(xb[:, half:], wbd_ref[k, 1])],
                                axis=-1)
            pre.append(p + gb[k:k + 1])
        r = _sigmoid(pre[0])
        i = _sigmoid(pre[1])
        z = -lam[d:d + 1]
        softplus = jnp.maximum(z, 0.0) + jnp.log(1.0 + jnp.exp(-jnp.abs(z)))
        a = jnp.exp(-LRU_C * r * softplus)
        bx = _sqrt_pos(1.0 - a * a) * (i * xr)
        hf_ref[d:d + 1, :] = _strided_scan(a, bx, h0[d:d + 1], d == 1, d == 0, sa_ref, sb_ref, tot_ref)
    total = jnp.concatenate([tot_ref[pl.ds(c * n, n), :] for c in range(N_LANE_TILES)], axis=-1)
    u_ref[...] = (total * _gelu_tanh(lg_ref[...])).astype(BF16)


def _lru_call(l, lx, lg, h0, w, n_seq, seq_len, row_block0):
    seq = lambda: pl.BlockSpec((seq_len, D_RNN), lambda s: (row_block0 + s, 0))
    lay = lambda *shape: _const_spec((None,) + shape, (l,) + (0,) * len(shape))
    return pl.pallas_call(
        _lru_kernel,
        grid=(n_seq,),
        in_specs=[
            seq(), seq(), lay(CONV_W, D_RNN), lay(1, D_RNN), lay(4, 2, D_RNN // 2, D_RNN // 2),
            lay(4, D_RNN), lay(2, D_RNN),
            pl.BlockSpec((None, 2, D_RNN), lambda s: (s, 0, 0)),
        ],
        out_specs=[
            pl.BlockSpec((seq_len, D_RNN), lambda s: (s, 0)),
            pl.BlockSpec((None, 2, D_RNN), lambda s: (s, 0, 0)),
        ],
        out_shape=[
            jax.ShapeDtypeStruct((n_seq * seq_len, D_RNN), BF16),
            jax.ShapeDtypeStruct((n_seq, 2, D_RNN), F32),
        ],
        scratch_shapes=[pltpu.VMEM((N_LANE_TILES * seq_len, LANES), F32)] * 3,
        compiler_params=pltpu.CompilerParams(
            dimension_semantics=("arbitrary",), vmem_limit_bytes=VMEM_LIMIT),
        name="rglru_%d" % seq_len,
    )(lx, lg, w["conv_w"], w["conv_b"], w["wbd"], w["gate_b"], w["lam"], h0)


def _softmax_parts(scores):
    m = scores[0].max(axis=-1, keepdims=True)
    for s in scores[1:]:
        m = jnp.maximum(m, s.max(axis=-1, keepdims=True))
    es = [jnp.exp2(s - m) for s in scores]
    tot = es[0].sum(axis=-1, keepdims=True)
    for e in es[1:]:
        tot = tot + e.sum(axis=-1, keepdims=True)
    return es, tot


def _mla_attn_kernel(n_seg, q_ref, *refs):
    k_refs = refs[0:2 * n_seg:2]
    v_refs = refs[1:2 * n_seg:2]
    o_ref = refs[2 * n_seg]
    low = _lane_iota((1, LANES)) < MLA_V
    for pair in range(MLA_HEADS // 2):
        vs = slice(pair * LANES, (pair + 1) * LANES)
        out = None
        for hh in range(2):
            hs = slice((2 * pair + hh) * LANES, (2 * pair + hh + 1) * LANES)
            qh = q_ref[:, hs]
            es, tot = _softmax_parts([_dot_nt(qh, k[:, hs]) for k in k_refs])
            pv = None
            for e, v in zip(es, v_refs):
                t = _dot(e.astype(BF16), v[:, vs])
                pv = t if pv is None else pv + t
            oh = pv / tot
            out = oh if hh == 0 else jnp.where(low, out, oh)
        o_ref[:, vs] = out.astype(BF16)


def _diff_attn_kernel(n_seg, lam_init, q_ref, *refs):
    k_refs = refs[0:2 * n_seg:2]
    v_refs = refs[1:2 * n_seg:2]
    lam_ref, sub_ref, o_ref = refs[2 * n_seg:2 * n_seg + 3]
    lp = lam_ref[...]
    lam = (jnp.exp(jnp.sum(lp[0:1] * lp[1:2], axis=-1, keepdims=True))
           - jnp.exp(jnp.sum(lp[2:3] * lp[3:4], axis=-1, keepdims=True)) + lam_init)
    low = _lane_iota((1, LANES)) < DIFF_DH
    zero = jnp.zeros((), BF16)
    for hd in range(DIFF_HEADS):
        hs = slice(hd * LANES, (hd + 1) * LANES)
        qh = q_ref[:, hs]
        ks = [k[:, hs].astype(BF16) for k in k_refs]
        e0, t0 = _softmax_parts([_dot_nt(jnp.where(low, qh, zero), k) for k in ks])
        e1, t1 = _softmax_parts([_dot_nt(jnp.where(low, zero, qh), k) for k in ks])
        w0 = 1.0 / t0
        w1 = lam / t1
        o = None
        for a, b, v in zip(e0, e1, v_refs):
            t = _dot((a * w0 - b * w1).astype(BF16), v[:, hs].astype(BF16))
            o = t if o is None else o + t
        o = _rms(o) * sub_ref[...] * (1.0 - lam_init)
        o_ref[:, hs] = o.astype(BF16)


def _over_q_tiles(kernel, n_seg, tiles, q_ref, *refs):
    if tiles == 1:
        kernel(n_seg, q_ref, *refs)
        return

    def tile(t, carry):
        rows = pl.ds(pl.multiple_of(t * TILE, TILE), TILE)
        kernel(n_seg, q_ref.at[rows], *refs[:-1], refs[-1].at[rows])
        return carry

    lax.fori_loop(0, tiles, tile, 0)


def _attn_call(kernel, name, q, segs, extra, extra_specs, q_block0, n_seq, tiles_per_seq, out_width):
    seq_len = tiles_per_seq * TILE
    in_specs = [pl.BlockSpec((seq_len, q.shape[1]), lambda b: (q_block0 + b, 0))]
    args = [q]
    for k, v, kind in segs:
        for arr in (k, v):
            if kind == "tok":
                in_specs.append(pl.BlockSpec((seq_len, arr.shape[1]), lambda b: (q_block0 + b, 0)))
            else:
                in_specs.append(pl.BlockSpec((None, None) + arr.shape[2:], kind))
            args.append(arr)
    in_specs += extra_specs
    args += extra
    return pl.pallas_call(
        functools.partial(_over_q_tiles, kernel, len(segs), tiles_per_seq),
        grid=(n_seq,),
        in_specs=in_specs,
        out_specs=pl.BlockSpec((seq_len, out_width), lambda b: (b, 0)),
        out_shape=jax.ShapeDtypeStruct((n_seq * seq_len, out_width), BF16),
        compiler_params=pltpu.CompilerParams(
            dimension_semantics=("arbitrary",), vmem_limit_bytes=VMEM_LIMIT),
        name=name,
    )(*args)


def _back_kernel(x_ref, mod_ref, n1_ref, n2_ref, omc_ref, oml_ref, ulc_ref, ull_ref, odc_ref, odl_ref,
                 wom_ref, wol_ref, wod_ref, wmg_ref, bmg_ref, wout_ref, rw_ref, rb_ref,
                 x1_ref, h2t_ref, ids_ref, sw_ref, off_ref, ng_ref):
    i = pl.program_id(0)
    is_ctx = i < N_CTX_TILES
    x = x_ref[...]
    mod = mod_ref[...]
    seg = lambda j: mod[:, j * D_MODEL:(j + 1) * D_MODEL]
    hb = _modulated(x, n1_ref[...], seg(0), seg(1)).astype(BF16)
    pick = lambda c_ref, l_ref: jnp.where(is_ctx, c_ref[...], l_ref[...])
    branches = (_dot(pick(omc_ref, oml_ref), wom_ref[...]), _dot(pick(ulc_ref, ull_ref), wol_ref[...]),
                _dot(pick(odc_ref, odl_ref), wod_ref[...]))
    merged = None
    for j, o in enumerate(branches):
        cs = slice(j * D_MODEL, (j + 1) * D_MODEL)
        g = _sigmoid(_dot(hb, wmg_ref[:, cs]) + bmg_ref[:, cs])
        merged = g * o if merged is None else merged + g * o
    x1 = x + seg(2) * _dot(merged.astype(BF16), wout_ref[...])
    x1_ref[...] = x1
    h2 = _modulated(x1, n2_ref[...], seg(3), seg(4))
    for s in range(SUB):
        h2t_ref[pl.ds(s, TILE, stride=SUB), :] = h2[:, s * LANES:(s + 1) * LANES]

    h_hi = h2.astype(BF16)
    h_lo = (h2 - h_hi.astype(F32)).astype(BF16)
    rw = rw_ref[...]
    w_hi = rw.astype(BF16)
    w_lo = (rw - w_hi.astype(F32)).astype(BF16)
    logits = _dot(h_hi, w_hi) + (_dot(h_lo, w_hi) + _dot(h_hi, w_lo)) + rb_ref[...]
    lane = _lane_iota(logits.shape).astype(F32)
    neg = jnp.float32(-jnp.inf)
    work = jnp.where(lane < N_EXPERTS, logits, neg)
    sels, probs = [], []
    top = None
    for _ in range(TOP_K):
        m = work.max(axis=-1, keepdims=True)
        idx = jnp.min(jnp.where(work == m, lane, float(LANES)), axis=-1, keepdims=True)
        sel = lane == idx
        top = m if top is None else top
        sels.append(sel)
        probs.append(jnp.exp(m - top))
        work = jnp.where(sel, neg, work)
    denom = probs[0] + probs[1] + probs[2] + probs[3]

    onehot = jnp.zeros_like(logits)
    for sel in sels:
        onehot = onehot + jnp.where(sel, 1.0, 0.0)
    r_i = lax.broadcasted_iota(jnp.int32, (TILE, TILE), 0)
    c_i = lax.broadcasted_iota(jnp.int32, (TILE, TILE), 1)
    earlier = jnp.where(c_i < r_i, 1.0, 0.0).astype(BF16)
    rank = _dot(earlier, onehot.astype(BF16))
    count = jnp.sum(onehot, axis=0, keepdims=True)
    n_gran = jnp.floor((count + (GRAN - 1.0)) * (1.0 / GRAN))
    r_l = lax.broadcasted_iota(jnp.int32, (LANES, LANES), 0)
    c_l = lax.broadcasted_iota(jnp.int32, (LANES, LANES), 1)
    before = jnp.where(r_l < c_l, 1.0, 0.0).astype(BF16)
    seg_off = _dot(jnp.broadcast_to(n_gran * GRAN, (SUB, LANES)).astype(BF16), before)[0:1]
    slot_base = seg_off + rank
    slot_lane = lax.broadcasted_iota(jnp.int32, (1, SLOTS), 1).astype(F32)
    tok_col = (lax.broadcasted_iota(jnp.int32, (TILE, 1), 0) + (i % MOE_TILES) * TILE).astype(F32)
    id_acc = jnp.zeros((TILE, SLOTS), F32)
    w_acc = jnp.zeros((TILE, SLOTS), F32)
    for sel, p in zip(sels, probs):
        slot = jnp.sum(jnp.where(sel, slot_base, 0.0), axis=-1, keepdims=True)
        hit = slot_lane == slot
        id_acc = jnp.where(hit, tok_col, id_acc)
        w_acc = jnp.where(hit, p / denom, w_acc)
    ids_ref[...] = jnp.sum(id_acc, axis=0, keepdims=True).astype(jnp.int32)
    sw_ref[...] = jnp.sum(w_acc, axis=0, keepdims=True)
    off_ref[...] = seg_off.astype(jnp.int32)
    ng_ref[...] = n_gran.astype(jnp.int32)


def _back_call(l, x, mod3, om, ul, od, w):
    tok = lambda width: pl.BlockSpec((TILE, width), lambda i: (i, 0))
    ctx_in = pl.BlockSpec((TILE, 512), lambda i: (jnp.minimum(i, N_CTX_TILES - 1), 0))
    lat_in = pl.BlockSpec((TILE, 512), lambda i: (jnp.maximum(i - N_CTX_TILES, 0), 0))
    lay = lambda *shape: _const_spec((None,) + shape, (l,) + (0,) * len(shape))
    row = lambda width: pl.BlockSpec((None, 1, width), lambda i: (i, 0, 0))
    return pl.pallas_call(
        _back_kernel,
        grid=(N_TILES,),
        in_specs=[
            tok(D_MODEL), _mod_spec(l),
            lay(1, D_MODEL), lay(1, D_MODEL), ctx_in, lat_in, ctx_in, lat_in, ctx_in, lat_in,
            lay(512, D_MODEL), lay(512, D_MODEL), lay(512, D_MODEL),
            lay(D_MODEL, 3 * D_MODEL), lay(1, 3 * D_MODEL), lay(D_MODEL, D_MODEL),
            lay(D_MODEL, LANES), lay(1, LANES),
        ],
        out_specs=[tok(D_MODEL), _tile_spec(), row(SLOTS), row(SLOTS), row(LANES), row(LANES)],
        out_shape=[
            jax.ShapeDtypeStruct((N_TOK, D_MODEL), F32),
            jax.ShapeDtypeStruct((N_TOK * SUB, LANES), F32),
            jax.ShapeDtypeStruct((N_TILES, 1, SLOTS), jnp.int32),
            jax.ShapeDtypeStruct((N_TILES, 1, SLOTS), F32),
            jax.ShapeDtypeStruct((N_TILES, 1, LANES), jnp.int32),
            jax.ShapeDtypeStruct((N_TILES, 1, LANES), jnp.int32),
        ],
        compiler_params=pltpu.CompilerParams(
            dimension_semantics=("arbitrary",), vmem_limit_bytes=VMEM_LIMIT),
        name="back",
    )(x, mod3, w["norm1"], w["norm2"], om[0], om[1], ul[0], ul[1], od[0], od[1],
      w["w_o_mla"], w["w_o_lru"], w["w_o_diff"],
      w["w_merge"], w["b_merge"], w["w_out"], w["router_w"], w["router_b"])


def _moe_kernel(ids_ref, sw_ref, off_ref, ng_ref, h2t_ref, w1_ref, b1_ref, w2_ref, b2_ref,
                y_ref, xga_ref, xgb_ref, za_ref, zb_ref, gb_ref):
    e = pl.program_id(1)

    @pl.when(e == 0)
    def _():
        y_ref[...] = jnp.zeros_like(y_ref)
        for ref in (xga_ref, xgb_ref, za_ref, zb_ref):
            ref[...] = jnp.zeros_like(ref)

    n_total = 0
    for t in range(MOE_TILES):
        first = t * SLOTS + off_ref[t * LANES + e]

        def add_gran(g, pos, first=first):
            gb_ref[pos] = first + g * GRAN
            return pos + 1

        n_total = lax.fori_loop(0, ng_ref[t * LANES + e], add_gran, n_total)
    for i in range(2 * CHUNK_GRAN):
        gb_ref[n_total + i] = SLOTS - GRAN
    n_chunks = lax.shift_right_logical(n_total + (CHUNK_GRAN - 1), CHUNK_SHIFT)

    def slot_rows(j):
        return pl.ds(j * SUB, SUB)

    def gather(t, xg_ref):
        for g in range(CHUNK_GRAN):
            base = gb_ref[t * CHUNK_GRAN + g]
            for u in range(GRAN):
                xg_ref[slot_rows(g * GRAN + u), :] = h2t_ref[ids_ref[base + u]]

    def expert(xg_ref, z_ref):
        xb = jnp.concatenate([xg_ref[pl.ds(s, MOE_CHUNK, stride=SUB), :] for s in range(SUB)],
                             axis=-1).astype(BF16)
        gu = _dot(xb, w1_ref[...].astype(BF16)) + b1_ref[...]
        gate = jnp.minimum(gu[:, :D_EXPERT], SWIGLU_LIMIT)
        up = jnp.clip(gu[:, D_EXPERT:], -SWIGLU_LIMIT, SWIGLU_LIMIT)
        act = (up + 1.0) * (gate * _sigmoid(SWIGLU_ALPHA * gate))
        y = _dot(act.astype(BF16), w2_ref[...].astype(BF16)) + b2_ref[...]
        for s in range(SUB):
            z_ref[pl.ds(s, MOE_CHUNK, stride=SUB), :] = y[:, s * LANES:(s + 1) * LANES]

    def combine(t, z_ref):
        for g in range(CHUNK_GRAN):
            base = gb_ref[t * CHUNK_GRAN + g]
            toks = [ids_ref[base + u] for u in range(GRAN)]
            new = [y_ref[toks[u]] + sw_ref[base + u] * z_ref[slot_rows(g * GRAN + u), :] for u in range(GRAN)]
            for u in reversed(range(GRAN)):
                y_ref[toks[u]] = new[u]

    gather(0, xga_ref)

    @pl.when(n_chunks > 0)
    def _():
        expert(xga_ref, za_ref)
        gather(1, xgb_ref)

    def tick_pair(p, carry):
        t = 2 * p + 1
        expert(xgb_ref, zb_ref)
        gather(t + 1, xga_ref)
        combine(t - 1, za_ref)

        @pl.when(t + 1 < n_chunks)
        def _():
            expert(xga_ref, za_ref)
            gather(t + 2, xgb_ref)
            combine(t, zb_ref)

        return carry

    lax.fori_loop(0, lax.shift_right_logical(n_chunks, 1), tick_pair, 0)

    @pl.when((n_chunks & 1) == 1)
    def _():
        combine(n_chunks - 1, za_ref)

    @pl.when(jnp.logical_and(n_chunks > 0, (n_chunks & 1) == 0))
    def _():
        combine(n_chunks - 1, zb_ref)


def _moe_call(l, h2t, ids, sw, seg_off, n_gran, w):
    smem = lambda n: pl.BlockSpec((MOE_TILES * n,), lambda j, e: (j,), memory_space=pltpu.SMEM,
                                  pipeline_mode=pl.Buffered(1))
    blk = pl.BlockSpec((MOE_TOK, SUB, LANES), lambda j, e: (j, 0, 0), pipeline_mode=pl.Buffered(1))
    return pl.pallas_call(
        _moe_kernel,
        grid=(N_MOE_BLOCKS, N_EXPERTS),
        in_specs=[
            smem(SLOTS), smem(SLOTS), smem(LANES), smem(LANES),
            blk,
            pl.BlockSpec((None, None, D_MODEL, 2 * D_EXPERT), lambda j, e: (l, e, 0, 0)),
            pl.BlockSpec((None, None, 1, 2 * D_EXPERT), lambda j, e: (l, e, 0, 0)),
            pl.BlockSpec((None, None, D_EXPERT, D_MODEL), lambda j, e: (l, e, 0, 0)),
            pl.BlockSpec((None, None, 1, D_MODEL), lambda j, e: (l, e, 0, 0)),
        ],
        out_specs=blk,
        out_shape=jax.ShapeDtypeStruct((N_TOK, SUB, LANES), F32),
        scratch_shapes=[pltpu.VMEM((MOE_CHUNK * SUB, LANES), F32)] * 4
        + [pltpu.SMEM((MAX_GRAN + 2 * CHUNK_GRAN,), jnp.int32)],
        compiler_params=pltpu.CompilerParams(
            dimension_semantics=("arbitrary", "arbitrary"), vmem_limit_bytes=VMEM_LIMIT),
        name="moe",
    )(ids.reshape(-1), sw.reshape(-1), seg_off.reshape(-1), n_gran.reshape(-1),
      h2t.reshape(N_TOK, SUB, LANES), w["exp_w1"], w["exp_b1"], w["exp_w2"], w["exp_b2"]
      ).reshape(N_TOK * SUB, LANES)


def _axial_tables(n_tokens, dim):
    rows = n_tokens // GRID_W
    row = jnp.repeat(jnp.arange(rows), GRID_W)
    col = jnp.tile(jnp.arange(GRID_W), rows)
    half = dim // 2
    inv = 1.0 / (ROPE_BASE ** (jnp.arange(0, half, 2, dtype=F32) / half))

    def axis_angles(pos):
        ang = pos.astype(F32)[:, None] * inv[None, :]
        return jnp.concatenate([ang, ang], axis=-1)

    ang = jnp.concatenate([axis_angles(row), axis_angles(col)], axis=-1)
    return jnp.cos(ang), jnp.sin(ang)


def _rope_slot_tables(dim, lane0, copies):
    cos, sin = _axial_tables(DEC_SEQ, dim)
    quarter = dim // 4
    sign = jnp.where((jnp.arange(dim) % (dim // 2)) < quarter, -1.0, 1.0)
    sin = sin * sign
    cos_slot = jnp.ones((DEC_SEQ, LANES), F32)
    sin_slot = jnp.zeros((DEC_SEQ, LANES), F32)
    for c in range(copies):
        cos_slot = cos_slot.at[:, lane0 + c * dim:lane0 + (c + 1) * dim].set(cos)
        sin_slot = sin_slot.at[:, lane0 + c * dim:lane0 + (c + 1) * dim].set(sin)
    ident = (jnp.ones((1, TILE, LANES), F32), jnp.zeros((1, TILE, LANES), F32))
    cos_t = jnp.concatenate([ident[0], cos_slot.reshape(LAT_TILES_PER_SEQ, TILE, LANES)], axis=0)
    sin_t = jnp.concatenate([ident[1], sin_slot.reshape(LAT_TILES_PER_SEQ, TILE, LANES)], axis=0)
    return cos_t, sin_t


def _half_swap(block):
    dst = jnp.arange(2 * LANES)
    src = jnp.where((dst % block) < block // 2, dst + block // 2, dst - block // 2)
    return (jnp.arange(2 * LANES)[:, None] == src[None, :]).astype(BF16)


def _pad_last(a, width):
    return jnp.pad(a, [(0, 0)] * (a.ndim - 1) + [(0, width - a.shape[-1])])


def _prepare(p):
    L = DEPTH
    w = {}
    w_in = p["w_in"]
    kr_slot = jnp.pad(w_in[:, :, OFF_KVA + KV_LORA:OFF_LRU_X], ((0, 0), (0, 0), (MLA_NOPE, LANES - MLA_QK)))
    w["w_in"] = jnp.concatenate(
        [w_in[:, :, OFF_QA:OFF_KVA + KV_LORA], kr_slot, w_in[:, :, OFF_LRU_X:IN_COLS]], axis=-1).astype(BF16)
    w["wqb"] = _pad_last(p["w_q_b"].reshape(L, Q_LORA, MLA_HEADS, MLA_QK), LANES).reshape(
        L, Q_LORA, MLA_HEADS * LANES).astype(BF16)
    kvb = p["w_kv_b"].reshape(L, KV_LORA, MLA_HEADS, MLA_NOPE + MLA_V)
    w["wkvk"] = _pad_last(kvb[..., :MLA_NOPE], LANES).reshape(L, KV_LORA, MLA_HEADS * LANES).astype(BF16)
    w["wkvv"] = kvb[..., MLA_NOPE:].reshape(L, KV_LORA, MLA_HEADS * MLA_V).astype(BF16)
    row = lambda a: a[:, None, :]
    w["norm1"] = row(p["norm1_g"])
    w["norm2"] = row(p["norm2_g"])
    w["qa_norm"] = row(p["mla_qa_norm"])
    w["kva_norm"] = row(p["mla_kva_norm"])
    w["qn"] = row(_pad_last(p["mla_qn"], LANES))
    w["kn"] = row(_pad_last(p["mla_kn"], LANES))
    w["dqn"] = row(jnp.tile(p["diff_qn"], (1, 2)))
    w["dkn"] = row(jnp.tile(p["diff_kn"], (1, 2)))
    w["cos_m"], w["sin_m"] = _rope_slot_tables(MLA_ROPE, MLA_NOPE, 1)
    w["cos_d"], w["sin_d"] = _rope_slot_tables(DIFF_DH, 0, 2)
    w["swap_m"] = _half_swap(MLA_ROPE // 2)
    w["swap_d"] = _half_swap(DIFF_DH // 2)
    w["conv_w"] = p["lru_conv_w"]
    w["conv_b"] = row(p["lru_conv_b"])
    per = (D_RNN // 2) // LRU_BW
    gw = p["lru_gate_w"].reshape(L, 4, 2, per, LRU_BW, LRU_BW)
    eye = jnp.eye(per, dtype=F32)
    w["wbd"] = jnp.einsum("lkhacd,ab->lkhacbd", gw, eye).reshape(L, 4, 2, D_RNN // 2, D_RNN // 2).astype(BF16)
    w["gate_b"] = p["lru_gate_b"].reshape(L, 4, D_RNN)
    w["lam"] = p["lru_lambda"]
    w["w_o_mla"] = p["w_o_mla"].astype(BF16)
    w["w_o_lru"] = p["w_o_lru"].astype(BF16)
    w["w_o_diff"] = p["w_o_diff"].astype(BF16)
    w["w_merge"] = p["w_merge"].astype(BF16)
    w["b_merge"] = row(p["b_merge"])
    w["w_out"] = p["w_out"].astype(BF16)
    w["router_w"] = _pad_last(p["router_w"], LANES)
    w["router_b"] = row(_pad_last(p["router_b"], LANES))
    w["exp_w1"] = p["exp_w1"]
    w["exp_b1"] = p["exp_b1"][:, :, None, :]
    w["exp_w2"] = p["exp_w2"]
    w["exp_b2"] = p["exp_b2"][:, :, None, :]
    w["diff_lambda"] = p["diff_lambda"]
    w["diff_subln"] = row(p["diff_subln"])
    return w


def kernel(x_prompt, x_sample, cache_mla_ckv, cache_mla_krope, state_lru, cache_diff_k, cache_diff_v,
           c, c_ctx, ada_w, ada_b, norm1_g, norm2_g, w_in, mla_qa_norm, w_q_b, mla_kva_norm, w_kv_b,
           mla_qn, mla_kn, w_o_mla, lru_conv_w, lru_conv_b, lru_gate_w, lru_gate_b, lru_lambda, w_o_lru,
           diff_qn, diff_kn, diff_lambda, diff_subln, w_o_diff, w_merge, b_merge, w_out,
           router_w, router_b, exp_w1, exp_b1, exp_w2, exp_b2):
    params = dict(
        norm1_g=norm1_g, norm2_g=norm2_g, w_in=w_in, mla_qa_norm=mla_qa_norm, w_q_b=w_q_b,
        mla_kva_norm=mla_kva_norm, w_kv_b=w_kv_b, mla_qn=mla_qn, mla_kn=mla_kn, w_o_mla=w_o_mla,
        lru_conv_w=lru_conv_w, lru_conv_b=lru_conv_b, lru_gate_w=lru_gate_w, lru_gate_b=lru_gate_b,
        lru_lambda=lru_lambda, w_o_lru=w_o_lru, diff_qn=diff_qn, diff_kn=diff_kn, diff_lambda=diff_lambda,
        diff_subln=diff_subln, w_o_diff=w_o_diff, w_merge=w_merge, b_merge=b_merge, w_out=w_out,
        router_w=router_w, router_b=router_b, exp_w1=exp_w1, exp_b1=exp_b1, exp_w2=exp_w2, exp_b2=exp_b2)
    w = _prepare(params)

    cond = jnp.concatenate(
        [c, c_ctx[None, :], jnp.zeros((COND_ROWS - DEC_BATCH - 1, D_MODEL), F32)], axis=0)
    mod = _ada_call(cond, ada_w, ada_b[:, None, :])
    mod3 = mod.reshape(DEPTH * COND_ROWS, 1, 6 * D_MODEL)

    place = jnp.pad(jnp.eye(MLA_ROPE, dtype=F32), ((0, 0), (MLA_NOPE, LANES - MLA_QK))).astype(BF16)
    k_ctx, v_ctx = _kvx_call(cache_mla_ckv, cache_mla_krope, w["wkvk"], w["wkvv"], w["kn"], place)
    dk_ctx = cache_diff_k.reshape(DEC_BATCH, DEPTH, PAST_LEN, 512)
    dv_ctx = cache_diff_v.reshape(DEC_BATCH, DEPTH, PAST_LEN, 512)

    x = jnp.concatenate([x_prompt.reshape(N_CTX_TOK, D_MODEL), x_sample.reshape(-1, D_MODEL)], axis=0)
    h0_ctx = jnp.zeros((BATCH, 2, D_RNN), F32)
    lat_blk = N_CTX_TOK // DEC_SEQ
    new_ckv, new_krope, new_lru, new_dk, new_dv = [], [], [], [], []
    moe_out = None
    for l in range(DEPTH):
        lam_init = 0.8 - 0.6 * math.exp(-0.3 * l)
        x, (qm, km, vm, qd, kd, vd, lx, lg, ckv_o, kro_o, dk_o, dv_o) = _front_call(l, x, moe_out, mod3, w)

        u_ctx, hf_ctx = _lru_call(l, lx, lg, h0_ctx, w, BATCH, SEQ, 0)
        u_lat, _ = _lru_call(l, lx, lg, state_lru[:, l], w, DEC_BATCH, DEC_SEQ, lat_blk)

        om_ctx = _attn_call(_mla_attn_kernel, "mla_ctx", qm, [(km, vm, "tok")], [], [], 0, BATCH, 1, 512)
        om_lat = _attn_call(_mla_attn_kernel, "mla_lat", qm,
                            [(k_ctx, v_ctx, lambda b: (l, b, 0, 0)), (km, vm, "tok")], [], [],
                            lat_blk, DEC_BATCH, LAT_TILES_PER_SEQ, 512)

        dk_extra = [w["diff_lambda"], w["diff_subln"]]
        dk_specs = lambda: [pl.BlockSpec((None, 4, DIFF_DH), lambda b: (l, 0, 0)),
                            pl.BlockSpec((None, 1, LANES), lambda b: (l, 0, 0))]
        dkern = lambda n_seg, *refs: _diff_attn_kernel(n_seg, lam_init, *refs)
        od_ctx = _attn_call(dkern, "diff_ctx", qd, [(kd, vd, "tok")], dk_extra, dk_specs(),
                            0, BATCH, 1, 512)
        od_lat = _attn_call(dkern, "diff_lat", qd,
                            [(dk_ctx, dv_ctx, lambda b: (b, l, 0, 0)), (kd, vd, "tok")], dk_extra, dk_specs(),
                            lat_blk, DEC_BATCH, LAT_TILES_PER_SEQ, 512)

        x, h2t, ids, sw, seg_off, n_gran = _back_call(
            l, x, mod3, (om_ctx, om_lat), (u_ctx, u_lat), (od_ctx, od_lat), w)
        moe_out = _moe_call(l, h2t, ids, sw, seg_off, n_gran, w)

        new_ckv.append(ckv_o.reshape(BATCH, SEQ, KV_LORA))
        new_krope.append(kro_o.reshape(BATCH, SEQ, MLA_ROPE))
        new_lru.append(hf_ctx)
        new_dk.append(dk_o.reshape(BATCH, SEQ, DIFF_HEADS, 2, DIFF_DH))
        new_dv.append(dv_o.reshape(BATCH, SEQ, DIFF_HEADS, DIFF_DV))

    x_ctx, x_lat = _residual_call(DEPTH - 1, x, moe_out, mod3)
    xp = x_ctx.reshape(BATCH, SEQ, D_MODEL)
    xs = x_lat.reshape(DEC_BATCH, DEC_SEQ, D_MODEL)
    return (xp, xs, jnp.stack(new_ckv, axis=1), jnp.stack(new_krope, axis=1), jnp.stack(new_lru, axis=1),
            jnp.stack(new_dk, axis=1), jnp.stack(new_dv, axis=1))
```

```python
import functools
import math

import jax
import jax.numpy as jnp
from jax import lax
from jax.experimental import pallas as pl
from jax.experimental.pallas import tpu as pltpu

F32 = jnp.float32
BF16 = jnp.bfloat16

D_MODEL = 1024
BATCH = 16
SEQ = 256
DEPTH = 4
DEC_BATCH = 8
DEC_SEQ = 1024
PAST_LEN = 256
GRID_W = 64
ROPE_BASE = 10000.0
NORM_EPS = 1e-6
LOG2_E = math.log2(math.e)

MLA_HEADS = 8
MLA_NOPE = 64
MLA_ROPE = 32
MLA_QK = MLA_NOPE + MLA_ROPE
MLA_V = 64
Q_LORA = 384
KV_LORA = 256
D_RNN = 512
LRU_BLOCKS = 8
LRU_BW = D_RNN // LRU_BLOCKS
CONV_W = 4
LRU_C = 8.0
DIFF_HEADS = 4
DIFF_DH = 64
DIFF_DV = 2 * DIFF_DH
N_EXPERTS = 32
TOP_K = 4
D_EXPERT = 512
SWIGLU_LIMIT = 7.0
SWIGLU_ALPHA = 1.702

OFF_QA = 0
OFF_KVA = OFF_QA + Q_LORA
OFF_LRU_X = OFF_KVA + KV_LORA + MLA_ROPE
OFF_LRU_G = OFF_LRU_X + D_RNN
OFF_DQ = OFF_LRU_G + D_RNN
OFF_DK = OFF_DQ + DIFF_HEADS * 2 * DIFF_DH
OFF_DV = OFF_DK + DIFF_HEADS * 2 * DIFF_DH
IN_COLS = OFF_DV + DIFF_HEADS * DIFF_DV

LANES = 128
TILE = 256
N_CTX_TILES = BATCH * SEQ // TILE
LAT_TILES_PER_SEQ = DEC_SEQ // TILE
N_LAT_TILES = DEC_BATCH * LAT_TILES_PER_SEQ
N_TILES = N_CTX_TILES + N_LAT_TILES
N_CTX_TOK = BATCH * SEQ
N_TOK = N_TILES * TILE
COND_ROWS = 16
CTX_COND_ROW = DEC_BATCH

P_QA = 0
P_CKV = P_QA + Q_LORA
P_KR = P_CKV + KV_LORA
P_LX = P_KR + LANES
P_LG = P_LX + D_RNN
P_DQ = P_LG + D_RNN
P_DK = P_DQ + 512
P_DV = P_DK + 512
P_COLS = P_DV + 512

SUB = 8
MOE_TILES = 16
MOE_TOK = MOE_TILES * TILE
N_MOE_BLOCKS = N_TOK // MOE_TOK
GRAN = 8
SLOTS = TILE * TOP_K + N_EXPERTS * GRAN
MOE_CHUNK = 256
CHUNK_GRAN = MOE_CHUNK // GRAN
CHUNK_SHIFT = CHUNK_GRAN.bit_length() - 1
MAX_GRAN = MOE_TILES * SLOTS // GRAN
VMEM_LIMIT = 56 * 1024 * 1024


def _cond_row(i):
    return jnp.where(i < N_CTX_TILES, CTX_COND_ROW, (i - N_CTX_TILES) // LAT_TILES_PER_SEQ)


def _rope_idx(i):
    return jnp.where(i < N_CTX_TILES, 0, 1 + (i - N_CTX_TILES) % LAT_TILES_PER_SEQ)


def _ctx_out_idx(i):
    return jnp.minimum(i, N_CTX_TILES - 1)


def _const_spec(shape, index):
    return pl.BlockSpec(shape, lambda *_: index, pipeline_mode=pl.Buffered(1))


def _dot(a, b):
    return jnp.dot(a, b, preferred_element_type=F32)


def _dot_nt(a, b):
    return lax.dot_general(a, b, (((1,), (1,)), ((), ())), preferred_element_type=F32)


def _rms(x, n=None):
    n = x.shape[-1] if n is None else n
    ss = jnp.sum(x * x, axis=-1, keepdims=True)
    return x * lax.rsqrt(ss * (1.0 / n) + NORM_EPS)


def _rope_pair(ya, yb, cos, sin_signed, swap2):
    rot = _dot(jnp.concatenate([ya, yb], axis=-1).astype(BF16), swap2)
    return ya * cos + rot[:, :LANES] * sin_signed, yb * cos + rot[:, LANES:] * sin_signed


def _rms_factor(x, n):
    return lax.rsqrt(jnp.sum(x * x, axis=-1, keepdims=True) * (1.0 / n) + NORM_EPS)


def _sigmoid(x):
    return 0.5 * jnp.tanh(0.5 * x) + 0.5


def _lane_iota(shape):
    return lax.broadcasted_iota(jnp.int32, shape, len(shape) - 1)


def _ada_kernel(cond_ref, w_ref, b_ref, o_ref):
    c = cond_ref[...]
    s = c * jax.nn.sigmoid(c)
    o_ref[...] = _dot(s.astype(BF16), w_ref[...].astype(BF16)) + b_ref[...]


def _ada_call(cond, ada_w, ada_b3):
    cb = 1024
    return pl.pallas_call(
        _ada_kernel,
        grid=(DEPTH, 6 * D_MODEL // cb),
        in_specs=[
            pl.BlockSpec((COND_ROWS, D_MODEL), lambda l, j: (0, 0)),
            pl.BlockSpec((None, D_MODEL, cb), lambda l, j: (l, 0, j)),
            pl.BlockSpec((None, 1, cb), lambda l, j: (l, 0, j)),
        ],
        out_specs=pl.BlockSpec((None, COND_ROWS, cb), lambda l, j: (l, 0, j)),
        out_shape=jax.ShapeDtypeStruct((DEPTH, COND_ROWS, 6 * D_MODEL), F32),
        compiler_params=pltpu.CompilerParams(
            dimension_semantics=("arbitrary", "arbitrary"), vmem_limit_bytes=VMEM_LIMIT),
        name="ada_mod",
    )(cond, ada_w, ada_b3)


def _mla_k_heads(kn, krs, kn_gain, cos, sin, swap, store):
    for hp in range(MLA_HEADS // 2):
        heads = (2 * hp, 2 * hp + 1)
        raw = [kn[:, hd * LANES:(hd + 1) * LANES] + krs for hd in heads]
        ys = [kh * kn_gain for kh in raw]
        if cos is not None:
            ys = _rope_pair(ys[0], ys[1], cos, sin, swap)
        for hd, kh, y in zip(heads, raw, ys):
            store(hd, y * _rms_factor(kh, MLA_QK))


def _kvx_kernel(ckv_ref, kr_ref, wk_ref, wv_ref, kn_ref, place_ref, k_ref, v_ref):
    c = ckv_ref[...].astype(BF16)
    kn = _dot(c, wk_ref[...])
    v_ref[...] = _dot(c, wv_ref[...]).astype(BF16)
    krs = _dot(kr_ref[...].astype(BF16), place_ref[...])

    def store(hd, kh):
        k_ref[:, hd * LANES:(hd + 1) * LANES] = kh.astype(BF16)

    _mla_k_heads(kn, krs, kn_ref[...], None, None, None, store)


def _kvx_call(cache_ckv, cache_krope, wkvk, wkvv, kn_pad, place):
    return pl.pallas_call(
        _kvx_kernel,
        grid=(DEPTH, DEC_BATCH),
        in_specs=[
            pl.BlockSpec((None, None, PAST_LEN, KV_LORA), lambda l, b: (b, l, 0, 0)),
            pl.BlockSpec((None, None, PAST_LEN, MLA_ROPE), lambda l, b: (b, l, 0, 0)),
            pl.BlockSpec((None, KV_LORA, MLA_HEADS * LANES), lambda l, b: (l, 0, 0)),
            pl.BlockSpec((None, KV_LORA, MLA_HEADS * MLA_V), lambda l, b: (l, 0, 0)),
            pl.BlockSpec((None, 1, LANES), lambda l, b: (l, 0, 0)),
            pl.BlockSpec((MLA_ROPE, LANES), lambda l, b: (0, 0)),
        ],
        out_specs=[
            pl.BlockSpec((None, None, PAST_LEN, MLA_HEADS * LANES), lambda l, b: (l, b, 0, 0)),
            pl.BlockSpec((None, None, PAST_LEN, MLA_HEADS * MLA_V), lambda l, b: (l, b, 0, 0)),
        ],
        out_shape=[
            jax.ShapeDtypeStruct((DEPTH, DEC_BATCH, PAST_LEN, MLA_HEADS * LANES), BF16),
            jax.ShapeDtypeStruct((DEPTH, DEC_BATCH, PAST_LEN, MLA_HEADS * MLA_V), BF16),
        ],
        compiler_params=pltpu.CompilerParams(
            dimension_semantics=("arbitrary", "arbitrary"), vmem_limit_bytes=VMEM_LIMIT),
        name="ctx_kv_expand",
    )(cache_ckv, cache_krope, wkvk, wkvv, kn_pad, place)


def _modulated(x, gain, shift, scale):
    return _rms(x) * gain * (1.0 + scale) + shift


def _untile(t_ref):
    rows = t_ref.shape[0] // SUB
    return jnp.concatenate([t_ref[pl.ds(s, rows, stride=SUB), :] for s in range(SUB)], axis=-1)


def _moe_residual(x1_ref, yt_ref, modp_ref):
    return x1_ref[...] + modp_ref[:, 5 * D_MODEL:6 * D_MODEL] * _untile(yt_ref)


def _front_kernel(has_moe, *refs):
    if has_moe:
        x1_ref, yt_ref, modp_ref = refs[:3]
        refs = refs[3:]
    else:
        x_ref = refs[0]
        refs = refs[1:]
    (mod_ref, n1_ref, wa_ref, wkr_ref, wc_ref, qan_ref, wqb_ref, kvan_ref, wkvk_ref, wkvv_ref,
     qn_ref, kn_ref, dqn_ref, dkn_ref, cosm_ref, sinm_ref, cosd_ref, sind_ref, swapm_ref, swapd_ref) = refs[:20]
    refs = refs[20:]
    if has_moe:
        xo_ref = refs[0]
        refs = refs[1:]
    (qm_ref, km_ref, vm_ref, qd_ref, kd_ref, vd_ref, lx_ref, lg_ref,
     ckv_ref, kro_ref, dko_ref, dvo_ref) = refs
    if has_moe:
        x = _moe_residual(x1_ref, yt_ref, modp_ref)
        xo_ref[...] = x
    else:
        x = x_ref[...]
    mod = mod_ref[...]
    h = _modulated(x, n1_ref[...], mod[:, 0:D_MODEL], mod[:, D_MODEL:2 * D_MODEL])
    hb = h.astype(BF16)

    def proj(col, width):
        if col < P_KR:
            w_cols = wa_ref[:, col:col + width]
        elif col == P_KR:
            w_cols = wkr_ref[...]
        else:
            w_cols = wc_ref[:, col - P_LX:col - P_LX + width]
        return _dot(hb, w_cols)

    lane = _lane_iota((1, LANES))
    cosm, sinm, swap_m = cosm_ref[...], sinm_ref[...], swapm_ref[...]
    cosd, sind, swap_d = cosd_ref[...], sind_ref[...], swapd_ref[...]

    qa = _rms(proj(P_QA, Q_LORA)) * qan_ref[...]
    q = _dot(qa.astype(BF16), wqb_ref[...])
    q_scale = MLA_QK ** -0.5 * LOG2_E
    for hp in range(MLA_HEADS // 2):
        heads = (2 * hp, 2 * hp + 1)
        raw = [q[:, hd * LANES:(hd + 1) * LANES] for hd in heads]
        ys = _rope_pair(raw[0] * qn_ref[...], raw[1] * qn_ref[...], cosm, sinm, swap_m)
        for hd, qh, y in zip(heads, raw, ys):
            qm_ref[:, hd * LANES:(hd + 1) * LANES] = (y * (_rms_factor(qh, MLA_QK) * q_scale)).astype(BF16)

    ckv = _rms(proj(P_CKV, KV_LORA)) * kvan_ref[...]
    krs = proj(P_KR, LANES)
    cb = ckv.astype(BF16)
    vm_ref[...] = _dot(cb, wkvv_ref[...]).astype(BF16)
    kn = _dot(cb, wkvk_ref[...])

    def store_k(hd, kh):
        km_ref[:, hd * LANES:(hd + 1) * LANES] = kh.astype(BF16)

    _mla_k_heads(kn, krs, kn_ref[...], cosm, sinm, swap_m, store_k)

    lx_ref[...] = proj(P_LX, D_RNN)
    lg_ref[...] = proj(P_LG, D_RNN)

    low = lane < DIFF_DH

    def pair_factor(t):
        sq = t * t
        s_all = jnp.sum(sq, axis=-1, keepdims=True)
        s_lo = jnp.sum(jnp.where(low, sq, 0.0), axis=-1, keepdims=True)
        r_lo = lax.rsqrt(s_lo * (1.0 / DIFF_DH) + NORM_EPS)
        r_hi = lax.rsqrt((s_all - s_lo) * (1.0 / DIFF_DH) + NORM_EPS)
        return jnp.where(low, r_lo, r_hi)

    dq = proj(P_DQ, 512)
    dk = proj(P_DK, 512)
    d_scale = DIFF_DH ** -0.5 * LOG2_E
    k_heads = []
    for hp in range(DIFF_HEADS // 2):
        sls = [slice(hd * LANES, (hd + 1) * LANES) for hd in (2 * hp, 2 * hp + 1)]
        qs = _rope_pair(dq[:, sls[0]] * dqn_ref[...], dq[:, sls[1]] * dqn_ref[...], cosd, sind, swap_d)
        ks = _rope_pair(dk[:, sls[0]] * dkn_ref[...], dk[:, sls[1]] * dkn_ref[...], cosd, sind, swap_d)
        for sl, qh, kh in zip(sls, qs, ks):
            qd_ref[:, sl] = (qh * (pair_factor(dq[:, sl]) * d_scale)).astype(BF16)
            kh = kh * pair_factor(dk[:, sl])
            kd_ref[:, sl] = kh.astype(BF16)
            k_heads.append(kh)
    dv = proj(P_DV, 512)
    vd_ref[...] = dv.astype(BF16)

    @pl.when(pl.program_id(0) < N_CTX_TILES)
    def _():
        ckv_ref[...] = ckv
        kro_ref[...] = krs[:, MLA_NOPE:MLA_NOPE + MLA_ROPE]
        for hd in range(DIFF_HEADS):
            dko_ref[:, hd * LANES:(hd + 1) * LANES] = k_heads[hd]
        dvo_ref[...] = dv


def _mod_spec(l):
    return pl.BlockSpec((None, 1, 6 * D_MODEL), lambda i: (l * COND_ROWS + _cond_row(i), 0, 0))


def _tile_spec():
    return pl.BlockSpec((TILE * SUB, LANES), lambda i: (i, 0))


def _front_call(l, x, moe_out, mod3, w):
    tok = lambda width: pl.BlockSpec((TILE, width), lambda i: (i, 0))
    ctx = lambda width: pl.BlockSpec((TILE, width), lambda i: (_ctx_out_idx(i), 0))
    lay = lambda *shape: _const_spec((None,) + shape, (l,) + (0,) * len(shape))
    rope = pl.BlockSpec((None, TILE, LANES), lambda i: (_rope_idx(i), 0, 0))
    swap = pl.BlockSpec((2 * LANES, 2 * LANES), lambda i: (0, 0))
    n_ctx_rows = N_CTX_TOK
    has_moe = moe_out is not None
    if has_moe:
        lead_specs = [tok(D_MODEL), _tile_spec(), _mod_spec(l - 1)]
        lead_args = [x, moe_out, mod3]
        x_out_specs = [tok(D_MODEL)]
        x_out_shape = [jax.ShapeDtypeStruct((N_TOK, D_MODEL), F32)]
    else:
        lead_specs, lead_args, x_out_specs, x_out_shape = [tok(D_MODEL)], [x], [], []
    outs = pl.pallas_call(
        functools.partial(_front_kernel, has_moe),
        grid=(N_TILES,),
        in_specs=lead_specs + [
            _mod_spec(l),
            lay(1, D_MODEL), lay(D_MODEL, P_KR), lay(D_MODEL, LANES), lay(D_MODEL, P_COLS - P_LX), lay(1, Q_LORA), lay(Q_LORA, MLA_HEADS * LANES),
            lay(1, KV_LORA), lay(KV_LORA, MLA_HEADS * LANES), lay(KV_LORA, MLA_HEADS * MLA_V),
            lay(1, LANES), lay(1, LANES), lay(1, LANES), lay(1, LANES),
            rope, rope, rope, rope, swap, swap,
        ],
        out_specs=x_out_specs + [
            tok(1024), tok(1024), tok(512), tok(512), tok(512), tok(512), tok(512), tok(512),
            ctx(KV_LORA), ctx(MLA_ROPE), ctx(512), ctx(512),
        ],
        out_shape=x_out_shape + [
            jax.ShapeDtypeStruct((N_TOK, 1024), BF16), jax.ShapeDtypeStruct((N_TOK, 1024), BF16),
            jax.ShapeDtypeStruct((N_TOK, 512), BF16), jax.ShapeDtypeStruct((N_TOK, 512), BF16),
            jax.ShapeDtypeStruct((N_TOK, 512), BF16), jax.ShapeDtypeStruct((N_TOK, 512), BF16),
            jax.ShapeDtypeStruct((N_TOK, 512), F32), jax.ShapeDtypeStruct((N_TOK, 512), F32),
            jax.ShapeDtypeStruct((n_ctx_rows, KV_LORA), F32),
            jax.ShapeDtypeStruct((n_ctx_rows, MLA_ROPE), F32),
            jax.ShapeDtypeStruct((n_ctx_rows, 512), F32),
            jax.ShapeDtypeStruct((n_ctx_rows, 512), F32),
        ],
        compiler_params=pltpu.CompilerParams(
            dimension_semantics=("arbitrary",), vmem_limit_bytes=VMEM_LIMIT),
        name="front",
    )(*lead_args, mod3, w["norm1"], w["w_in_a"], w["w_in_kr"], w["w_in_c"], w["qa_norm"], w["wqb"], w["kva_norm"], w["wkvk"], w["wkvv"],
      w["qn"], w["kn"], w["dqn"], w["dkn"], w["cos_m"], w["sin_m"], w["cos_d"], w["sin_d"],
      w["swap_m"], w["swap_d"])
    if has_moe:
        return outs[0], outs[1:]
    return x, outs


def _residual_kernel(x1_ref, yt_ref, modp_ref, oc_ref, ol_ref):
    x = _moe_residual(x1_ref, yt_ref, modp_ref)
    is_ctx = pl.program_id(0) < N_CTX_TILES

    @pl.when(is_ctx)
    def _():
        oc_ref[...] = x

    @pl.when(jnp.logical_not(is_ctx))
    def _():
        ol_ref[...] = x


def _residual_call(l, x1, moe_out, mod3):
    tok = pl.BlockSpec((TILE, D_MODEL), lambda i: (i, 0))
    return pl.pallas_call(
        _residual_kernel,
        grid=(N_TILES,),
        in_specs=[tok, _tile_spec(), _mod_spec(l)],
        out_specs=[
            pl.BlockSpec((TILE, D_MODEL), lambda i: (jnp.minimum(i, N_CTX_TILES - 1), 0)),
            pl.BlockSpec((TILE, D_MODEL), lambda i: (jnp.maximum(i - N_CTX_TILES, 0), 0)),
        ],
        out_shape=[
            jax.ShapeDtypeStruct((N_CTX_TOK, D_MODEL), F32),
            jax.ShapeDtypeStruct((N_TOK - N_CTX_TOK, D_MODEL), F32),
        ],
        compiler_params=pltpu.CompilerParams(
            dimension_semantics=("arbitrary",), vmem_limit_bytes=VMEM_LIMIT),
        name="moe_residual",
    )(x1, moe_out, mod3)


def _scan(a, b, h0, reverse):
    n = a.shape[0]
    row = lax.broadcasted_iota(jnp.int32, (n, 1), 0)
    d = 1
    while d < n:
        if reverse:
            valid = row < n - d
            shift = n - d
        else:
            valid = row >= d
            shift = d
        a_s = jnp.where(valid, pltpu.roll(a, shift, 0), 1.0)
        b_s = jnp.where(valid, pltpu.roll(b, shift, 0), 0.0)
        b = a * b_s + b
        a = a * a_s
        d *= 2
    return a * h0 + b


def _gelu_tanh(x):
    return 0.5 * x * (1.0 + jnp.tanh(math.sqrt(2.0 / math.pi) * (x + 0.044715 * x * x * x)))


N_LANE_TILES = D_RNN // LANES


def _strided_scan(a, b, h0, reverse, first, sa_ref, sb_ref, tot_ref):
    n = a.shape[0]
    g = n // SUB
    for c in range(N_LANE_TILES):
        sa_ref[pl.ds(c * n, n), :] = a[:, c * LANES:(c + 1) * LANES]
        sb_ref[pl.ds(c * n, n), :] = b[:, c * LANES:(c + 1) * LANES]
    order = range(SUB - 1, -1, -1) if reverse else range(SUB)
    row = lax.broadcasted_iota(jnp.int32, (g, 1), 0)
    finals = []
    for c in range(N_LANE_TILES):
        comps = []
        acc_a = acc_b = None
        for r in order:
            cls = pl.ds(c * n + r, g, stride=SUB)
            a_r, b_r = sa_ref[cls, :], sb_ref[cls, :]
            if acc_a is None:
                acc_a, acc_b = a_r, b_r
            else:
                acc_b = a_r * acc_b + b_r
                acc_a = a_r * acc_a
            comps.append((cls, acc_a, acc_b))
        h0c = h0[:, c * LANES:(c + 1) * LANES]
        hg = _scan(acc_a, acc_b, h0c, reverse)
        if reverse:
            carry = jnp.where(row == g - 1, h0c, pltpu.roll(hg, g - 1, 0))
            finals.append(hg[0:1, :])
        else:
            carry = jnp.where(row == 0, h0c, pltpu.roll(hg, 1, 0))
            finals.append(hg[g - 1:g, :])
        for cls, comp_a, comp_b in comps:
            h = comp_a * carry + comp_b
            tot_ref[cls, :] = h if first else tot_ref[cls, :] + h
    return jnp.concatenate(finals, axis=-1)


def _sqrt_pos(t):
    return jnp.where(t > 0.0, t * lax.rsqrt(t), 0.0)


def _lru_kernel(lx_ref, lg_ref, cw_ref, cb_ref, wbd_ref, gb_ref, lam_ref, h0_ref, u_ref, hf_ref,
                sa_ref, sb_ref, tot_ref):
    x = lx_ref[...]
    n = x.shape[0]
    row = lax.broadcasted_iota(jnp.int32, (n, 1), 0)
    cw = cw_ref[...]
    xr = cb_ref[...] + cw[2:3] * x
    xr = xr + cw[0:1] * jnp.where(row >= 2, pltpu.roll(x, 2, 0), 0.0)
    xr = xr + cw[1:2] * jnp.where(row >= 1, pltpu.roll(x, 1, 0), 0.0)
    xr = xr + cw[3:4] * jnp.where(row < n - 1, pltpu.roll(x, n - 1, 0), 0.0)
    xb = xr.astype(BF16)
    half = D_RNN // 2
    lam = lam_ref[...]
    h0 = h0_ref[...]
    gb = gb_ref[...]
    for d in range(2):
        pre = []
        for g in range(2):
            k = d * 2 + g
            p = jnp.concatenate([_dot(xb[:, :half], wbd_ref[k, 0]), _dot(xb[:, half:], wbd_ref[k, 1])],
                                axis=-1)
            pre.append(p + gb[k:k + 1])
        r = _sigmoid(pre[0])
        i = _sigmoid(pre[1])
        z = -lam[d:d + 1]
        softplus = jnp.maximum(z, 0.0) + jnp.log(1.0 + jnp.exp(-jnp.abs(z)))
        a = jnp.exp(-LRU_C * r * softplus)
        bx = _sqrt_pos(1.0 - a * a) * (i * xr)
        hf_ref[d:d + 1, :] = _strided_scan(a, bx, h0[d:d + 1], d == 1, d == 0, sa_ref, sb_ref, tot_ref)
    total = jnp.concatenate([tot_ref[pl.ds(c * n, n), :] for c in range(N_LANE_TILES)], axis=-1)
    u_ref[...] = (total * _gelu_tanh(lg_ref[...])).astype(BF16)


def _lru_call(l, lx, lg, h0, w, n_seq, seq_len, row_block0):
    seq = lambda: pl.BlockSpec((seq_len, D_RNN), lambda s: (row_block0 + s, 0))
    lay = lambda *shape: _const_spec((None,) + shape, (l,) + (0,) * len(shape))
    return pl.pallas_call(
        _lru_kernel,
        grid=(n_seq,),
        in_specs=[
            seq(), seq(), lay(CONV_W, D_RNN), lay(1, D_RNN), lay(4, 2, D_RNN // 2, D_RNN // 2),
            lay(4, D_RNN), lay(2, D_RNN),
            pl.BlockSpec((None, 2, D_RNN), lambda s: (s, 0, 0)),
        ],
        out_specs=[
            pl.BlockSpec((seq_len, D_RNN), lambda s: (s, 0)),
            pl.BlockSpec((None, 2, D_RNN), lambda s: (s, 0, 0)),
        ],
        out_shape=[
            jax.ShapeDtypeStruct((n_seq * seq_len, D_RNN), BF16),
            jax.ShapeDtypeStruct((n_seq, 2, D_RNN), F32),
        ],
        scratch_shapes=[pltpu.VMEM((N_LANE_TILES * seq_len, LANES), F32)] * 3,
        compiler_params=pltpu.CompilerParams(
            dimension_semantics=("arbitrary",), vmem_limit_bytes=VMEM_LIMIT),
        name="rglru_%d" % seq_len,
    )(lx, lg, w["conv_w"], w["conv_b"], w["wbd"], w["gate_b"], w["lam"], h0)


def _softmax_parts(scores):
    m = scores[0].max(axis=-1, keepdims=True)
    for s in scores[1:]:
        m = jnp.maximum(m, s.max(axis=-1, keepdims=True))
    es = [jnp.exp2(s - m) for s in scores]
    tot = es[0].sum(axis=-1, keepdims=True)
    for e in es[1:]:
        tot = tot + e.sum(axis=-1, keepdims=True)
    return es, tot


def _mla_attn_kernel(n_seg, q_ref, *refs):
    k_refs = refs[0:2 * n_seg:2]
    v_refs = refs[1:2 * n_seg:2]
    o_ref = refs[2 * n_seg]
    low = _lane_iota((1, LANES)) < MLA_V
    for pair in range(MLA_HEADS // 2):
        vs = slice(pair * LANES, (pair + 1) * LANES)
        out = None
        for hh in range(2):
            hs = slice((2 * pair + hh) * LANES, (2 * pair + hh + 1) * LANES)
            qh = q_ref[:, hs]
            es, tot = _softmax_parts([_dot_nt(qh, k[:, hs]) for k in k_refs])
            pv = None
            for e, v in zip(es, v_refs):
                t = _dot(e.astype(BF16), v[:, vs])
                pv = t if pv is None else pv + t
            oh = pv / tot
            out = oh if hh == 0 else jnp.where(low, out, oh)
        o_ref[:, vs] = out.astype(BF16)


def _diff_attn_kernel(n_seg, lam_init, q_ref, *refs):
    k_refs = refs[0:2 * n_seg:2]
    v_refs = refs[1:2 * n_seg:2]
    lam_ref, sub_ref, o_ref = refs[2 * n_seg:2 * n_seg + 3]
    lp = lam_ref[...]
    lam = (jnp.exp(jnp.sum(lp[0:1] * lp[1:2], axis=-1, keepdims=True))
           - jnp.exp(jnp.sum(lp[2:3] * lp[3:4], axis=-1, keepdims=True)) + lam_init)
    low = _lane_iota((1, LANES)) < DIFF_DH
    zero = jnp.zeros((), BF16)
    for hd in range(DIFF_HEADS):
        hs = slice(hd * LANES, (hd + 1) * LANES)
        qh = q_ref[:, hs]
        ks = [k[:, hs].astype(BF16) for k in k_refs]
        e0, t0 = _softmax_parts([_dot_nt(jnp.where(low, qh, zero), k) for k in ks])
        e1, t1 = _softmax_parts([_dot_nt(jnp.where(low, zero, qh), k) for k in ks])
        w0 = 1.0 / t0
        w1 = lam / t1
        o = None
        for a, b, v in zip(e0, e1, v_refs):
            t = _dot((a * w0 - b * w1).astype(BF16), v[:, hs].astype(BF16))
            o = t if o is None else o + t
        o = _rms(o) * sub_ref[...] * (1.0 - lam_init)
        o_ref[:, hs] = o.astype(BF16)


def _over_q_tiles(kernel, n_seg, tiles, q_ref, *refs):
    if tiles == 1:
        kernel(n_seg, q_ref, *refs)
        return

    def tile(t, carry):
        rows = pl.ds(pl.multiple_of(t * TILE, TILE), TILE)
        kernel(n_seg, q_ref.at[rows], *refs[:-1], refs[-1].at[rows])
        return carry

    lax.fori_loop(0, tiles, tile, 0)


def _attn_call(kernel, name, q, segs, extra, extra_specs, q_block0, n_seq, tiles_per_seq, out_width):
    seq_len = tiles_per_seq * TILE
    in_specs = [pl.BlockSpec((seq_len, q.shape[1]), lambda b: (q_block0 + b, 0))]
    args = [q]
    for k, v, kind in segs:
        for arr in (k, v):
            if kind == "tok":
                in_specs.append(pl.BlockSpec((seq_len, arr.shape[1]), lambda b: (q_block0 + b, 0)))
            else:
                in_specs.append(pl.BlockSpec((None, None) + arr.shape[2:], kind))
            args.append(arr)
    in_specs += extra_specs
    args += extra
    return pl.pallas_call(
        functools.partial(_over_q_tiles, kernel, len(segs), tiles_per_seq),
        grid=(n_seq,),
        in_specs=in_specs,
        out_specs=pl.BlockSpec((seq_len, out_width), lambda b: (b, 0)),
        out_shape=jax.ShapeDtypeStruct((n_seq * seq_len, out_width), BF16),
        compiler_params=pltpu.CompilerParams(
            dimension_semantics=("arbitrary",), vmem_limit_bytes=VMEM_LIMIT),
        name=name,
    )(*args)


def _back_kernel(x_ref, mod_ref, n1_ref, n2_ref, omc_ref, oml_ref, ulc_ref, ull_ref, odc_ref, odl_ref,
                 wom_ref, wol_ref, wod_ref, wmg_ref, bmg_ref, wout_ref, rw_ref, rb_ref,
                 x1_ref, h2t_ref, ids_ref, sw_ref, off_ref, ng_ref):
    i = pl.program_id(0)
    is_ctx = i < N_CTX_TILES
    x = x_ref[...]
    mod = mod_ref[...]
    seg = lambda j: mod[:, j * D_MODEL:(j + 1) * D_MODEL]
    hb = _modulated(x, n1_ref[...], seg(0), seg(1)).astype(BF16)
    pick = lambda c_ref, l_ref: jnp.where(is_ctx, c_ref[...], l_ref[...])
    branches = (_dot(pick(omc_ref, oml_ref), wom_ref[...]), _dot(pick(ulc_ref, ull_ref), wol_ref[...]),
                _dot(pick(odc_ref, odl_ref), wod_ref[...]))
    merged = None
    for j, o in enumerate(branches):
        cs = slice(j * D_MODEL, (j + 1) * D_MODEL)
        g = _sigmoid(_dot(hb, wmg_ref[:, cs]) + bmg_ref[:, cs])
        merged = g * o if merged is None else merged + g * o
    x1 = x + seg(2) * _dot(merged.astype(BF16), wout_ref[...])
    x1_ref[...] = x1
    h2 = _modulated(x1, n2_ref[...], seg(3), seg(4))
    for s in range(SUB):
        h2t_ref[pl.ds(s, TILE, stride=SUB), :] = h2[:, s * LANES:(s + 1) * LANES]

    h_hi = h2.astype(BF16)
    h_lo = (h2 - h_hi.astype(F32)).astype(BF16)
    rw = rw_ref[...]
    w_hi = rw.astype(BF16)
    w_lo = (rw - w_hi.astype(F32)).astype(BF16)
    logits = _dot(h_hi, w_hi) + (_dot(h_lo, w_hi) + _dot(h_hi, w_lo)) + rb_ref[...]
    lane = _lane_iota(logits.shape).astype(F32)
    neg = jnp.float32(-jnp.inf)
    work = jnp.where(lane < N_EXPERTS, logits, neg)
    sels, probs = [], []
    top = None
    for _ in range(TOP_K):
        m = work.max(axis=-1, keepdims=True)
        idx = jnp.min(jnp.where(work == m, lane, float(LANES)), axis=-1, keepdims=True)
        sel = lane == idx
        top = m if top is None else top
        sels.append(sel)
        probs.append(jnp.exp(m - top))
        work = jnp.where(sel, neg, work)
    denom = probs[0] + probs[1] + probs[2] + probs[3]

    onehot = jnp.zeros_like(logits)
    for sel in sels:
        onehot = onehot + jnp.where(sel, 1.0, 0.0)
    r_i = lax.broadcasted_iota(jnp.int32, (TILE, TILE), 0)
    c_i = lax.broadcasted_iota(jnp.int32, (TILE, TILE), 1)
    earlier = jnp.where(c_i < r_i, 1.0, 0.0).astype(BF16)
    rank = _dot(earlier, onehot.astype(BF16))
    count = jnp.sum(onehot, axis=0, keepdims=True)
    n_gran = jnp.floor((count + (GRAN - 1.0)) * (1.0 / GRAN))
    r_l = lax.broadcasted_iota(jnp.int32, (LANES, LANES), 0)
    c_l = lax.broadcasted_iota(jnp.int32, (LANES, LANES), 1)
    before = jnp.where(r_l < c_l, 1.0, 0.0).astype(BF16)
    seg_off = _dot(jnp.broadcast_to(n_gran * GRAN, (SUB, LANES)).astype(BF16), before)[0:1]
    slot_base = seg_off + rank
    slot_lane = lax.broadcasted_iota(jnp.int32, (1, SLOTS), 1).astype(F32)
    tok_col = (lax.broadcasted_iota(jnp.int32, (TILE, 1), 0) + (i % MOE_TILES) * TILE).astype(F32)
    id_acc = jnp.zeros((TILE, SLOTS), F32)
    w_acc = jnp.zeros((TILE, SLOTS), F32)
    for sel, p in zip(sels, probs):
        slot = jnp.sum(jnp.where(sel, slot_base, 0.0), axis=-1, keepdims=True)
        hit = slot_lane == slot
        id_acc = jnp.where(hit, tok_col, id_acc)
        w_acc = jnp.where(hit, p / denom, w_acc)
    ids_ref[...] = jnp.sum(id_acc, axis=0, keepdims=True).astype(jnp.int32)
    sw_ref[...] = jnp.sum(w_acc, axis=0, keepdims=True)
    off_ref[...] = seg_off.astype(jnp.int32)
    ng_ref[...] = n_gran.astype(jnp.int32)


def _back_call(l, x, mod3, om, ul, od, w):
    tok = lambda width: pl.BlockSpec((TILE, width), lambda i: (i, 0))
    ctx_in = pl.BlockSpec((TILE, 512), lambda i: (jnp.minimum(i, N_CTX_TILES - 1), 0))
    lat_in = pl.BlockSpec((TILE, 512), lambda i: (jnp.maximum(i - N_CTX_TILES, 0), 0))
    lay = lambda *shape: _const_spec((None,) + shape, (l,) + (0,) * len(shape))
    row = lambda width: pl.BlockSpec((None, 1, width), lambda i: (i, 0, 0))
    return pl.pallas_call(
        _back_kernel,
        grid=(N_TILES,),
        in_specs=[
            tok(D_MODEL), _mod_spec(l),
            lay(1, D_MODEL), lay(1, D_MODEL), ctx_in, lat_in, ctx_in, lat_in, ctx_in, lat_in,
            lay(512, D_MODEL), lay(512, D_MODEL), lay(512, D_MODEL),
            lay(D_MODEL, 3 * D_MODEL), lay(1, 3 * D_MODEL), lay(D_MODEL, D_MODEL),
            lay(D_MODEL, LANES), lay(1, LANES),
        ],
        out_specs=[tok(D_MODEL), _tile_spec(), row(SLOTS), row(SLOTS), row(LANES), row(LANES)],
        out_shape=[
            jax.ShapeDtypeStruct((N_TOK, D_MODEL), F32),
            jax.ShapeDtypeStruct((N_TOK * SUB, LANES), F32),
            jax.ShapeDtypeStruct((N_TILES, 1, SLOTS), jnp.int32),
            jax.ShapeDtypeStruct((N_TILES, 1, SLOTS), F32),
            jax.ShapeDtypeStruct((N_TILES, 1, LANES), jnp.int32),
            jax.ShapeDtypeStruct((N_TILES, 1, LANES), jnp.int32),
        ],
        compiler_params=pltpu.CompilerParams(
            dimension_semantics=("arbitrary",), vmem_limit_bytes=VMEM_LIMIT),
        name="back",
    )(x, mod3, w["norm1"], w["norm2"], om[0], om[1], ul[0], ul[1], od[0], od[1],
      w["w_o_mla"], w["w_o_lru"], w["w_o_diff"],
      w["w_merge"], w["b_merge"], w["w_out"], w["router_w"], w["router_b"])


def _moe_kernel(ids_ref, sw_ref, off_ref, ng_ref, h2t_ref, w1_ref, b1_ref, w2_ref, b2_ref,
                y_ref, xga_ref, xgb_ref, za_ref, zb_ref, gb_ref):
    e = pl.program_id(1)

    @pl.when(e == 0)
    def _():
        y_ref[...] = jnp.zeros_like(y_ref)
        for ref in (xga_ref, xgb_ref, za_ref, zb_ref):
            ref[...] = jnp.zeros_like(ref)

    n_total = 0
    for t in range(MOE_TILES):
        first = t * SLOTS + off_ref[t * LANES + e]

        def add_gran(g, pos, first=first):
            gb_ref[pos] = first + g * GRAN
            return pos + 1

        n_total = lax.fori_loop(0, ng_ref[t * LANES + e], add_gran, n_total)
    for i in range(2 * CHUNK_GRAN):
        gb_ref[n_total + i] = SLOTS - GRAN
    n_chunks = lax.shift_right_logical(n_total + (CHUNK_GRAN - 1), CHUNK_SHIFT)

    def slot_rows(j):
        return pl.ds(j * SUB, SUB)

    def gather(t, xg_ref):
        for g in range(CHUNK_GRAN):
            base = gb_ref[t * CHUNK_GRAN + g]
            for u in range(GRAN):
                xg_ref[slot_rows(g * GRAN + u), :] = h2t_ref[ids_ref[base + u]]

    def expert(xg_ref, z_ref):
        xb = jnp.concatenate([xg_ref[pl.ds(s, MOE_CHUNK, stride=SUB), :] for s in range(SUB)],
                             axis=-1).astype(BF16)
        gu = _dot(xb, w1_ref[...].astype(BF16)) + b1_ref[...]
        gate = jnp.minimum(gu[:, :D_EXPERT], SWIGLU_LIMIT)
        up = jnp.clip(gu[:, D_EXPERT:], -SWIGLU_LIMIT, SWIGLU_LIMIT)
        act = (up + 1.0) * (gate * _sigmoid(SWIGLU_ALPHA * gate))
        y = _dot(act.astype(BF16), w2_ref[...].astype(BF16)) + b2_ref[...]
        for s in range(SUB):
            z_ref[pl.ds(s, MOE_CHUNK, stride=SUB), :] = y[:, s * LANES:(s + 1) * LANES]

    def combine(t, z_ref):
        for g in range(CHUNK_GRAN):
            base = gb_ref[t * CHUNK_GRAN + g]
            toks = [ids_ref[base + u] for u in range(GRAN)]
            new = [y_ref[toks[u]] + sw_ref[base + u] * z_ref[slot_rows(g * GRAN + u), :] for u in range(GRAN)]
            for u in reversed(range(GRAN)):
                y_ref[toks[u]] = new[u]

    gather(0, xga_ref)

    @pl.when(n_chunks > 0)
    def _():
        expert(xga_ref, za_ref)
        gather(1, xgb_ref)

    def tick_pair(p, carry):
        t = 2 * p + 1
        expert(xgb_ref, zb_ref)
        gather(t + 1, xga_ref)
        combine(t - 1, za_ref)

        @pl.when(t + 1 < n_chunks)
        def _():
            expert(xga_ref, za_ref)
            gather(t + 2, xgb_ref)
            combine(t, zb_ref)

        return carry

    lax.fori_loop(0, lax.shift_right_logical(n_chunks, 1), tick_pair, 0)

    @pl.when((n_chunks & 1) == 1)
    def _():
        combine(n_chunks - 1, za_ref)

    @pl.when(jnp.logical_and(n_chunks > 0, (n_chunks & 1) == 0))
    def _():
        combine(n_chunks - 1, zb_ref)


def _moe_call(l, h2t, ids, sw, seg_off, n_gran, w):
    smem = lambda n: pl.BlockSpec((MOE_TILES * n,), lambda j, e: (j,), memory_space=pltpu.SMEM,
                                  pipeline_mode=pl.Buffered(1))
    blk = pl.BlockSpec((MOE_TOK, SUB, LANES), lambda j, e: (j, 0, 0), pipeline_mode=pl.Buffered(1))
    return pl.pallas_call(
        _moe_kernel,
        grid=(N_MOE_BLOCKS, N_EXPERTS),
        in_specs=[
            smem(SLOTS), smem(SLOTS), smem(LANES), smem(LANES),
            blk,
            pl.BlockSpec((None, None, D_MODEL, 2 * D_EXPERT), lambda j, e: (l, e, 0, 0)),
            pl.BlockSpec((None, None, 1, 2 * D_EXPERT), lambda j, e: (l, e, 0, 0)),
            pl.BlockSpec((None, None, D_EXPERT, D_MODEL), lambda j, e: (l, e, 0, 0)),
            pl.BlockSpec((None, None, 1, D_MODEL), lambda j, e: (l, e, 0, 0)),
        ],
        out_specs=blk,
        out_shape=jax.ShapeDtypeStruct((N_TOK, SUB, LANES), F32),
        scratch_shapes=[pltpu.VMEM((MOE_CHUNK * SUB, LANES), F32)] * 4
        + [pltpu.SMEM((MAX_GRAN + 2 * CHUNK_GRAN,), jnp.int32)],
        compiler_params=pltpu.CompilerParams(
            dimension_semantics=("arbitrary", "arbitrary"), vmem_limit_bytes=VMEM_LIMIT),
        name="moe",
    )(ids.reshape(-1), sw.reshape(-1), seg_off.reshape(-1), n_gran.reshape(-1),
      h2t.reshape(N_TOK, SUB, LANES), w["exp_w1"], w["exp_b1"], w["exp_w2"], w["exp_b2"]
      ).reshape(N_TOK * SUB, LANES)


def _axial_tables(n_tokens, dim):
    rows = n_tokens // GRID_W
    row = jnp.repeat(jnp.arange(rows), GRID_W)
    col = jnp.tile(jnp.arange(GRID_W), rows)
    half = dim // 2
    inv = 1.0 / (ROPE_BASE ** (jnp.arange(0, half, 2, dtype=F32) / half))

    def axis_angles(pos):
        ang = pos.astype(F32)[:, None] * inv[None, :]
        return jnp.concatenate([ang, ang], axis=-1)

    ang = jnp.concatenate([axis_angles(row), axis_angles(col)], axis=-1)
    return jnp.cos(ang), jnp.sin(ang)


def _rope_slot_tables(dim, lane0, copies):
    cos, sin = _axial_tables(DEC_SEQ, dim)
    quarter = dim // 4
    sign = jnp.where((jnp.arange(dim) % (dim // 2)) < quarter, -1.0, 1.0)
    sin = sin * sign
    cos_slot = jnp.ones((DEC_SEQ, LANES), F32)
    sin_slot = jnp.zeros((DEC_SEQ, LANES), F32)
    for c in range(copies):
        cos_slot = cos_slot.at[:, lane0 + c * dim:lane0 + (c + 1) * dim].set(cos)
        sin_slot = sin_slot.at[:, lane0 + c * dim:lane0 + (c + 1) * dim].set(sin)
    ident = (jnp.ones((1, TILE, LANES), F32), jnp.zeros((1, TILE, LANES), F32))
    cos_t = jnp.concatenate([ident[0], cos_slot.reshape(LAT_TILES_PER_SEQ, TILE, LANES)], axis=0)
    sin_t = jnp.concatenate([ident[1], sin_slot.reshape(LAT_TILES_PER_SEQ, TILE, LANES)], axis=0)
    return cos_t, sin_t


def _half_swap(block):
    dst = jnp.arange(2 * LANES)
    src = jnp.where((dst % block) < block // 2, dst + block // 2, dst - block // 2)
    return (jnp.arange(2 * LANES)[:, None] == src[None, :]).astype(BF16)


def _pad_last(a, width):
    return jnp.pad(a, [(0, 0)] * (a.ndim - 1) + [(0, width - a.shape[-1])])


def _prepare(p):
    L = DEPTH
    w = {}
    w_in = p["w_in"]
    kr_slot = jnp.pad(w_in[:, :, OFF_KVA + KV_LORA:OFF_LRU_X], ((0, 0), (0, 0), (MLA_NOPE, LANES - MLA_QK)))
    w["w_in_a"] = w_in[:, :, OFF_QA:OFF_KVA + KV_LORA].astype(BF16)
    w["w_in_kr"] = kr_slot.astype(BF16)
    w["w_in_c"] = w_in[:, :, OFF_LRU_X:IN_COLS].astype(BF16)
    w["wqb"] = _pad_last(p["w_q_b"].reshape(L, Q_LORA, MLA_HEADS, MLA_QK), LANES).reshape(
        L, Q_LORA, MLA_HEADS * LANES).astype(BF16)
    kvb = p["w_kv_b"].reshape(L, KV_LORA, MLA_HEADS, MLA_NOPE + MLA_V)
    w["wkvk"] = _pad_last(kvb[..., :MLA_NOPE], LANES).reshape(L, KV_LORA, MLA_HEADS * LANES).astype(BF16)
    w["wkvv"] = kvb[..., MLA_NOPE:].reshape(L, KV_LORA, MLA_HEADS * MLA_V).astype(BF16)
    row = lambda a: a[:, None, :]
    w["norm1"] = row(p["norm1_g"])
    w["norm2"] = row(p["norm2_g"])
    w["qa_norm"] = row(p["mla_qa_norm"])
    w["kva_norm"] = row(p["mla_kva_norm"])
    w["qn"] = row(_pad_last(p["mla_qn"], LANES))
    w["kn"] = row(_pad_last(p["mla_kn"], LANES))
    w["dqn"] = row(jnp.tile(p["diff_qn"], (1, 2)))
    w["dkn"] = row(jnp.tile(p["diff_kn"], (1, 2)))
    w["cos_m"], w["sin_m"] = _rope_slot_tables(MLA_ROPE, MLA_NOPE, 1)
    w["cos_d"], w["sin_d"] = _rope_slot_tables(DIFF_DH, 0, 2)
    w["swap_m"] = _half_swap(MLA_ROPE // 2)
    w["swap_d"] = _half_swap(DIFF_DH // 2)
    w["conv_w"] = p["lru_conv_w"]
    w["conv_b"] = row(p["lru_conv_b"])
    per = (D_RNN // 2) // LRU_BW
    gw = p["lru_gate_w"].reshape(L, 4, 2, per, LRU_BW, LRU_BW)
    eye = jnp.eye(per, dtype=F32)
    w["wbd"] = jnp.einsum("lkhacd,ab->lkhacbd", gw, eye).reshape(L, 4, 2, D_RNN // 2, D_RNN // 2).astype(BF16)
    w["gate_b"] = p["lru_gate_b"].reshape(L, 4, D_RNN)
    w["lam"] = p["lru_lambda"]
    w["w_o_mla"] = p["w_o_mla"].astype(BF16)
    w["w_o_lru"] = p["w_o_lru"].astype(BF16)
    w["w_o_diff"] = p["w_o_diff"].astype(BF16)
    w["w_merge"] = p["w_merge"].astype(BF16)
    w["b_merge"] = row(p["b_merge"])
    w["w_out"] = p["w_out"].astype(BF16)
    w["router_w"] = _pad_last(p["router_w"], LANES)
    w["router_b"] = row(_pad_last(p["router_b"], LANES))
    w["exp_w1"] = p["exp_w1"]
    w["exp_b1"] = p["exp_b1"][:, :, None, :]
    w["exp_w2"] = p["exp_w2"]
    w["exp_b2"] = p["exp_b2"][:, :, None, :]
    w["diff_lambda"] = p["diff_lambda"]
    w["diff_subln"] = row(p["diff_subln"])
    return w


def kernel(x_prompt, x_sample, cache_mla_ckv, cache_mla_krope, state_lru, cache_diff_k, cache_diff_v,
           c, c_ctx, ada_w, ada_b, norm1_g, norm2_g, w_in, mla_qa_norm, w_q_b, mla_kva_norm, w_kv_b,
           mla_qn, mla_kn, w_o_mla, lru_conv_w, lru_conv_b, lru_gate_w, lru_gate_b, lru_lambda, w_o_lru,
           diff_qn, diff_kn, diff_lambda, diff_subln, w_o_diff, w_merge, b_merge, w_out,
           router_w, router_b, exp_w1, exp_b1, exp_w2, exp_b2):
    params = dict(
        norm1_g=norm1_g, norm2_g=norm2_g, w_in=w_in, mla_qa_norm=mla_qa_norm, w_q_b=w_q_b,
        mla_kva_norm=mla_kva_norm, w_kv_b=w_kv_b, mla_qn=mla_qn, mla_kn=mla_kn, w_o_mla=w_o_mla,
        lru_conv_w=lru_conv_w, lru_conv_b=lru_conv_b, lru_gate_w=lru_gate_w, lru_gate_b=lru_gate_b,
        lru_lambda=lru_lambda, w_o_lru=w_o_lru, diff_qn=diff_qn, diff_kn=diff_kn, diff_lambda=diff_lambda,
        diff_subln=diff_subln, w_o_diff=w_o_diff, w_merge=w_merge, b_merge=b_merge, w_out=w_out,
        router_w=router_w, router_b=router_b, exp_w1=exp_w1, exp_b1=exp_b1, exp_w2=exp_w2, exp_b2=exp_b2)
    w = _prepare(params)

    cond = jnp.concatenate(
        [c, c_ctx[None, :], jnp.zeros((COND_ROWS - DEC_BATCH - 1, D_MODEL), F32)], axis=0)
    mod = _ada_call(cond, ada_w, ada_b[:, None, :])
    mod3 = mod.reshape(DEPTH * COND_ROWS, 1, 6 * D_MODEL)

    place = jnp.pad(jnp.eye(MLA_ROPE, dtype=F32), ((0, 0), (MLA_NOPE, LANES - MLA_QK))).astype(BF16)
    k_ctx, v_ctx = _kvx_call(cache_mla_ckv, cache_mla_krope, w["wkvk"], w["wkvv"], w["kn"], place)
    dk_ctx = cache_diff_k.reshape(DEC_BATCH, DEPTH, PAST_LEN, 512)
    dv_ctx = cache_diff_v.reshape(DEC_BATCH, DEPTH, PAST_LEN, 512)

    x = jnp.concatenate([x_prompt.reshape(N_CTX_TOK, D_MODEL), x_sample.reshape(-1, D_MODEL)], axis=0)
    h0_ctx = jnp.zeros((BATCH, 2, D_RNN), F32)
    lat_blk = N_CTX_TOK // DEC_SEQ
    new_ckv, new_krope, new_lru, new_dk, new_dv = [], [], [], [], []
    moe_out = None
    for l in range(DEPTH):
        lam_init = 0.8 - 0.6 * math.exp(-0.3 * l)
        x, (qm, km, vm, qd, kd, vd, lx, lg, ckv_o, kro_o, dk_o, dv_o) = _front_call(l, x, moe_out, mod3, w)

        u_ctx, hf_ctx = _lru_call(l, lx, lg, h0_ctx, w, BATCH, SEQ, 0)
        u_lat, _ = _lru_call(l, lx, lg, state_lru[:, l], w, DEC_BATCH, DEC_SEQ, lat_blk)

        om_ctx = _attn_call(_mla_attn_kernel, "mla_ctx", qm, [(km, vm, "tok")], [], [], 0, BATCH, 1, 512)
        om_lat = _attn_call(_mla_attn_kernel, "mla_lat", qm,
                            [(k_ctx, v_ctx, lambda b: (l, b, 0, 0)), (km, vm, "tok")], [], [],
                            lat_blk, DEC_BATCH, LAT_TILES_PER_SEQ, 512)

        dk_extra = [w["diff_lambda"], w["diff_subln"]]
        dk_specs = lambda: [pl.BlockSpec((None, 4, DIFF_DH), lambda b: (l, 0, 0)),
                            pl.BlockSpec((None, 1, LANES), lambda b: (l, 0, 0))]
        dkern = lambda n_seg, *refs: _diff_attn_kernel(n_seg, lam_init, *refs)
        od_ctx = _attn_call(dkern, "diff_ctx", qd, [(kd, vd, "tok")], dk_extra, dk_specs(),
                            0, BATCH, 1, 512)
        od_lat = _attn_call(dkern, "diff_lat", qd,
                            [(dk_ctx, dv_ctx, lambda b: (b, l, 0, 0)), (kd, vd, "tok")], dk_extra, dk_specs(),
                            lat_blk, DEC_BATCH, LAT_TILES_PER_SEQ, 512)

        x, h2t, ids, sw, seg_off, n_gran = _back_call(
            l, x, mod3, (om_ctx, om_lat), (u_ctx, u_lat), (od_ctx, od_lat), w)
        moe_out = _moe_call(l, h2t, ids, sw, seg_off, n_gran, w)

        new_ckv.append(ckv_o.reshape(BATCH, SEQ, KV_LORA))
        new_krope.append(kro_o.reshape(BATCH, SEQ, MLA_ROPE))
        new_lru.append(hf_ctx)
        new_dk.append(dk_o.reshape(BATCH, SEQ, DIFF_HEADS, 2, DIFF_DH))
        new_dv.append(dv_o.reshape(BATCH, SEQ, DIFF_HEADS, DIFF_DV))

    x_ctx, x_lat = _residual_call(DEPTH - 1, x, moe_out, mod3)
    xp = x_ctx.reshape(BATCH, SEQ, D_MODEL)
    xs = x_lat.reshape(DEC_BATCH, DEC_SEQ, D_MODEL)
    return (xp, xs, jnp.stack(new_ckv, axis=1), jnp.stack(new_krope, axis=1), jnp.stack(new_lru, axis=1),
            jnp.stack(new_dk, axis=1), jnp.stack(new_dv, axis=1))
```

```python
import functools
import math

import jax
import jax.numpy as jnp
from jax import lax
from jax.experimental import pallas as pl
from jax.experimental.pallas import tpu as pltpu

F32 = jnp.float32
BF16 = jnp.bfloat16

D_MODEL = 1024
BATCH = 16
SEQ = 256
DEPTH = 4
DEC_BATCH = 8
DEC_SEQ = 1024
PAST_LEN = 256
GRID_W = 64
ROPE_BASE = 10000.0
NORM_EPS = 1e-6
LOG2_E = math.log2(math.e)

MLA_HEADS = 8
MLA_NOPE = 64
MLA_ROPE = 32
MLA_QK = MLA_NOPE + MLA_ROPE
MLA_V = 64
Q_LORA = 384
KV_LORA = 256
D_RNN = 512
LRU_BLOCKS = 8
LRU_BW = D_RNN // LRU_BLOCKS
CONV_W = 4
LRU_C = 8.0
DIFF_HEADS = 4
DIFF_DH = 64
DIFF_DV = 2 * DIFF_DH
N_EXPERTS = 32
TOP_K = 4
D_EXPERT = 512
SWIGLU_LIMIT = 7.0
SWIGLU_ALPHA = 1.702

OFF_QA = 0
OFF_KVA = OFF_QA + Q_LORA
OFF_LRU_X = OFF_KVA + KV_LORA + MLA_ROPE
OFF_LRU_G = OFF_LRU_X + D_RNN
OFF_DQ = OFF_LRU_G + D_RNN
OFF_DK = OFF_DQ + DIFF_HEADS * 2 * DIFF_DH
OFF_DV = OFF_DK + DIFF_HEADS * 2 * DIFF_DH
IN_COLS = OFF_DV + DIFF_HEADS * DIFF_DV

LANES = 128
TILE = 256
N_CTX_TILES = BATCH * SEQ // TILE
LAT_TILES_PER_SEQ = DEC_SEQ // TILE
N_LAT_TILES = DEC_BATCH * LAT_TILES_PER_SEQ
N_TILES = N_CTX_TILES + N_LAT_TILES
N_CTX_TOK = BATCH * SEQ
N_TOK = N_TILES * TILE
COND_ROWS = 16
CTX_COND_ROW = DEC_BATCH

P_QA = 0
P_CKV = P_QA + Q_LORA
P_KR = P_CKV + KV_LORA
P_LX = P_KR + LANES
P_LG = P_LX + D_RNN
P_DQ = P_LG + D_RNN
P_DK = P_DQ + 512
P_DV = P_DK + 512
P_COLS = P_DV + 512

SUB = 8
MOE_TILES = 16
MOE_TOK = MOE_TILES * TILE
N_MOE_BLOCKS = N_TOK // MOE_TOK
GRAN = 8
SLOTS = TILE * TOP_K + N_EXPERTS * GRAN
MOE_CHUNK = 256
CHUNK_GRAN = MOE_CHUNK // GRAN
CHUNK_SHIFT = CHUNK_GRAN.bit_length() - 1
MAX_GRAN = MOE_TILES * SLOTS // GRAN
VMEM_LIMIT = 56 * 1024 * 1024


def _cond_row(i):
    return jnp.where(i < N_CTX_TILES, CTX_COND_ROW, (i - N_CTX_TILES) // LAT_TILES_PER_SEQ)


def _rope_idx(i):
    return jnp.where(i < N_CTX_TILES, 0, 1 + (i - N_CTX_TILES) % LAT_TILES_PER_SEQ)


def _ctx_out_idx(i):
    return jnp.minimum(i, N_CTX_TILES - 1)


def _const_spec(shape, index):
    return pl.BlockSpec(shape, lambda *_: index, pipeline_mode=pl.Buffered(1))


def _dot(a, b):
    return jnp.dot(a, b, preferred_element_type=F32)


def _dot_nt(a, b):
    return lax.dot_general(a, b, (((1,), (1,)), ((), ())), preferred_element_type=F32)


def _rms(x, n=None):
    n = x.shape[-1] if n is None else n
    ss = jnp.sum(x * x, axis=-1, keepdims=True)
    return x * lax.rsqrt(ss * (1.0 / n) + NORM_EPS)


def _rope_pair(ya, yb, cos, sin_signed, swap2):
    rot = _dot(jnp.concatenate([ya, yb], axis=-1).astype(BF16), swap2)
    return ya * cos + rot[:, :LANES] * sin_signed, yb * cos + rot[:, LANES:] * sin_signed


def _rms_factor(x, n):
    return lax.rsqrt(jnp.sum(x * x, axis=-1, keepdims=True) * (1.0 / n) + NORM_EPS)


def _sigmoid(x):
    return 0.5 * jnp.tanh(0.5 * x) + 0.5


def _lane_iota(shape):
    return lax.broadcasted_iota(jnp.int32, shape, len(shape) - 1)


def _ada_kernel(cond_ref, w_ref, b_ref, o_ref):
    c = cond_ref[...]
    s = c * jax.nn.sigmoid(c)
    o_ref[...] = _dot(s.astype(BF16), w_ref[...].astype(BF16)) + b_ref[...]


def _ada_call(cond, ada_w, ada_b3):
    cb = 1024
    return pl.pallas_call(
        _ada_kernel,
        grid=(DEPTH, 6 * D_MODEL // cb),
        in_specs=[
            pl.BlockSpec((COND_ROWS, D_MODEL), lambda l, j: (0, 0)),
            pl.BlockSpec((None, D_MODEL, cb), lambda l, j: (l, 0, j)),
            pl.BlockSpec((None, 1, cb), lambda l, j: (l, 0, j)),
        ],
        out_specs=pl.BlockSpec((None, COND_ROWS, cb), lambda l, j: (l, 0, j)),
        out_shape=jax.ShapeDtypeStruct((DEPTH, COND_ROWS, 6 * D_MODEL), F32),
        compiler_params=pltpu.CompilerParams(
            dimension_semantics=("arbitrary", "arbitrary"), vmem_limit_bytes=VMEM_LIMIT),
        name="ada_mod",
    )(cond, ada_w, ada_b3)


def _mla_k_heads(kn, krs, kn_gain, cos, sin, swap, store):
    for hp in range(MLA_HEADS // 2):
        heads = (2 * hp, 2 * hp + 1)
        raw = [kn[:, hd * LANES:(hd + 1) * LANES] + krs for hd in heads]
        ys = [kh * kn_gain for kh in raw]
        if cos is not None:
            ys = _rope_pair(ys[0], ys[1], cos, sin, swap)
        for hd, kh, y in zip(heads, raw, ys):
            store(hd, y * _rms_factor(kh, MLA_QK))


def _kvx_kernel(ckv_ref, kr_ref, wk_ref, wv_ref, kn_ref, place_ref, k_ref, v_ref):
    c = ckv_ref[...].astype(BF16)
    kn = _dot(c, wk_ref[...])
    v_ref[...] = _dot(c, wv_ref[...]).astype(BF16)
    krs = _dot(kr_ref[...].astype(BF16), place_ref[...])

    def store(hd, kh):
        k_ref[:, hd * LANES:(hd + 1) * LANES] = kh.astype(BF16)

    _mla_k_heads(kn, krs, kn_ref[...], None, None, None, store)


def _kvx_call(cache_ckv, cache_krope, wkvk, wkvv, kn_pad, place):
    return pl.pallas_call(
        _kvx_kernel,
        grid=(DEPTH, DEC_BATCH),
        in_specs=[
            pl.BlockSpec((None, None, PAST_LEN, KV_LORA), lambda l, b: (b, l, 0, 0)),
            pl.BlockSpec((None, None, PAST_LEN, MLA_ROPE), lambda l, b: (b, l, 0, 0)),
            pl.BlockSpec((None, KV_LORA, MLA_HEADS * LANES), lambda l, b: (l, 0, 0)),
            pl.BlockSpec((None, KV_LORA, MLA_HEADS * MLA_V), lambda l, b: (l, 0, 0)),
            pl.BlockSpec((None, 1, LANES), lambda l, b: (l, 0, 0)),
            pl.BlockSpec((MLA_ROPE, LANES), lambda l, b: (0, 0)),
        ],
        out_specs=[
            pl.BlockSpec((None, None, PAST_LEN, MLA_HEADS * LANES), lambda l, b: (l, b, 0, 0)),
            pl.BlockSpec((None, None, PAST_LEN, MLA_HEADS * MLA_V), lambda l, b: (l, b, 0, 0)),
        ],
        out_shape=[
            jax.ShapeDtypeStruct((DEPTH, DEC_BATCH, PAST_LEN, MLA_HEADS * LANES), BF16),
            jax.ShapeDtypeStruct((DEPTH, DEC_BATCH, PAST_LEN, MLA_HEADS * MLA_V), BF16),
        ],
        compiler_params=pltpu.CompilerParams(
            dimension_semantics=("arbitrary", "arbitrary"), vmem_limit_bytes=VMEM_LIMIT),
        name="ctx_kv_expand",
    )(cache_ckv, cache_krope, wkvk, wkvv, kn_pad, place)


def _modulated(x, gain, shift, scale):
    return _rms(x) * gain * (1.0 + scale) + shift


def _untile(t_ref):
    rows = t_ref.shape[0] // SUB
    return jnp.concatenate([t_ref[pl.ds(s, rows, stride=SUB), :] for s in range(SUB)], axis=-1)


def _moe_residual(x1_ref, yt_ref, modp_ref):
    return x1_ref[...] + modp_ref[:, 5 * D_MODEL:6 * D_MODEL] * _untile(yt_ref)


def _front_kernel(has_moe, *refs):
    if has_moe:
        x1_ref, yt_ref, modp_ref = refs[:3]
        refs = refs[3:]
    else:
        x_ref = refs[0]
        refs = refs[1:]
    (mod_ref, n1_ref, wa_ref, wkr_ref, wc_ref, qan_ref, wqb_ref, kvan_ref, wkvk_ref, wkvv_ref,
     qn_ref, kn_ref, dqn_ref, dkn_ref, cosm_ref, sinm_ref, cosd_ref, sind_ref, swapm_ref, swapd_ref) = refs[:20]
    refs = refs[20:]
    if has_moe:
        xo_ref = refs[0]
        refs = refs[1:]
    (qm_ref, km_ref, vm_ref, qd_ref, kd_ref, vd_ref, lx_ref, lg_ref,
     ckv_ref, kro_ref, dko_ref, dvo_ref) = refs
    if has_moe:
        x = _moe_residual(x1_ref, yt_ref, modp_ref)
        xo_ref[...] = x
    else:
        x = x_ref[...]
    mod = mod_ref[...]
    h = _modulated(x, n1_ref[...], mod[:, 0:D_MODEL], mod[:, D_MODEL:2 * D_MODEL])
    hb = h.astype(BF16)

    def proj(col, width):
        if col < P_KR:
            w_cols = wa_ref[:, col:col + width]
        elif col == P_KR:
            w_cols = wkr_ref[...]
        else:
            w_cols = wc_ref[:, col - P_LX:col - P_LX + width]
        return _dot(hb, w_cols)

    lane = _lane_iota((1, LANES))
    cosm, sinm, swap_m = cosm_ref[...], sinm_ref[...], swapm_ref[...]
    cosd, sind, swap_d = cosd_ref[...], sind_ref[...], swapd_ref[...]

    qa = _rms(proj(P_QA, Q_LORA)) * qan_ref[...]
    q = _dot(qa.astype(BF16), wqb_ref[...])
    q_scale = MLA_QK ** -0.5 * LOG2_E
    for hp in range(MLA_HEADS // 2):
        heads = (2 * hp, 2 * hp + 1)
        raw = [q[:, hd * LANES:(hd + 1) * LANES] for hd in heads]
        ys = _rope_pair(raw[0] * qn_ref[...], raw[1] * qn_ref[...], cosm, sinm, swap_m)
        for hd, qh, y in zip(heads, raw, ys):
            qm_ref[:, hd * LANES:(hd + 1) * LANES] = (y * (_rms_factor(qh, MLA_QK) * q_scale)).astype(BF16)

    ckv = _rms(proj(P_CKV, KV_LORA)) * kvan_ref[...]
    krs = proj(P_KR, LANES)
    cb = ckv.astype(BF16)
    vm_ref[...] = _dot(cb, wkvv_ref[...]).astype(BF16)
    kn = _dot(cb, wkvk_ref[...])

    def store_k(hd, kh):
        km_ref[:, hd * LANES:(hd + 1) * LANES] = kh.astype(BF16)

    _mla_k_heads(kn, krs, kn_ref[...], cosm, sinm, swap_m, store_k)

    lx_ref[...] = proj(P_LX, D_RNN)
    lg_ref[...] = proj(P_LG, D_RNN)

    low = lane < DIFF_DH

    def pair_factor(t):
        sq = t * t
        s_all = jnp.sum(sq, axis=-1, keepdims=True)
        s_lo = jnp.sum(jnp.where(low, sq, 0.0), axis=-1, keepdims=True)
        r_lo = lax.rsqrt(s_lo * (1.0 / DIFF_DH) + NORM_EPS)
        r_hi = lax.rsqrt((s_all - s_lo) * (1.0 / DIFF_DH) + NORM_EPS)
        return jnp.where(low, r_lo, r_hi)

    dq = proj(P_DQ, 512)
    dk = proj(P_DK, 512)
    d_scale = DIFF_DH ** -0.5 * LOG2_E
    k_heads = []
    for hp in range(DIFF_HEADS // 2):
        sls = [slice(hd * LANES, (hd + 1) * LANES) for hd in (2 * hp, 2 * hp + 1)]
        qs = _rope_pair(dq[:, sls[0]] * dqn_ref[...], dq[:, sls[1]] * dqn_ref[...], cosd, sind, swap_d)
        ks = _rope_pair(dk[:, sls[0]] * dkn_ref[...], dk[:, sls[1]] * dkn_ref[...], cosd, sind, swap_d)
        for sl, qh, kh in zip(sls, qs, ks):
            qd_ref[:, sl] = (qh * (pair_factor(dq[:, sl]) * d_scale)).astype(BF16)
            kh = kh * pair_factor(dk[:, sl])
            kd_ref[:, sl] = kh.astype(BF16)
            k_heads.append(kh)
    dv = proj(P_DV, 512)
    vd_ref[...] = dv.astype(BF16)

    @pl.when(pl.program_id(0) < N_CTX_TILES)
    def _():
        ckv_ref[...] = ckv
        kro_ref[...] = krs[:, MLA_NOPE:MLA_NOPE + MLA_ROPE]
        for hd in range(DIFF_HEADS):
            dko_ref[:, hd * LANES:(hd + 1) * LANES] = k_heads[hd]
        dvo_ref[...] = dv


def _mod_spec(l):
    return pl.BlockSpec((None, 1, 6 * D_MODEL), lambda i: (l * COND_ROWS + _cond_row(i), 0, 0))


def _tile_spec():
    return pl.BlockSpec((TILE * SUB, LANES), lambda i: (i, 0))


def _front_call(l, x, moe_out, mod3, w):
    tok = lambda width: pl.BlockSpec((TILE, width), lambda i: (i, 0))
    ctx = lambda width: pl.BlockSpec((TILE, width), lambda i: (_ctx_out_idx(i), 0))
    lay = lambda *shape: _const_spec((None,) + shape, (l,) + (0,) * len(shape))
    rope = pl.BlockSpec((None, TILE, LANES), lambda i: (_rope_idx(i), 0, 0))
    swap = pl.BlockSpec((2 * LANES, 2 * LANES), lambda i: (0, 0))
    n_ctx_rows = N_CTX_TOK
    has_moe = moe_out is not None
    if has_moe:
        lead_specs = [tok(D_MODEL), _tile_spec(), _mod_spec(l - 1)]
        lead_args = [x, moe_out, mod3]
        x_out_specs = [tok(D_MODEL)]
        x_out_shape = [jax.ShapeDtypeStruct((N_TOK, D_MODEL), F32)]
    else:
        lead_specs, lead_args, x_out_specs, x_out_shape = [tok(D_MODEL)], [x], [], []
    outs = pl.pallas_call(
        functools.partial(_front_kernel, has_moe),
        grid=(N_TILES,),
        in_specs=lead_specs + [
            _mod_spec(l),
            lay(1, D_MODEL), lay(D_MODEL, P_KR), lay(D_MODEL, LANES), lay(D_MODEL, P_COLS - P_LX), lay(1, Q_LORA), lay(Q_LORA, MLA_HEADS * LANES),
            lay(1, KV_LORA), lay(KV_LORA, MLA_HEADS * LANES), lay(KV_LORA, MLA_HEADS * MLA_V),
            lay(1, LANES), lay(1, LANES), lay(1, LANES), lay(1, LANES),
            rope, rope, rope, rope, swap, swap,
        ],
        out_specs=x_out_specs + [
            tok(1024), tok(1024), tok(512), tok(512), tok(512), tok(512), tok(512), tok(512),
            ctx(KV_LORA), ctx(MLA_ROPE), ctx(512), ctx(512),
        ],
        out_shape=x_out_shape + [
            jax.ShapeDtypeStruct((N_TOK, 1024), BF16), jax.ShapeDtypeStruct((N_TOK, 1024), BF16),
            jax.ShapeDtypeStruct((N_TOK, 512), BF16), jax.ShapeDtypeStruct((N_TOK, 512), BF16),
            jax.ShapeDtypeStruct((N_TOK, 512), BF16), jax.ShapeDtypeStruct((N_TOK, 512), BF16),
            jax.ShapeDtypeStruct((N_TOK, 512), F32), jax.ShapeDtypeStruct((N_TOK, 512), F32),
            jax.ShapeDtypeStruct((n_ctx_rows, KV_LORA), F32),
            jax.ShapeDtypeStruct((n_ctx_rows, MLA_ROPE), F32),
            jax.ShapeDtypeStruct((n_ctx_rows, 512), F32),
            jax.ShapeDtypeStruct((n_ctx_rows, 512), F32),
        ],
        compiler_params=pltpu.CompilerParams(
            dimension_semantics=("arbitrary",), vmem_limit_bytes=VMEM_LIMIT),
        name="front",
    )(*lead_args, mod3, w["norm1"], w["w_in_a"], w["w_in_kr"], w["w_in_c"], w["qa_norm"], w["wqb"], w["kva_norm"], w["wkvk"], w["wkvv"],
      w["qn"], w["kn"], w["dqn"], w["dkn"], w["cos_m"], w["sin_m"], w["cos_d"], w["sin_d"],
      w["swap_m"], w["swap_d"])
    if has_moe:
        return outs[0], outs[1:]
    return x, outs


def _residual_kernel(x1_ref, yt_ref, modp_ref, oc_ref, ol_ref):
    x = _moe_residual(x1_ref, yt_ref, modp_ref)
    is_ctx = pl.program_id(0) < N_CTX_TILES

    @pl.when(is_ctx)
    def _():
        oc_ref[...] = x

    @pl.when(jnp.logical_not(is_ctx))
    def _():
        ol_ref[...] = x


def _residual_call(l, x1, moe_out, mod3):
    tok = pl.BlockSpec((TILE, D_MODEL), lambda i: (i, 0))
    return pl.pallas_call(
        _residual_kernel,
        grid=(N_TILES,),
        in_specs=[tok, _tile_spec(), _mod_spec(l)],
        out_specs=[
            pl.BlockSpec((TILE, D_MODEL), lambda i: (jnp.minimum(i, N_CTX_TILES - 1), 0)),
            pl.BlockSpec((TILE, D_MODEL), lambda i: (jnp.maximum(i - N_CTX_TILES, 0), 0)),
        ],
        out_shape=[
            jax.ShapeDtypeStruct((N_CTX_TOK, D_MODEL), F32),
            jax.ShapeDtypeStruct((N_TOK - N_CTX_TOK, D_MODEL), F32),
        ],
        compiler_params=pltpu.CompilerParams(
            dimension_semantics=("arbitrary",), vmem_limit_bytes=VMEM_LIMIT),
        name="moe_residual",
    )(x1, moe_out, mod3)


def _scan(a, b, h0, reverse):
    n = a.shape[0]
    row = lax.broadcasted_iota(jnp.int32, (n, 1), 0)
    d = 1
    while d < n:
        if reverse:
            valid = row < n - d
            shift = n - d
        else:
            valid = row >= d
            shift = d
        a_s = jnp.where(valid, pltpu.roll(a, shift, 0), 1.0)
        b_s = jnp.where(valid, pltpu.roll(b, shift, 0), 0.0)
        b = a * b_s + b
        a = a * a_s
        d *= 2
    return a * h0 + b


def _gelu_tanh(x):
    return 0.5 * x * (1.0 + jnp.tanh(math.sqrt(2.0 / math.pi) * (x + 0.044715 * x * x * x)))


N_LANE_TILES = D_RNN // LANES


def _strided_scan(a, b, h0, reverse, first, sa_ref, sb_ref, tot_ref):
    n = a.shape[0]
    g = n // SUB
    for c in range(N_LANE_TILES):
        sa_ref[pl.ds(c * n, n), :] = a[:, c * LANES:(c + 1) * LANES]
        sb_ref[pl.ds(c * n, n), :] = b[:, c * LANES:(c + 1) * LANES]
    order = range(SUB - 1, -1, -1) if reverse else range(SUB)
    row = lax.broadcasted_iota(jnp.int32, (g, 1), 0)
    finals = []
    for c in range(N_LANE_TILES):
        comps = []
        acc_a = acc_b = None
        for r in order:
            cls = pl.ds(c * n + r, g, stride=SUB)
            a_r, b_r = sa_ref[cls, :], sb_ref[cls, :]
            if acc_a is None:
                acc_a, acc_b = a_r, b_r
            else:
                acc_b = a_r * acc_b + b_r
                acc_a = a_r * acc_a
            comps.append((cls, acc_a, acc_b))
        h0c = h0[:, c * LANES:(c + 1) * LANES]
        hg = _scan(acc_a, acc_b, h0c, reverse)
        if reverse:
            carry = jnp.where(row == g - 1, h0c, pltpu.roll(hg, g - 1, 0))
            finals.append(hg[0:1, :])
        else:
            carry = jnp.where(row == 0, h0c, pltpu.roll(hg, 1, 0))
            finals.append(hg[g - 1:g, :])
        for cls, comp_a, comp_b in comps:
            h = comp_a * carry + comp_b
            tot_ref[cls, :] = h if first else tot_ref[cls, :] + h
    return jnp.concatenate(finals, axis=-1)


def _sqrt_pos(t):
    return jnp.where(t > 0.0, t * lax.rsqrt(t), 0.0)


def _lru_kernel(lx_ref, lg_ref, cw_ref, cb_ref, wbd_ref, gb_ref, lam_ref, h0_ref, u_ref, hf_ref,
                sa_ref, sb_ref, tot_ref):
    x = lx_ref[...]
    n = x.shape[0]
    row = lax.broadcasted_iota(jnp.int32, (n, 1), 0)
    cw = cw_ref[...]
    xr = cb_ref[...] + cw[2:3] * x
    xr = xr + cw[0:1] * jnp.where(row >= 2, pltpu.roll(x, 2, 0), 0.0)
    xr = xr + cw[1:2] * jnp.where(row >= 1, pltpu.roll(x, 1, 0), 0.0)
    xr = xr + cw[3:4] * jnp.where(row < n - 1, pltpu.roll(x, n - 1, 0), 0.0)
    xb = xr.astype(BF16)
    half = D_RNN // 2
    lam = lam_ref[...]
    h0 = h0_ref[...]
    gb = gb_ref[...]
    for d in range(2):
        pre = []
        for g in range(2):
            k = d * 2 + g
            p = jnp.concatenate([_dot(xb[:, :half], wbd_ref[k, 0]), _dot(xb[:, half:], wbd_ref[k, 1])],
                                axis=-1)
            pre.append(p + gb[k:k + 1])
        r = _sigmoid(pre[0])
        i = _sigmoid(pre[1])
        z = -lam[d:d + 1]
        softplus = jnp.maximum(z, 0.0) + jnp.log(1.0 + jnp.exp(-jnp.abs(z)))
        a = jnp.exp(-LRU_C * r * softplus)
        bx = _sqrt_pos(1.0 - a * a) * (i * xr)
        hf_ref[d:d + 1, :] = _strided_scan(a, bx, h0[d:d + 1], d == 1, d == 0, sa_ref, sb_ref, tot_ref)
    total = jnp.concatenate([tot_ref[pl.ds(c * n, n), :] for c in range(N_LANE_TILES)], axis=-1)
    u_ref[...] = (total * _gelu_tanh(lg_ref[...])).astype(BF16)


def _lru_call(l, lx, lg, h0, w, n_seq, seq_len, row_block0):
    seq = lambda: pl.BlockSpec((seq_len, D_RNN), lambda s: (row_block0 + s, 0))
    lay = lambda *shape: _const_spec((None,) + shape, (l,) + (0,) * len(shape))
    return pl.pallas_call(
        _lru_kernel,
        grid=(n_seq,),
        in_specs=[
            seq(), seq(), lay(CONV_W, D_RNN), lay(1, D_RNN), lay(4, 2, D_RNN // 2, D_RNN // 2),
            lay(4, D_RNN), lay(2, D_RNN),
            pl.BlockSpec((None, 2, D_RNN), lambda s: (s, 0, 0)),
        ],
        out_specs=[
            pl.BlockSpec((seq_len, D_RNN), lambda s: (s, 0)),
            pl.BlockSpec((None, 2, D_RNN), lambda s: (s, 0, 0)),
        ],
        out_shape=[
            jax.ShapeDtypeStruct((n_seq * seq_len, D_RNN), BF16),
            jax.ShapeDtypeStruct((n_seq, 2, D_RNN), F32),
        ],
        scratch_shapes=[pltpu.VMEM((N_LANE_TILES * seq_len, LANES), F32)] * 3,
        compiler_params=pltpu.CompilerParams(
            dimension_semantics=("arbitrary",), vmem_limit_bytes=VMEM_LIMIT),
        name="rglru_%d" % seq_len,
    )(lx, lg, w["conv_w"], w["conv_b"], w["wbd"], w["gate_b"], w["lam"], h0)


def _softmax_parts(scores):
    m = scores[0].max(axis=-1, keepdims=True)
    for s in scores[1:]:
        m = jnp.maximum(m, s.max(axis=-1, keepdims=True))
    es = [jnp.exp2(s - m) for s in scores]
    tot = es[0].sum(axis=-1, keepdims=True)
    for e in es[1:]:
        tot = tot + e.sum(axis=-1, keepdims=True)
    return es, tot


def _mla_attn_kernel(n_seg, q_ref, *refs):
    k_refs = refs[0:2 * n_seg:2]
    v_refs = refs[1:2 * n_seg:2]
    o_ref = refs[2 * n_seg]
    low = _lane_iota((1, LANES)) < MLA_V
    for pair in range(MLA_HEADS // 2):
        vs = slice(pair * LANES, (pair + 1) * LANES)
        out = None
        for hh in range(2):
            hs = slice((2 * pair + hh) * LANES, (2 * pair + hh + 1) * LANES)
            qh = q_ref[:, hs]
            es, tot = _softmax_parts([_dot_nt(qh, k[:, hs]) for k in k_refs])
            pv = None
            for e, v in zip(es, v_refs):
                t = _dot(e.astype(BF16), v[:, vs])
                pv = t if pv is None else pv + t
            oh = pv / tot
            out = oh if hh == 0 else jnp.where(low, out, oh)
        o_ref[:, vs] = out.astype(BF16)


def _diff_attn_kernel(n_seg, lam_init, q_ref, *refs):
    k_refs = refs[0:2 * n_seg:2]
    v_refs = refs[1:2 * n_seg:2]
    lam_ref, sub_ref, o_ref = refs[2 * n_seg:2 * n_seg + 3]
    lp = lam_ref[...]
    lam = (jnp.exp(jnp.sum(lp[0:1] * lp[1:2], axis=-1, keepdims=True))
           - jnp.exp(jnp.sum(lp[2:3] * lp[3:4], axis=-1, keepdims=True)) + lam_init)
    low = _lane_iota((1, LANES)) < DIFF_DH
    zero = jnp.zeros((), BF16)
    for hd in range(DIFF_HEADS):
        hs = slice(hd * LANES, (hd + 1) * LANES)
        qh = q_ref[:, hs]
        ks = [k[:, hs].astype(BF16) for k in k_refs]
        e0, t0 = _softmax_parts([_dot_nt(jnp.where(low, qh, zero), k) for k in ks])
        e1, t1 = _softmax_parts([_dot_nt(jnp.where(low, zero, qh), k) for k in ks])
        w0 = 1.0 / t0
        w1 = lam / t1
        o = None
        for a, b, v in zip(e0, e1, v_refs):
            t = _dot((a * w0 - b * w1).astype(BF16), v[:, hs].astype(BF16))
            o = t if o is None else o + t
        o = _rms(o) * sub_ref[...] * (1.0 - lam_init)
        o_ref[:, hs] = o.astype(BF16)


def _over_q_tiles(kernel, n_seg, tiles, q_ref, *refs):
    if tiles == 1:
        kernel(n_seg, q_ref, *refs)
        return

    n_rows = 2 * TILE if tiles % 2 == 0 else TILE

    def tile(t, carry):
        rows = pl.ds(pl.multiple_of(t * n_rows, n_rows), n_rows)
        kernel(n_seg, q_ref.at[rows], *refs[:-1], refs[-1].at[rows])
        return carry

    lax.fori_loop(0, tiles * TILE // n_rows, tile, 0)


def _attn_call(kernel, name, q, segs, extra, extra_specs, q_block0, n_seq, tiles_per_seq, out_width):
    seq_len = tiles_per_seq * TILE
    in_specs = [pl.BlockSpec((seq_len, q.shape[1]), lambda b: (q_block0 + b, 0))]
    args = [q]
    for k, v, kind in segs:
        for arr in (k, v):
            if kind == "tok":
                in_specs.append(pl.BlockSpec((seq_len, arr.shape[1]), lambda b: (q_block0 + b, 0)))
            else:
                in_specs.append(pl.BlockSpec((None, None) + arr.shape[2:], kind))
            args.append(arr)
    in_specs += extra_specs
    args += extra
    return pl.pallas_call(
        functools.partial(_over_q_tiles, kernel, len(segs), tiles_per_seq),
        grid=(n_seq,),
        in_specs=in_specs,
        out_specs=pl.BlockSpec((seq_len, out_width), lambda b: (b, 0)),
        out_shape=jax.ShapeDtypeStruct((n_seq * seq_len, out_width), BF16),
        compiler_params=pltpu.CompilerParams(
            dimension_semantics=("arbitrary",), vmem_limit_bytes=VMEM_LIMIT),
        name=name,
    )(*args)


def _back_kernel(x_ref, mod_ref, n1_ref, n2_ref, omc_ref, oml_ref, ulc_ref, ull_ref, odc_ref, odl_ref,
                 wom_ref, wol_ref, wod_ref, wmg_ref, bmg_ref, wout_ref, rw_ref, rb_ref,
                 x1_ref, h2t_ref, ids_ref, sw_ref, off_ref, ng_ref):
    i = pl.program_id(0)
    is_ctx = i < N_CTX_TILES
    x = x_ref[...]
    mod = mod_ref[...]
    seg = lambda j: mod[:, j * D_MODEL:(j + 1) * D_MODEL]
    hb = _modulated(x, n1_ref[...], seg(0), seg(1)).astype(BF16)
    pick = lambda c_ref, l_ref: jnp.where(is_ctx, c_ref[...], l_ref[...])
    branches = (_dot(pick(omc_ref, oml_ref), wom_ref[...]), _dot(pick(ulc_ref, ull_ref), wol_ref[...]),
                _dot(pick(odc_ref, odl_ref), wod_ref[...]))
    merged = None
    for j, o in enumerate(branches):
        cs = slice(j * D_MODEL, (j + 1) * D_MODEL)
        g = _sigmoid(_dot(hb, wmg_ref[:, cs]) + bmg_ref[:, cs])
        merged = g * o if merged is None else merged + g * o
    x1 = x + seg(2) * _dot(merged.astype(BF16), wout_ref[...])
    x1_ref[...] = x1
    h2 = _modulated(x1, n2_ref[...], seg(3), seg(4))
    for s in range(SUB):
        h2t_ref[pl.ds(s, TILE, stride=SUB), :] = h2[:, s * LANES:(s + 1) * LANES]

    h_hi = h2.astype(BF16)
    h_lo = (h2 - h_hi.astype(F32)).astype(BF16)
    rw = rw_ref[...]
    w_hi = rw.astype(BF16)
    w_lo = (rw - w_hi.astype(F32)).astype(BF16)
    logits = _dot(h_hi, w_hi) + (_dot(h_lo, w_hi) + _dot(h_hi, w_lo)) + rb_ref[...]
    lane = _lane_iota(logits.shape).astype(F32)
    neg = jnp.float32(-jnp.inf)
    work = jnp.where(lane < N_EXPERTS, logits, neg)
    sels, probs = [], []
    top = None
    for _ in range(TOP_K):
        m = work.max(axis=-1, keepdims=True)
        idx = jnp.min(jnp.where(work == m, lane, float(LANES)), axis=-1, keepdims=True)
        sel = lane == idx
        top = m if top is None else top
        sels.append(sel)
        probs.append(jnp.exp(m - top))
        work = jnp.where(sel, neg, work)
    denom = probs[0] + probs[1] + probs[2] + probs[3]

    onehot = jnp.zeros_like(logits)
    for sel in sels:
        onehot = onehot + jnp.where(sel, 1.0, 0.0)
    r_i = lax.broadcasted_iota(jnp.int32, (TILE, TILE), 0)
    c_i = lax.broadcasted_iota(jnp.int32, (TILE, TILE), 1)
    earlier = jnp.where(c_i < r_i, 1.0, 0.0).astype(BF16)
    rank = _dot(earlier, onehot.astype(BF16))
    count = jnp.sum(onehot, axis=0, keepdims=True)
    n_gran = jnp.floor((count + (GRAN - 1.0)) * (1.0 / GRAN))
    r_l = lax.broadcasted_iota(jnp.int32, (LANES, LANES), 0)
    c_l = lax.broadcasted_iota(jnp.int32, (LANES, LANES), 1)
    before = jnp.where(r_l < c_l, 1.0, 0.0).astype(BF16)
    seg_off = _dot(jnp.broadcast_to(n_gran * GRAN, (SUB, LANES)).astype(BF16), before)[0:1]
    slot_base = seg_off + rank
    slot_lane = lax.broadcasted_iota(jnp.int32, (1, SLOTS), 1).astype(F32)
    tok_col = (lax.broadcasted_iota(jnp.int32, (TILE, 1), 0) + (i % MOE_TILES) * TILE).astype(F32)
    id_acc = jnp.zeros((TILE, SLOTS), F32)
    w_acc = jnp.zeros((TILE, SLOTS), F32)
    for sel, p in zip(sels, probs):
        slot = jnp.sum(jnp.where(sel, slot_base, 0.0), axis=-1, keepdims=True)
        hit = slot_lane == slot
        id_acc = jnp.where(hit, tok_col, id_acc)
        w_acc = jnp.where(hit, p / denom, w_acc)
    ids_ref[...] = jnp.sum(id_acc, axis=0, keepdims=True).astype(jnp.int32)
    sw_ref[...] = jnp.sum(w_acc, axis=0, keepdims=True)
    off_ref[...] = seg_off.astype(jnp.int32)
    ng_ref[...] = n_gran.astype(jnp.int32)


def _back_call(l, x, mod3, om, ul, od, w):
    tok = lambda width: pl.BlockSpec((TILE, width), lambda i: (i, 0))
    ctx_in = pl.BlockSpec((TILE, 512), lambda i: (jnp.minimum(i, N_CTX_TILES - 1), 0))
    lat_in = pl.BlockSpec((TILE, 512), lambda i: (jnp.maximum(i - N_CTX_TILES, 0), 0))
    lay = lambda *shape: _const_spec((None,) + shape, (l,) + (0,) * len(shape))
    row = lambda width: pl.BlockSpec((None, 1, width), lambda i: (i, 0, 0))
    return pl.pallas_call(
        _back_kernel,
        grid=(N_TILES,),
        in_specs=[
            tok(D_MODEL), _mod_spec(l),
            lay(1, D_MODEL), lay(1, D_MODEL), ctx_in, lat_in, ctx_in, lat_in, ctx_in, lat_in,
            lay(512, D_MODEL), lay(512, D_MODEL), lay(512, D_MODEL),
            lay(D_MODEL, 3 * D_MODEL), lay(1, 3 * D_MODEL), lay(D_MODEL, D_MODEL),
            lay(D_MODEL, LANES), lay(1, LANES),
        ],
        out_specs=[tok(D_MODEL), _tile_spec(), row(SLOTS), row(SLOTS), row(LANES), row(LANES)],
        out_shape=[
            jax.ShapeDtypeStruct((N_TOK, D_MODEL), F32),
            jax.ShapeDtypeStruct((N_TOK * SUB, LANES), F32),
            jax.ShapeDtypeStruct((N_TILES, 1, SLOTS), jnp.int32),
            jax.ShapeDtypeStruct((N_TILES, 1, SLOTS), F32),
            jax.ShapeDtypeStruct((N_TILES, 1, LANES), jnp.int32),
            jax.ShapeDtypeStruct((N_TILES, 1, LANES), jnp.int32),
        ],
        compiler_params=pltpu.CompilerParams(
            dimension_semantics=("arbitrary",), vmem_limit_bytes=VMEM_LIMIT),
        name="back",
    )(x, mod3, w["norm1"], w["norm2"], om[0], om[1], ul[0], ul[1], od[0], od[1],
      w["w_o_mla"], w["w_o_lru"], w["w_o_diff"],
      w["w_merge"], w["b_merge"], w["w_out"], w["router_w"], w["router_b"])


def _moe_kernel(ids_ref, sw_ref, off_ref, ng_ref, h2t_ref, w1_ref, b1_ref, w2_ref, b2_ref,
                y_ref, xga_ref, xgb_ref, za_ref, zb_ref, gb_ref):
    e = pl.program_id(1)

    @pl.when(e == 0)
    def _():
        y_ref[...] = jnp.zeros_like(y_ref)
        for ref in (xga_ref, xgb_ref, za_ref, zb_ref):
            ref[...] = jnp.zeros_like(ref)

    n_total = 0
    for t in range(MOE_TILES):
        first = t * SLOTS + off_ref[t * LANES + e]

        def add_gran(g, pos, first=first):
            gb_ref[pos] = first + g * GRAN
            return pos + 1

        n_total = lax.fori_loop(0, ng_ref[t * LANES + e], add_gran, n_total)
    for i in range(2 * CHUNK_GRAN):
        gb_ref[n_total + i] = SLOTS - GRAN
    n_chunks = lax.shift_right_logical(n_total + (CHUNK_GRAN - 1), CHUNK_SHIFT)

    def slot_rows(j):
        return pl.ds(j * SUB, SUB)

    def gather(t, xg_ref):
        for g in range(CHUNK_GRAN):
            base = gb_ref[t * CHUNK_GRAN + g]
            for u in range(GRAN):
                xg_ref[slot_rows(g * GRAN + u), :] = h2t_ref[ids_ref[base + u]]

    def expert(xg_ref, z_ref):
        xb = jnp.concatenate([xg_ref[pl.ds(s, MOE_CHUNK, stride=SUB), :] for s in range(SUB)],
                             axis=-1).astype(BF16)
        gu = _dot(xb, w1_ref[...].astype(BF16)) + b1_ref[...]
        gate = jnp.minimum(gu[:, :D_EXPERT], SWIGLU_LIMIT)
        up = jnp.clip(gu[:, D_EXPERT:], -SWIGLU_LIMIT, SWIGLU_LIMIT)
        act = (up + 1.0) * (gate * _sigmoid(SWIGLU_ALPHA * gate))
        y = _dot(act.astype(BF16), w2_ref[...].astype(BF16)) + b2_ref[...]
        for s in range(SUB):
            z_ref[pl.ds(s, MOE_CHUNK, stride=SUB), :] = y[:, s * LANES:(s + 1) * LANES]

    def combine(t, z_ref):
        for g in range(CHUNK_GRAN):
            base = gb_ref[t * CHUNK_GRAN + g]
            toks = [ids_ref[base + u] for u in range(GRAN)]
            new = [y_ref[toks[u]] + sw_ref[base + u] * z_ref[slot_rows(g * GRAN + u), :] for u in range(GRAN)]
            for u in reversed(range(GRAN)):
                y_ref[toks[u]] = new[u]

    gather(0, xga_ref)

    @pl.when(n_chunks > 0)
    def _():
        expert(xga_ref, za_ref)
        gather(1, xgb_ref)

    def tick_pair(p, carry):
        t = 2 * p + 1
        expert(xgb_ref, zb_ref)
        gather(t + 1, xga_ref)
        combine(t - 1, za_ref)

        @pl.when(t + 1 < n_chunks)
        def _():
            expert(xga_ref, za_ref)
            gather(t + 2, xgb_ref)
            combine(t, zb_ref)

        return carry

    lax.fori_loop(0, lax.shift_right_logical(n_chunks, 1), tick_pair, 0)

    @pl.when((n_chunks & 1) == 1)
    def _():
        combine(n_chunks - 1, za_ref)

    @pl.when(jnp.logical_and(n_chunks > 0, (n_chunks & 1) == 0))
    def _():
        combine(n_chunks - 1, zb_ref)


def _moe_call(l, h2t, ids, sw, seg_off, n_gran, w):
    smem = lambda n: pl.BlockSpec((MOE_TILES * n,), lambda j, e: (j,), memory_space=pltpu.SMEM,
                                  pipeline_mode=pl.Buffered(1))
    blk = pl.BlockSpec((MOE_TOK, SUB, LANES), lambda j, e: (j, 0, 0), pipeline_mode=pl.Buffered(1))
    return pl.pallas_call(
        _moe_kernel,
        grid=(N_MOE_BLOCKS, N_EXPERTS),
        in_specs=[
            smem(SLOTS), smem(SLOTS), smem(LANES), smem(LANES),
            blk,
            pl.BlockSpec((None, None, D_MODEL, 2 * D_EXPERT), lambda j, e: (l, e, 0, 0)),
            pl.BlockSpec((None, None, 1, 2 * D_EXPERT), lambda j, e: (l, e, 0, 0)),
            pl.BlockSpec((None, None, D_EXPERT, D_MODEL), lambda j, e: (l, e, 0, 0)),
            pl.BlockSpec((None, None, 1, D_MODEL), lambda j, e: (l, e, 0, 0)),
        ],
        out_specs=blk,
        out_shape=jax.ShapeDtypeStruct((N_TOK, SUB, LANES), F32),
        scratch_shapes=[pltpu.VMEM((MOE_CHUNK * SUB, LANES), F32)] * 4
        + [pltpu.SMEM((MAX_GRAN + 2 * CHUNK_GRAN,), jnp.int32)],
        compiler_params=pltpu.CompilerParams(
            dimension_semantics=("arbitrary", "arbitrary"), vmem_limit_bytes=VMEM_LIMIT),
        name="moe",
    )(ids.reshape(-1), sw.reshape(-1), seg_off.reshape(-1), n_gran.reshape(-1),
      h2t.reshape(N_TOK, SUB, LANES), w["exp_w1"], w["exp_b1"], w["exp_w2"], w["exp_b2"]
      ).reshape(N_TOK * SUB, LANES)


def _axial_tables(n_tokens, dim):
    rows = n_tokens // GRID_W
    row = jnp.repeat(jnp.arange(rows), GRID_W)
    col = jnp.tile(jnp.arange(GRID_W), rows)
    half = dim // 2
    inv = 1.0 / (ROPE_BASE ** (jnp.arange(0, half, 2, dtype=F32) / half))

    def axis_angles(pos):
        ang = pos.astype(F32)[:, None] * inv[None, :]
        return jnp.concatenate([ang, ang], axis=-1)

    ang = jnp.concatenate([axis_angles(row), axis_angles(col)], axis=-1)
    return jnp.cos(ang), jnp.sin(ang)


def _rope_slot_tables(dim, lane0, copies):
    cos, sin = _axial_tables(DEC_SEQ, dim)
    quarter = dim // 4
    sign = jnp.where((jnp.arange(dim) % (dim // 2)) < quarter, -1.0, 1.0)
    sin = sin * sign
    cos_slot = jnp.ones((DEC_SEQ, LANES), F32)
    sin_slot = jnp.zeros((DEC_SEQ, LANES), F32)
    for c in range(copies):
        cos_slot = cos_slot.at[:, lane0 + c * dim:lane0 + (c + 1) * dim].set(cos)
        sin_slot = sin_slot.at[:, lane0 + c * dim:lane0 + (c + 1) * dim].set(sin)
    ident = (jnp.ones((1, TILE, LANES), F32), jnp.zeros((1, TILE, LANES), F32))
    cos_t = jnp.concatenate([ident[0], cos_slot.reshape(LAT_TILES_PER_SEQ, TILE, LANES)], axis=0)
    sin_t = jnp.concatenate([ident[1], sin_slot.reshape(LAT_TILES_PER_SEQ, TILE, LANES)], axis=0)
    return cos_t, sin_t


def _half_swap(block):
    dst = jnp.arange(2 * LANES)
    src = jnp.where((dst % block) < block // 2, dst + block // 2, dst - block // 2)
    return (jnp.arange(2 * LANES)[:, None] == src[None, :]).astype(BF16)


def _pad_last(a, width):
    return jnp.pad(a, [(0, 0)] * (a.ndim - 1) + [(0, width - a.shape[-1])])


def _prepare(p):
    L = DEPTH
    w = {}
    w_in = p["w_in"]
    kr_slot = jnp.pad(w_in[:, :, OFF_KVA + KV_LORA:OFF_LRU_X], ((0, 0), (0, 0), (MLA_NOPE, LANES - MLA_QK)))
    w["w_in_a"] = w_in[:, :, OFF_QA:OFF_KVA + KV_LORA].astype(BF16)
    w["w_in_kr"] = kr_slot.astype(BF16)
    w["w_in_c"] = w_in[:, :, OFF_LRU_X:IN_COLS].astype(BF16)
    w["wqb"] = _pad_last(p["w_q_b"].reshape(L, Q_LORA, MLA_HEADS, MLA_QK), LANES).reshape(
        L, Q_LORA, MLA_HEADS * LANES).astype(BF16)
    kvb = p["w_kv_b"].reshape(L, KV_LORA, MLA_HEADS, MLA_NOPE + MLA_V)
    w["wkvk"] = _pad_last(kvb[..., :MLA_NOPE], LANES).reshape(L, KV_LORA, MLA_HEADS * LANES).astype(BF16)
    w["wkvv"] = kvb[..., MLA_NOPE:].reshape(L, KV_LORA, MLA_HEADS * MLA_V).astype(BF16)
    row = lambda a: a[:, None, :]
    w["norm1"] = row(p["norm1_g"])
    w["norm2"] = row(p["norm2_g"])
    w["qa_norm"] = row(p["mla_qa_norm"])
    w["kva_norm"] = row(p["mla_kva_norm"])
    w["qn"] = row(_pad_last(p["mla_qn"], LANES))
    w["kn"] = row(_pad_last(p["mla_kn"], LANES))
    w["dqn"] = row(jnp.tile(p["diff_qn"], (1, 2)))
    w["dkn"] = row(jnp.tile(p["diff_kn"], (1, 2)))
    w["cos_m"], w["sin_m"] = _rope_slot_tables(MLA_ROPE, MLA_NOPE, 1)
    w["cos_d"], w["sin_d"] = _rope_slot_tables(DIFF_DH, 0, 2)
    w["swap_m"] = _half_swap(MLA_ROPE // 2)
    w["swap_d"] = _half_swap(DIFF_DH // 2)
    w["conv_w"] = p["lru_conv_w"]
    w["conv_b"] = row(p["lru_conv_b"])
    per = (D_RNN // 2) // LRU_BW
    gw = p["lru_gate_w"].reshape(L, 4, 2, per, LRU_BW, LRU_BW)
    eye = jnp.eye(per, dtype=F32)
    w["wbd"] = jnp.einsum("lkhacd,ab->lkhacbd", gw, eye).reshape(L, 4, 2, D_RNN // 2, D_RNN // 2).astype(BF16)
    w["gate_b"] = p["lru_gate_b"].reshape(L, 4, D_RNN)
    w["lam"] = p["lru_lambda"]
    w["w_o_mla"] = p["w_o_mla"].astype(BF16)
    w["w_o_lru"] = p["w_o_lru"].astype(BF16)
    w["w_o_diff"] = p["w_o_diff"].astype(BF16)
    w["w_merge"] = p["w_merge"].astype(BF16)
    w["b_merge"] = row(p["b_merge"])
    w["w_out"] = p["w_out"].astype(BF16)
    w["router_w"] = _pad_last(p["router_w"], LANES)
    w["router_b"] = row(_pad_last(p["router_b"], LANES))
    w["exp_w1"] = p["exp_w1"]
    w["exp_b1"] = p["exp_b1"][:, :, None, :]
    w["exp_w2"] = p["exp_w2"]
    w["exp_b2"] = p["exp_b2"][:, :, None, :]
    w["diff_lambda"] = p["diff_lambda"]
    w["diff_subln"] = row(p["diff_subln"])
    return w


def kernel(x_prompt, x_sample, cache_mla_ckv, cache_mla_krope, state_lru, cache_diff_k, cache_diff_v,
           c, c_ctx, ada_w, ada_b, norm1_g, norm2_g, w_in, mla_qa_norm, w_q_b, mla_kva_norm, w_kv_b,
           mla_qn, mla_kn, w_o_mla, lru_conv_w, lru_conv_b, lru_gate_w, lru_gate_b, lru_lambda, w_o_lru,
           diff_qn, diff_kn, diff_lambda, diff_subln, w_o_diff, w_merge, b_merge, w_out,
           router_w, router_b, exp_w1, exp_b1, exp_w2, exp_b2):
    params = dict(
        norm1_g=norm1_g, norm2_g=norm2_g, w_in=w_in, mla_qa_norm=mla_qa_norm, w_q_b=w_q_b,
        mla_kva_norm=mla_kva_norm, w_kv_b=w_kv_b, mla_qn=mla_qn, mla_kn=mla_kn, w_o_mla=w_o_mla,
        lru_conv_w=lru_conv_w, lru_conv_b=lru_conv_b, lru_gate_w=lru_gate_w, lru_gate_b=lru_gate_b,
        lru_lambda=lru_lambda, w_o_lru=w_o_lru, diff_qn=diff_qn, diff_kn=diff_kn, diff_lambda=diff_lambda,
        diff_subln=diff_subln, w_o_diff=w_o_diff, w_merge=w_merge, b_merge=b_merge, w_out=w_out,
        router_w=router_w, router_b=router_b, exp_w1=exp_w1, exp_b1=exp_b1, exp_w2=exp_w2, exp_b2=exp_b2)
    w = _prepare(params)

    cond = jnp.concatenate(
        [c, c_ctx[None, :], jnp.zeros((COND_ROWS - DEC_BATCH - 1, D_MODEL), F32)], axis=0)
    mod = _ada_call(cond, ada_w, ada_b[:, None, :])
    mod3 = mod.reshape(DEPTH * COND_ROWS, 1, 6 * D_MODEL)

    place = jnp.pad(jnp.eye(MLA_ROPE, dtype=F32), ((0, 0), (MLA_NOPE, LANES - MLA_QK))).astype(BF16)
    k_ctx, v_ctx = _kvx_call(cache_mla_ckv, cache_mla_krope, w["wkvk"], w["wkvv"], w["kn"], place)
    dk_ctx = cache_diff_k.reshape(DEC_BATCH, DEPTH, PAST_LEN, 512)
    dv_ctx = cache_diff_v.reshape(DEC_BATCH, DEPTH, PAST_LEN, 512)

    x = jnp.concatenate([x_prompt.reshape(N_CTX_TOK, D_MODEL), x_sample.reshape(-1, D_MODEL)], axis=0)
    h0_ctx = jnp.zeros((BATCH, 2, D_RNN), F32)
    lat_blk = N_CTX_TOK // DEC_SEQ
    new_ckv, new_krope, new_lru, new_dk, new_dv = [], [], [], [], []
    moe_out = None
    for l in range(DEPTH):
        lam_init = 0.8 - 0.6 * math.exp(-0.3 * l)
        x, (qm, km, vm, qd, kd, vd, lx, lg, ckv_o, kro_o, dk_o, dv_o) = _front_call(l, x, moe_out, mod3, w)

        u_ctx, hf_ctx = _lru_call(l, lx, lg, h0_ctx, w, BATCH, SEQ, 0)
        u_lat, _ = _lru_call(l, lx, lg, state_lru[:, l], w, DEC_BATCH, DEC_SEQ, lat_blk)

        om_ctx = _attn_call(_mla_attn_kernel, "mla_ctx", qm, [(km, vm, "tok")], [], [], 0, BATCH, 1, 512)
        om_lat = _attn_call(_mla_attn_kernel, "mla_lat", qm,
                            [(k_ctx, v_ctx, lambda b: (l, b, 0, 0)), (km, vm, "tok")], [], [],
                            lat_blk, DEC_BATCH, LAT_TILES_PER_SEQ, 512)

        dk_extra = [w["diff_lambda"], w["diff_subln"]]
        dk_specs = lambda: [pl.BlockSpec((None, 4, DIFF_DH), lambda b: (l, 0, 0)),
                            pl.BlockSpec((None, 1, LANES), lambda b: (l, 0, 0))]
        dkern = lambda n_seg, *refs: _diff_attn_kernel(n_seg, lam_init, *refs)
        od_ctx = _attn_call(dkern, "diff_ctx", qd, [(kd, vd, "tok")], dk_extra, dk_specs(),
                            0, BATCH, 1, 512)
        od_lat = _attn_call(dkern, "diff_lat", qd,
                            [(dk_ctx, dv_ctx, lambda b: (b, l, 0, 0)), (kd, vd, "tok")], dk_extra, dk_specs(),
                            lat_blk, DEC_BATCH, LAT_TILES_PER_SEQ, 512)

        x, h2t, ids, sw, seg_off, n_gran = _back_call(
            l, x, mod3, (om_ctx, om_lat), (u_ctx, u_lat), (od_ctx, od_lat), w)
        moe_out = _moe_call(l, h2t, ids, sw, seg_off, n_gran, w)

        new_ckv.append(ckv_o.reshape(BATCH, SEQ, KV_LORA))
        new_krope.append(kro_o.reshape(BATCH, SEQ, MLA_ROPE))
        new_lru.append(hf_ctx)
        new_dk.append(dk_o.reshape(BATCH, SEQ, DIFF_HEADS, 2, DIFF_DH))
        new_dv.append(dv_o.reshape(BATCH, SEQ, DIFF_HEADS, DIFF_DV))

    x_ctx, x_lat = _residual_call(DEPTH - 1, x, moe_out, mod3)
    xp = x_ctx.reshape(BATCH, SEQ, D_MODEL)
    xs = x_lat.reshape(DEC_BATCH, DEC_SEQ, D_MODEL)
    return (xp, xs, jnp.stack(new_ckv, axis=1), jnp.stack(new_krope, axis=1), jnp.stack(new_lru, axis=1),
            jnp.stack(new_dk, axis=1), jnp.stack(new_dv, axis=1))
```
